```python
import math
import jax, jax.numpy as jnp
from jax import lax
import numpy as np

D_MODEL = 1024
BATCH = 1
SEQ = 16384
DEPTH = 1
DEC_BATCH = 32
DEC_SEQ = 1
PAST_LEN = 16384
PAGE_SIZE = 128

MIX_WIDTH = D_MODEL
ATTN_WIDTH = D_MODEL // 2
SSM_WIDTH = MIX_WIDTH - ATTN_WIDTH
HEAD_DIM = 64
N_HEADS = ATTN_WIDTH // HEAD_DIM
MOBA_BLOCK = 256
MOBA_TOPK = 3
Q_CHUNK = 128
SSM_GROUP = 16
N_SSM_GROUPS = SSM_WIDTH // SSM_GROUP
SSM_STATE = 64
D_FF = ((8 * D_MODEL // 3) + 127) // 128 * 128
CONV_W = 3
RMS_EPS = 1e-6
NEG_INF = -1e30
DT_MIN = 1e-3
DT_MAX = 1e-1

kernel_name = 'hymba_s5_moba_convffn_step'


def rmsnorm(x, g):
    xf = x.astype(jnp.float32)
    y = xf * lax.rsqrt(jnp.mean(xf * xf, axis=-1, keepdims=True) + RMS_EPS) * g.astype(jnp.float32)
    return y.astype(x.dtype)


def alibi_slopes():
    return jnp.exp2(-8.0 * jnp.arange(1, N_HEADS + 1, dtype=jnp.float32) / N_HEADS)


def pad_to_blocks(t):
    npad = (-t.shape[1]) % MOBA_BLOCK
    return jnp.pad(t, ((0, 0), (0, npad), (0, 0), (0, 0)))


def moba_attention(q, k_all, v_all, q_pos):
    B, Lq, H, Dh = q.shape
    nb = k_all.shape[1] // MOBA_BLOCK
    kb = k_all.reshape(B, nb, MOBA_BLOCK, H, Dh)
    vb = v_all.reshape(B, nb, MOBA_BLOCK, H, Dh)
    kmean = jnp.mean(kb.astype(jnp.float32), axis=2)
    topk = min(MOBA_TOPK, nb)
    slopes = alibi_slopes()
    scale = HEAD_DIM ** -0.5
    bi = jnp.arange(B)[:, None, None, None]
    hi = jnp.arange(H)[None, :, None, None]
    offs = jnp.arange(MOBA_BLOCK, dtype=jnp.int32)
    blk_ids = jnp.arange(nb, dtype=jnp.int32)
    qc_len = math.gcd(Lq, Q_CHUNK)
    nc = Lq // qc_len

    def chunk(args):
        qc, pos = args
        qf = qc.astype(jnp.float32)
        own = pos // MOBA_BLOCK
        gate = jnp.einsum('bqhd,bnhd->bhqn', qf, kmean)
        gate = jnp.where(blk_ids[None, :] < own[:, None], gate, NEG_INF)
        _, top = lax.top_k(gate, topk)
        own_b = jnp.broadcast_to(own[None, None, :, None], (B, H, qc_len, 1)).astype(jnp.int32)
        blk = jnp.concatenate([top.astype(jnp.int32), own_b], axis=-1)
        valid = jnp.concatenate([top < own[None, None, :, None], jnp.ones(own_b.shape, bool)], axis=-1)
        kg = kb[bi, blk, :, hi].astype(jnp.float32)
        vg = vb[bi, blk, :, hi].astype(jnp.float32)
        s = jnp.einsum('bqhd,bhqjsd->bhqjs', qf, kg) * scale
        kpos = blk[..., None] * MOBA_BLOCK + offs
        dist = pos[None, None, :, None, None] - kpos
        s = s - slopes[None, :, None, None, None] * dist.astype(jnp.float32)
        s = jnp.where(valid[..., None] & (dist >= 0), s, NEG_INF)
        J = blk.shape[-1]
        p = jax.nn.softmax(s.reshape(B, H, qc_len, J * MOBA_BLOCK), axis=-1).reshape(s.shape)
        o = jnp.einsum('bhqjs,bhqjsd->bqhd', p, vg)
        return o.astype(q.dtype)

    q_chunks = q.reshape(B, nc, qc_len, H, Dh).swapaxes(0, 1)
    pos_chunks = q_pos.reshape(nc, qc_len)
    out = lax.map(chunk, (q_chunks, pos_chunks))
    return out.swapaxes(0, 1).reshape(B, Lq, H * Dh)


def s5_mixer(u, s0_re, s0_im, a_re, a_im, log_step, b_re, b_im, c_re, c_im, d, w_glu, b_glu):
    Bn, L, _ = u.shape
    f32 = jnp.float32
    uf = u.astype(f32).reshape(Bn, L, N_SSM_GROUPS, SSM_GROUP)
    a_re = a_re.astype(f32)
    a_im = a_im.astype(f32)
    dt = jnp.exp(log_step.astype(f32))[:, None]
    mag = jnp.exp(a_re * dt)
    abar_re = mag * jnp.cos(a_im * dt)
    abar_im = mag * jnp.sin(a_im * dt)
    den = a_re * a_re + a_im * a_im
    nr = abar_re - 1.0
    ni = abar_im
    coef_re = (nr * a_re + ni * a_im) / den
    coef_im = (ni * a_re - nr * a_im) / den
    b_re = b_re.astype(f32)
    b_im = b_im.astype(f32)
    bb_re = coef_re[..., None] * b_re - coef_im[..., None] * b_im
    bb_im = coef_re[..., None] * b_im + coef_im[..., None] * b_re
    bu_re = jnp.einsum('blgp,gnp->blgn', uf, bb_re)
    bu_im = jnp.einsum('blgp,gnp->blgn', uf, bb_im)
    ar0 = jnp.broadcast_to(abar_re, bu_re.shape)
    ai0 = jnp.broadcast_to(abar_im, bu_re.shape)

    def combine(e1, e2):
        a1r, a1i, b1r, b1i = e1
        a2r, a2i, b2r, b2i = e2
        return (a2r * a1r - a2i * a1i, a2r * a1i + a2i * a1r,
                a2r * b1r - a2i * b1i + b2r, a2r * b1i + a2i * b1r + b2i)

    ar, ai, br, bi_ = lax.associative_scan(combine, (ar0, ai0, bu_re, bu_im), axis=1)
    s0r = s0_re.astype(f32)[:, None]
    s0i = s0_im.astype(f32)[:, None]
    s_re = br + ar * s0r - ai * s0i
    s_im = bi_ + ar * s0i + ai * s0r
    y = (jnp.einsum('blgn,gpn->blgp', s_re, c_re.astype(f32))
         - jnp.einsum('blgn,gpn->blgp', s_im, c_im.astype(f32))
         + d.astype(f32) * uf).reshape(Bn, L, SSM_WIDTH)
    z = jax.nn.gelu(y)
    out = z * jax.nn.sigmoid(z @ w_glu.astype(f32) + b_glu.astype(f32))
    return out.astype(u.dtype), s_re[:, -1], s_im[:, -1]


def conv_ffn(h, buf, w_gate, w_up, conv_w, conv_b, w_down):
    g = h @ w_gate
    up = h @ w_up
    gext = jnp.concatenate([buf.astype(g.dtype), g], axis=1)
    gc = lax.conv_general_dilated(gext, conv_w.astype(g.dtype)[:, None, :], window_strides=(1,),
                                  padding='VALID', dimension_numbers=('NWC', 'WIO', 'NWC'),
                                  feature_group_count=g.shape[-1])
    gc = gc.astype(jnp.float32) + conv_b.astype(jnp.float32)
    act = (jax.nn.gelu(gc) * up.astype(jnp.float32)).astype(h.dtype)
    return act @ w_down, gext[:, -(CONV_W - 1):]


def layer(x, q_pos, k_prefix, v_prefix, s0_re, s0_im, conv_buf, lp):
    Bn, L, _ = x.shape
    h = rmsnorm(x, lp['norm_mix_pre'])
    proj = h @ lp['w_in']
    q = proj[..., :ATTN_WIDTH].reshape(Bn, L, N_HEADS, HEAD_DIM)
    k = proj[..., ATTN_WIDTH:2 * ATTN_WIDTH].reshape(Bn, L, N_HEADS, HEAD_DIM)
    v = proj[..., 2 * ATTN_WIDTH:3 * ATTN_WIDTH].reshape(Bn, L, N_HEADS, HEAD_DIM)
    u = proj[..., 3 * ATTN_WIDTH:]
    k_all = pad_to_blocks(jnp.concatenate([k_prefix.astype(k.dtype), k], axis=1))
    v_all = pad_to_blocks(jnp.concatenate([v_prefix.astype(v.dtype), v], axis=1))
    attn = moba_attention(q, k_all, v_all, q_pos)
    ssm, s_re, s_im = s5_mixer(u, s0_re, s0_im, lp['ssm_a_re'], lp['ssm_a_im'], lp['ssm_log_step'],
                               lp['ssm_b_re'], lp['ssm_b_im'], lp['ssm_c_re'], lp['ssm_c_im'], lp['ssm_d'],
                               lp['w_glu'], lp['b_glu'])
    mix = jnp.concatenate([attn, ssm.astype(attn.dtype)], axis=-1) @ lp['w_out']
    x = x + rmsnorm(mix, lp['norm_mix_post'])
    f, conv_new = conv_ffn(rmsnorm(x, lp['norm_ffn_pre']), conv_buf, lp['w_gate'], lp['w_up'],
                           lp['conv_w'], lp['conv_b'], lp['w_down'])
    x = x + rmsnorm(f, lp['norm_ffn_post'])
    return x, k, v, s_re, s_im, conv_new


def setup_inputs(seed: int = 0) -> dict:
    key = jax.random.key(seed)
    ks = jax.random.split(key, 32)
    f32 = jnp.float32
    n_pages = PAST_LEN // PAGE_SIZE
    n_used = DEC_BATCH * n_pages
    n_pool = n_used + max(1, n_used // 4)
    nrm = lambda k, shape, s=1.0: jax.random.normal(k, shape, f32) * s
    qkvu = 3 * ATTN_WIDTH + SSM_WIDTH
    a_im = (math.pi * jnp.arange(SSM_STATE, dtype=f32))[None, None, :] + nrm(ks[12], (DEPTH, N_SSM_GROUPS, SSM_STATE), 0.01)
    return {
        'x_prompt': nrm(ks[0], (BATCH, SEQ, D_MODEL)),
        'x_sample': nrm(ks[1], (DEC_BATCH, DEC_SEQ, D_MODEL)),
        'cache_k': nrm(ks[2], (DEPTH, n_pool, PAGE_SIZE, N_HEADS, HEAD_DIM)),
        'cache_v': nrm(ks[3], (DEPTH, n_pool, PAGE_SIZE, N_HEADS, HEAD_DIM)),
        'page_table': jax.random.permutation(ks[4], n_pool)[:n_used].reshape(DEC_BATCH, n_pages).astype(jnp.int32),
        'state_ssm_re': nrm(ks[5], (DEPTH, DEC_BATCH, N_SSM_GROUPS, SSM_STATE), 0.5),
        'state_ssm_im': nrm(ks[6], (DEPTH, DEC_BATCH, N_SSM_GROUPS, SSM_STATE), 0.5),
        'state_conv': nrm(ks[7], (DEPTH, DEC_BATCH, CONV_W - 1, D_FF)),
        'norm_mix_pre': 1.0 + nrm(ks[8], (DEPTH, D_MODEL), 0.05),
        'norm_mix_post': 1.0 + nrm(ks[9], (DEPTH, D_MODEL), 0.05),
        'w_in': nrm(ks[10], (DEPTH, D_MODEL, qkvu), D_MODEL ** -0.5),
        'ssm_a_re': -0.5 + nrm(ks[11], (DEPTH, N_SSM_GROUPS, SSM_STATE), 0.01),
        'ssm_a_im': a_im,
        'ssm_log_step': jax.random.uniform(ks[13], (DEPTH, N_SSM_GROUPS), f32, math.log(DT_MIN), math.log(DT_MAX)),
        'ssm_b_re': nrm(ks[14], (DEPTH, N_SSM_GROUPS, SSM_STATE, SSM_GROUP), (2 * SSM_GROUP) ** -0.5),
        'ssm_b_im': nrm(ks[15], (DEPTH, N_SSM_GROUPS, SSM_STATE, SSM_GROUP), (2 * SSM_GROUP) ** -0.5),
        'ssm_c_re': nrm(ks[16], (DEPTH, N_SSM_GROUPS, SSM_GROUP, SSM_STATE), SSM_STATE ** -0.5),
        'ssm_c_im': nrm(ks[17], (DEPTH, N_SSM_GROUPS, SSM_GROUP, SSM_STATE), SSM_STATE ** -0.5),
        'ssm_d': nrm(ks[18], (DEPTH, N_SSM_GROUPS, SSM_GROUP)),
        'w_glu': nrm(ks[19], (DEPTH, SSM_WIDTH, SSM_WIDTH), SSM_WIDTH ** -0.5),
        'b_glu': nrm(ks[20], (DEPTH, SSM_WIDTH), 0.01),
        'w_out': nrm(ks[21], (DEPTH, MIX_WIDTH, D_MODEL), MIX_WIDTH ** -0.5),
        'norm_ffn_pre': 1.0 + nrm(ks[22], (DEPTH, D_MODEL), 0.05),
        'norm_ffn_post': 1.0 + nrm(ks[23], (DEPTH, D_MODEL), 0.05),
        'w_gate': nrm(ks[24], (DEPTH, D_MODEL, D_FF), D_MODEL ** -0.5),
        'w_up': nrm(ks[25], (DEPTH, D_MODEL, D_FF), D_MODEL ** -0.5),
        'conv_w': nrm(ks[26], (DEPTH, CONV_W, D_FF), CONV_W ** -0.5),
        'conv_b': nrm(ks[27], (DEPTH, D_FF), 0.01),
        'w_down': nrm(ks[28], (DEPTH, D_FF, D_MODEL), D_FF ** -0.5),
    }


def reference(x_prompt, x_sample, cache_k, cache_v, page_table, state_ssm_re, state_ssm_im, state_conv,
              norm_mix_pre, norm_mix_post, w_in, ssm_a_re, ssm_a_im, ssm_log_step, ssm_b_re, ssm_b_im,
              ssm_c_re, ssm_c_im, ssm_d, w_glu, b_glu, w_out, norm_ffn_pre, norm_ffn_post,
              w_gate, w_up, conv_w, conv_b, w_down):
    bp, lp_len = x_prompt.shape[0], x_prompt.shape[1]
    bs, ls_len = x_sample.shape[0], x_sample.shape[1]
    past_len = page_table.shape[1] * cache_k.shape[2]
    pos_p = jnp.arange(lp_len, dtype=jnp.int32)
    pos_s = past_len + jnp.arange(ls_len, dtype=jnp.int32)
    hp, hs = x_prompt, x_sample
    kps, vps, kss, vss, srp, sip, srs, sis, cps, css = [], [], [], [], [], [], [], [], [], []
    for l in range(DEPTH):
        lp = {
            'norm_mix_pre': norm_mix_pre[l], 'norm_mix_post': norm_mix_post[l], 'w_in': w_in[l],
            'ssm_a_re': ssm_a_re[l], 'ssm_a_im': ssm_a_im[l], 'ssm_log_step': ssm_log_step[l],
            'ssm_b_re': ssm_b_re[l], 'ssm_b_im': ssm_b_im[l], 'ssm_c_re': ssm_c_re[l], 'ssm_c_im': ssm_c_im[l],
            'ssm_d': ssm_d[l], 'w_glu': w_glu[l], 'b_glu': b_glu[l], 'w_out': w_out[l],
            'norm_ffn_pre': norm_ffn_pre[l], 'norm_ffn_post': norm_ffn_post[l],
            'w_gate': w_gate[l], 'w_up': w_up[l], 'conv_w': conv_w[l], 'conv_b': conv_b[l], 'w_down': w_down[l],
        }
        empty = jnp.zeros((bp, 0, N_HEADS, HEAD_DIM), hp.dtype)
        zs = jnp.zeros((bp, N_SSM_GROUPS, SSM_STATE), jnp.float32)
        zc = jnp.zeros((bp, CONV_W - 1, D_FF), hp.dtype)
        hp, kp, vp, sr, si, cp = layer(hp, pos_p, empty, empty, zs, zs, zc, lp)
        kps.append(kp); vps.append(vp); srp.append(sr); sip.append(si); cps.append(cp)
        k_past = cache_k[l][page_table].reshape(bs, past_len, N_HEADS, HEAD_DIM)
        v_past = cache_v[l][page_table].reshape(bs, past_len, N_HEADS, HEAD_DIM)
        hs, ks_, vs_, sr, si, cs = layer(hs, pos_s, k_past, v_past, state_ssm_re[l], state_ssm_im[l], state_conv[l], lp)
        kss.append(ks_); vss.append(vs_); srs.append(sr); sis.append(si); css.append(cs)
    return (hp, hs, jnp.stack(kps), jnp.stack(vps), jnp.stack(kss), jnp.stack(vss),
            jnp.stack(srp), jnp.stack(sip), jnp.stack(srs), jnp.stack(sis), jnp.stack(cps), jnp.stack(css))
```

```python
import functools
import math

import jax
import jax.numpy as jnp
from jax import lax
from jax.experimental import pallas as pl
from jax.experimental.pallas import tpu as pltpu

F32 = jnp.float32
BF16 = jnp.bfloat16

D_MODEL = 1024
N_HEADS = 8
HEAD_DIM = 64
ATTN_WIDTH = N_HEADS * HEAD_DIM
SSM_WIDTH = D_MODEL - ATTN_WIDTH
MOBA_BLOCK = 256
MOBA_TOPK = 3
SSM_GROUP = 16
N_SSM_GROUPS = SSM_WIDTH // SSM_GROUP
SSM_STATE = 64
SSM_CHUNK = 16
CONV_W = 3
RMS_EPS = 1e-6
NEG = -1e30
LOG2E = 1.4426950408889634
V7X_VMEM_LIMIT = 56 * 1024 * 1024
HI = lax.Precision.HIGHEST


def _cparams(n_axes, vmem=None):
    return pltpu.CompilerParams(dimension_semantics=("arbitrary",) * n_axes, vmem_limit_bytes=vmem)


def _rms(x, g):
    return x * lax.rsqrt(jnp.mean(x * x, axis=-1, keepdims=True) + RMS_EPS) * g


def _gelu_tanh(x):
    return 0.5 * x * (1.0 + jnp.tanh(math.sqrt(2.0 / math.pi) * (x + 0.044715 * (x * x * x))))


def _sigmoid(x):
    return 1.0 / (1.0 + jnp.exp(-x))


def _split_bf16(a):
    hi = a.astype(BF16)
    lo = (a - hi.astype(F32)).astype(BF16)
    return hi, lo


def _dot(a, b):
    return jnp.dot(a, b, preferred_element_type=F32)


def _dot_nt(a, b):
    return lax.dot_general(a, b, (((1,), (1,)), ((), ())), preferred_element_type=F32)


def _inproj_prompt_kernel(x_ref, g_ref, w_ref, k_ref, v_ref, u_ref, qs_ref, kb_ref, vb_ref, sel_ref,
                          kmean_s, *, tl, nb):
    i = pl.program_id(0)
    bpt = tl // MOBA_BLOCK

    @pl.when(i == 0)
    def _():
        kmean_s[...] = jnp.zeros_like(kmean_s)

    h = _rms(x_ref[...], g_ref[...]).astype(BF16)
    proj = _dot(h, w_ref[...])
    q = proj[:, :ATTN_WIDTH]
    k = proj[:, ATTN_WIDTH:2 * ATTN_WIDTH]
    v = proj[:, 2 * ATTN_WIDTH:3 * ATTN_WIDTH]
    k_ref[...] = k
    v_ref[...] = v
    u_ref[...] = proj[:, 3 * ATTN_WIDTH:].astype(BF16)
    qs_ref[...] = (q * (HEAD_DIM ** -0.5 * LOG2E)).astype(BF16)
    kb_ref[...] = k.astype(BF16)
    vb_ref[...] = v.astype(BF16)

    row = lax.broadcasted_iota(jnp.int32, kmean_s.shape, 0)
    km = kmean_s[...]
    for b in range(bpt):
        kmb = jnp.mean(k[b * MOBA_BLOCK:(b + 1) * MOBA_BLOCK, :], axis=0, keepdims=True)
        km = jnp.where(row == i * bpt + b, kmb, km)
    kmean_s[...] = km

    blk = lax.broadcasted_iota(jnp.int32, (nb, tl), 0).astype(F32)
    own = ((i * tl + lax.broadcasted_iota(jnp.int32, (1, tl), 1)) // MOBA_BLOCK).astype(F32)
    for hd in range(N_HEADS):
        sl = slice(hd * HEAD_DIM, (hd + 1) * HEAD_DIM)
        gate = _dot_nt(km[:, sl].astype(BF16), q[:, sl].astype(BF16))
        cur = jnp.where(blk < own, gate, NEG)
        chosen = jnp.zeros((nb, tl), F32)
        for _ in range(MOBA_TOPK):
            mx = jnp.max(cur, axis=0, keepdims=True)
            first = jnp.min(jnp.where(cur == mx, blk, float(nb)), axis=0, keepdims=True)
            hit = blk == first
            chosen = jnp.where(hit & (mx > 0.5 * NEG), 1.0, chosen)
            cur = jnp.where(hit, NEG, cur)
        sel_ref[hd] = jnp.where(chosen > 0.5, 0.0, NEG)


def _inproj_prompt(x, g, w_bf, tl=512):
    L = x.shape[0]
    nb = L // MOBA_BLOCK
    nt = L // tl
    row_f32 = jax.ShapeDtypeStruct((L, ATTN_WIDTH), F32)
    row_bf = jax.ShapeDtypeStruct((L, ATTN_WIDTH), BF16)
    rows = pl.BlockSpec((tl, ATTN_WIDTH), lambda i: (i, 0))
    return pl.pallas_call(
        functools.partial(_inproj_prompt_kernel, tl=tl, nb=nb),
        grid=(nt,),
        in_specs=[pl.BlockSpec((tl, D_MODEL), lambda i: (i, 0)),
                  pl.BlockSpec((1, D_MODEL), lambda i: (0, 0)),
                  pl.BlockSpec((D_MODEL, 4 * ATTN_WIDTH), lambda i: (0, 0))],
        out_specs=[rows, rows, rows, rows, rows, rows,
                   pl.BlockSpec((N_HEADS, nb, tl), lambda i: (0, 0, i))],
        out_shape=[row_f32, row_f32, row_bf, row_bf, row_bf, row_bf,
                   jax.ShapeDtypeStruct((N_HEADS, nb, L), F32)],
        scratch_shapes=[pltpu.VMEM((nb, ATTN_WIDTH), F32)],
        compiler_params=_cparams(1, V7X_VMEM_LIMIT),
        name="inproj_prompt",
    )(x, g, w_bf)


def _attn_prompt_kernel(qt_ref, k_ref, vt_ref, sel_ref, slope_ref, o_ref, bias_s):
    i = pl.program_id(1)
    B = MOBA_BLOCK
    slope = slope_ref[0]
    rel = (lax.broadcasted_iota(jnp.int32, (B, B), 1)
           - lax.broadcasted_iota(jnp.int32, (B, B), 0)).astype(F32)
    bias = -slope * rel
    bias_s[...] = bias
    qt = qt_ref[0, 0]

    s = _dot(k_ref[0, i], qt) + bias
    s = jnp.where(rel >= 0.0, s, NEG)
    m = jnp.max(s, axis=0, keepdims=True)
    p = jnp.exp2(s - m)
    l = jnp.sum(p, axis=0, keepdims=True)
    acc = _dot(vt_ref[0, i], p.astype(BF16))

    def body(j, carry):
        m, l, acc = carry
        rowb = sel_ref[0, 0, pl.ds(j, 1), :] - slope * ((i - j) * B).astype(F32)
        s = _dot(k_ref[0, j], qt) + bias_s[...] + rowb
        m_new = jnp.maximum(m, jnp.max(s, axis=0, keepdims=True))
        alpha = jnp.exp2(m - m_new)
        p = jnp.exp2(s - m_new)
        l = alpha * l + jnp.sum(p, axis=0, keepdims=True)
        acc = alpha * acc + _dot(vt_ref[0, j], p.astype(BF16))
        return m_new, l, acc

    m, l, acc = lax.fori_loop(0, i, body, (m, l, acc))
    o_ref[0, 0] = (acc / l).astype(o_ref.dtype)


def _attn_prompt(qt, kb, vt, sel, slopes):
    H, nb, Dh, B = qt.shape
    return pl.pallas_call(
        _attn_prompt_kernel,
        grid=(H, nb),
        in_specs=[pl.BlockSpec((1, 1, Dh, B), lambda h, i: (h, i, 0, 0)),
                  pl.BlockSpec((1, nb, B, Dh), lambda h, i: (h, 0, 0, 0)),
                  pl.BlockSpec((1, nb, Dh, B), lambda h, i: (h, 0, 0, 0)),
                  pl.BlockSpec((1, 1, nb, B), lambda h, i: (h, i, 0, 0)),
                  pl.BlockSpec((1, 1, B), lambda h, i: (h, 0, 0))],
        out_specs=pl.BlockSpec((1, 1, Dh, B), lambda h, i: (h, i, 0, 0)),
        out_shape=jax.ShapeDtypeStruct((H, nb, Dh, B), BF16),
        scratch_shapes=[pltpu.VMEM((B, B), F32)],
        compiler_params=_cparams(2, V7X_VMEM_LIMIT),
        name="attn_prompt",
    )(qt, kb, vt, sel, slopes)


def _ssm_tables(a_re, a_im, log_step, b_re, b_im, c_re, c_im, d):
    G, N = a_re.shape
    P = d.shape[1]
    T = SSM_CHUNK
    dt = jnp.exp(log_step)[:, None]
    j = jnp.arange(T + 1, dtype=F32)[:, None, None]
    mag = jnp.exp(a_re * dt * j)
    pw_re = mag * jnp.cos(a_im * dt * j)
    pw_im = mag * jnp.sin(a_im * dt * j)
    abar_re, abar_im = pw_re[1], pw_im[1]
    den = a_re * a_re + a_im * a_im
    nr = abar_re - 1.0
    ni = abar_im
    coef_re = (nr * a_re + ni * a_im) / den
    coef_im = (ni * a_re - nr * a_im) / den
    bb_re = coef_re[..., None] * b_re - coef_im[..., None] * b_im
    bb_im = coef_re[..., None] * b_im + coef_im[..., None] * b_re
    return dict(pw_re=pw_re, pw_im=pw_im, bb_re=bb_re, bb_im=bb_im, abar_re=abar_re, abar_im=abar_im)


def _ssm_chunk_tables(tb, c_re, c_im, d):
    pw_re, pw_im, bb_re, bb_im = tb["pw_re"], tb["pw_im"], tb["bb_re"], tb["bb_im"]
    T = SSM_CHUNK
    G, N, P = bb_re.shape
    x_re = pw_re[:T, :, :, None] * bb_re[None] - pw_im[:T, :, :, None] * bb_im[None]
    x_im = pw_re[:T, :, :, None] * bb_im[None] + pw_im[:T, :, :, None] * bb_re[None]
    kj = (jnp.einsum("gpn,jgnq->jgpq", c_re, x_re, precision=HI)
          - jnp.einsum("gpn,jgnq->jgpq", c_im, x_im, precision=HI))
    kj = jnp.concatenate([kj, jnp.zeros_like(kj[:1])], axis=0)
    lag = jnp.arange(T)[None, :] - jnp.arange(T)[:, None]
    m = kj[jnp.where(lag >= 0, lag, T)]
    m = jnp.transpose(m, (2, 0, 4, 1, 3))
    eye = (jnp.eye(T, dtype=F32)[:, None, :, None] * jnp.eye(P, dtype=F32)[None, :, None, :])
    m = m + eye[None] * d[:, None, :, None, None]
    m = m.reshape(G, T * P, T * P)
    f_re = jnp.transpose(x_re[::-1], (1, 0, 3, 2)).reshape(G, T * P, N)
    f_im = jnp.transpose(x_im[::-1], (1, 0, 3, 2)).reshape(G, T * P, N)
    ar = jnp.transpose(pw_re[1:T + 1], (1, 2, 0))[:, :, :, None]
    ai = jnp.transpose(pw_im[1:T + 1], (1, 2, 0))[:, :, :, None]
    cr = jnp.transpose(c_re, (0, 2, 1))[:, :, None, :]
    ci = jnp.transpose(c_im, (0, 2, 1))[:, :, None, :]
    e_re = (cr * ar - ci * ai).reshape(G, N, T * P)
    e_im = (-(cr * ai + ci * ar)).reshape(G, N, T * P)

    def pair_diag(x):
        r, c = x.shape[1:]
        x = x.reshape(G // 2, 2, r, c)
        z = jnp.zeros((G // 2, r, c), F32)
        return jnp.concatenate([jnp.concatenate([x[:, 0], z], axis=2),
                                jnp.concatenate([z, x[:, 1]], axis=2)], axis=1)

    m_pair = pair_diag(m).astype(BF16)
    f_pair = jnp.concatenate([pair_diag(f_re), pair_diag(f_im)], axis=2).astype(BF16)
    e_pair = jnp.concatenate([pair_diag(e_re), pair_diag(e_im)], axis=1).astype(BF16)
    a16_re = pw_re[T].reshape(1, G * N)
    a16_im = pw_im[T].reshape(1, G * N)
    return m_pair, f_pair, e_pair, a16_re, a16_im


def _ssm_chunk_in_kernel(u_ref, f_ref, bre_ref, bim_ref):
    b = _dot(u_ref[0], f_ref[0])
    bre_ref[...] = b[:, :2 * SSM_STATE]
    bim_ref[...] = b[:, 2 * SSM_STATE:]


def _ssm_scan_kernel(bre_ref, bim_ref, are_ref, aim_ref, sre_ref, sim_ref, fre_ref, fim_ref):
    nc = bre_ref.shape[0]
    ar = are_ref[...]
    ai = aim_ref[...]

    def body(c8, carry):
        sr, si = carry
        r0 = pl.multiple_of(c8 * 8, 8)
        br = bre_ref[pl.ds(r0, 8), :]
        bi = bim_ref[pl.ds(r0, 8), :]
        rows_r, rows_i = [], []
        for r in range(8):
            rows_r.append(sr)
            rows_i.append(si)
            sr, si = (ar * sr - ai * si + br[r:r + 1, :], ar * si + ai * sr + bi[r:r + 1, :])
        sre_ref[pl.ds(r0, 8), :] = jnp.concatenate(rows_r, axis=0)
        sim_ref[pl.ds(r0, 8), :] = jnp.concatenate(rows_i, axis=0)
        return sr, si

    z = jnp.zeros(are_ref.shape, F32)
    sr, si = lax.fori_loop(0, nc // 8, body, (z, z))
    fre_ref[...] = sr
    fim_ref[...] = si


def _ssm_chunk_out_kernel(u_ref, m_ref, e_ref, sre_ref, sim_ref, y_ref):
    s = jnp.concatenate([sre_ref[...], sim_ref[...]], axis=1).astype(BF16)
    y_ref[0] = _dot(u_ref[0], m_ref[0]) + _dot(s, e_ref[0])


def _ssm_prompt(u_pair, m_pair, f_pair, e_pair, a16_re, a16_im):
    GP, nc, W = u_pair.shape
    GN = N_SSM_GROUPS * SSM_STATE
    S2 = 2 * SSM_STATE
    st = jax.ShapeDtypeStruct((nc, GN), F32)
    b_re, b_im = pl.pallas_call(
        _ssm_chunk_in_kernel,
        grid=(GP,),
        in_specs=[pl.BlockSpec((1, nc, W), lambda g: (g, 0, 0)),
                  pl.BlockSpec((1, W, 2 * S2), lambda g: (g, 0, 0))],
        out_specs=[pl.BlockSpec((nc, S2), lambda g: (0, g))] * 2,
        out_shape=[st, st],
        compiler_params=_cparams(1),
        name="ssm_chunk_in",
    )(u_pair, f_pair)
    fin = jax.ShapeDtypeStruct((1, GN), F32)
    s_re, s_im, f_re, f_im = pl.pallas_call(
        _ssm_scan_kernel,
        out_shape=[st, st, fin, fin],
        compiler_params=pltpu.CompilerParams(vmem_limit_bytes=V7X_VMEM_LIMIT),
        name="ssm_scan",
    )(b_re, b_im, a16_re, a16_im)
    y_pair = pl.pallas_call(
        _ssm_chunk_out_kernel,
        grid=(GP,),
        in_specs=[pl.BlockSpec((1, nc, W), lambda g: (g, 0, 0)),
                  pl.BlockSpec((1, W, W), lambda g: (g, 0, 0)),
                  pl.BlockSpec((1, 2 * S2, W), lambda g: (g, 0, 0)),
                  pl.BlockSpec((nc, S2), lambda g: (0, g)),
                  pl.BlockSpec((nc, S2), lambda g: (0, g))],
        out_specs=pl.BlockSpec((1, nc, W), lambda g: (g, 0, 0)),
        out_shape=jax.ShapeDtypeStruct((GP, nc, W), F32),
        compiler_params=_cparams(1),
        name="ssm_chunk_out",
    )(u_pair, m_pair, e_pair, s_re, s_im)
    return y_pair, f_re, f_im


FF_CHUNK = 256


def _mix_and_prenorm(x, attn_bf, y, wglu_ref, bglu_ref, wouta_ref, wouts_ref, gpost_ref, gpre_ref):
    z = _gelu_tanh(y)
    ssm = z * _sigmoid(_dot(z.astype(BF16), wglu_ref[...]) + bglu_ref[...])
    mix = _dot(attn_bf, wouta_ref[...]) + _dot(ssm.astype(BF16), wouts_ref[...])
    x1 = x + _rms(mix, gpost_ref[...])
    h2 = _rms(x1, gpre_ref[...]).astype(BF16)
    return x1, h2


def _ffn_prompt_kernel(x_ref, attn_ref, y_ref, wglu_ref, bglu_ref, wouta_ref, wouts_ref, gpost_ref, gpre_ref,
                       wgate_ref, wup_ref, cw_ref, cb_ref, wdown_ref, gfpost_ref, out_ref, conv_ref, tail_s,
                       *, tl, dff):
    i = pl.program_id(0)

    @pl.when(i == 0)
    def _():
        tail_s[...] = jnp.zeros_like(tail_s)

    x1, h2 = _mix_and_prenorm(x_ref[...], attn_ref[...], y_ref[...], wglu_ref, bglu_ref, wouta_ref,
                              wouts_ref, gpost_ref, gpre_ref)
    row = lax.broadcasted_iota(jnp.int32, (tl, FF_CHUNK), 0)
    f = jnp.zeros((tl, D_MODEL), F32)
    for c in range(dff // FF_CHUNK):
        cs = slice(c * FF_CHUNK, (c + 1) * FF_CHUNK)
        g = _dot(h2, wgate_ref[:, cs])
        up = _dot(h2, wup_ref[:, cs])
        tail = tail_s[c]
        p1 = tail[7:8, :]
        p2 = tail[6:7, :]
        g1 = jnp.where(row == 0, p1, pltpu.roll(g, 1, 0))
        g2 = jnp.where(row == 0, p2, jnp.where(row == 1, p1, pltpu.roll(g, 2, 0)))
        gc = cw_ref[0:1, cs] * g2 + cw_ref[1:2, cs] * g1 + cw_ref[2:3, cs] * g + cb_ref[:, cs]
        act = (_gelu_tanh(gc) * up).astype(BF16)
        f = f + _dot(act, wdown_ref[cs, :])
        tail_s[c] = g[tl - 8:, :]
        conv_ref[:, cs] = g[tl - 8:, :]
    out_ref[...] = x1 + _rms(f, gfpost_ref[...])


def _ffn_sample_kernel(x_ref, attn_ref, y_ref, b0_ref, b1_ref, wglu_ref, bglu_ref, wouta_ref, wouts_ref,
                       gpost_ref, gpre_ref, wgate_ref, wup_ref, cw_ref, cb_ref, wdown_ref, gfpost_ref,
                       out_ref, g_ref, *, dff):
    x1, h2 = _mix_and_prenorm(x_ref[...], attn_ref[...], y_ref[...], wglu_ref, bglu_ref, wouta_ref,
                              wouts_ref, gpost_ref, gpre_ref)
    f = jnp.zeros(x1.shape, F32)
    for c in range(dff // FF_CHUNK):
        cs = slice(c * FF_CHUNK, (c + 1) * FF_CHUNK)
        g = _dot(h2, wgate_ref[:, cs])
        up = _dot(h2, wup_ref[:, cs])
        gc = (cw_ref[0:1, cs] * b0_ref[:, cs] + cw_ref[1:2, cs] * b1_ref[:, cs] + cw_ref[2:3, cs] * g
              + cb_ref[:, cs])
        act = (_gelu_tanh(gc) * up).astype(BF16)
        f = f + _dot(act, wdown_ref[cs, :])
        g_ref[:, cs] = g
    out_ref[...] = x1 + _rms(f, gfpost_ref[...])


def _weight_specs(dff):
    c2 = lambda *_: (0, 0)
    full = lambda r, c: pl.BlockSpec((r, c), c2, pipeline_mode=pl.Buffered(1))
    return [full(SSM_WIDTH, SSM_WIDTH), full(1, SSM_WIDTH), full(ATTN_WIDTH, D_MODEL), full(SSM_WIDTH, D_MODEL),
            full(1, D_MODEL), full(1, D_MODEL), full(D_MODEL, dff), full(D_MODEL, dff), full(CONV_W, dff),
            full(1, dff), full(dff, D_MODEL), full(1, D_MODEL)]


def _ffn_prompt(x, attn_bf, y, weights, tl=512):
    L = x.shape[0]
    dff = weights[6].shape[1]
    rows = lambda w: pl.BlockSpec((tl, w), lambda i: (i, 0))
    return pl.pallas_call(
        functools.partial(_ffn_prompt_kernel, tl=tl, dff=dff),
        grid=(L // tl,),
        in_specs=[rows(D_MODEL), rows(ATTN_WIDTH), rows(SSM_WIDTH)] + _weight_specs(dff),
        out_specs=[rows(D_MODEL), pl.BlockSpec((8, dff), lambda i: (0, 0))],
        out_shape=[jax.ShapeDtypeStruct((L, D_MODEL), F32), jax.ShapeDtypeStruct((8, dff), F32)],
        scratch_shapes=[pltpu.VMEM((dff // FF_CHUNK, 8, FF_CHUNK), F32)],
        compiler_params=_cparams(1, V7X_VMEM_LIMIT),
        name="ffn_prompt",
    )(x, attn_bf, y, *weights)


def _ffn_sample(x, attn_bf, y, buf0, buf1, weights):
    nb = x.shape[0]
    dff = weights[6].shape[1]
    rows = lambda w: pl.BlockSpec((nb, w), lambda i: (0, 0))
    return pl.pallas_call(
        functools.partial(_ffn_sample_kernel, dff=dff),
        grid=(1,),
        in_specs=[rows(D_MODEL), rows(ATTN_WIDTH), rows(SSM_WIDTH), rows(dff), rows(dff)] + _weight_specs(dff),
        out_specs=[rows(D_MODEL), rows(dff)],
        out_shape=[jax.ShapeDtypeStruct((nb, D_MODEL), F32), jax.ShapeDtypeStruct((nb, dff), F32)],
        compiler_params=_cparams(1, V7X_VMEM_LIMIT),
        name="ffn_sample",
    )(x, attn_bf, y, buf0, buf1, *weights)


def _inproj_sample_kernel(x_ref, g_ref, w_ref, o_ref):
    o_ref[...] = _dot(_rms(x_ref[...], g_ref[...]).astype(BF16), w_ref[...])


def _inproj_sample(x, g, w_bf):
    nb = x.shape[0]
    return pl.pallas_call(
        _inproj_sample_kernel,
        out_shape=jax.ShapeDtypeStruct((nb, w_bf.shape[1]), F32),
        compiler_params=pltpu.CompilerParams(vmem_limit_bytes=V7X_VMEM_LIMIT),
        name="inproj_sample",
    )(x, g, w_bf)


PAGES_PER_STEP = 16


def _kmean_kernel(pt_ref, *refs, ppb):
    page_refs, o_ref = refs[:-1], refs[-1]
    rows = []
    for b in range(len(page_refs) // ppb):
        acc = jnp.sum(page_refs[b * ppb][0], axis=0, keepdims=True)
        for r in range(1, ppb):
            acc = acc + jnp.sum(page_refs[b * ppb + r][0], axis=0, keepdims=True)
        rows.append(acc)
    o_ref[0] = jnp.concatenate(rows, axis=0) * (1.0 / MOBA_BLOCK)


def _kmean_paged(cache_k, page_table):
    n_pool, page, W = cache_k.shape
    S, n_pages = page_table.shape
    ppb = MOBA_BLOCK // page
    pps = PAGES_PER_STEP
    bps = pps // ppb
    nb = n_pages // ppb

    def page_spec(r):
        return pl.BlockSpec((1, page, W), lambda s, c, pt: (pt[s * n_pages + c * pps + r], 0, 0))

    return pl.pallas_call(
        functools.partial(_kmean_kernel, ppb=ppb),
        grid_spec=pltpu.PrefetchScalarGridSpec(
            num_scalar_prefetch=1,
            grid=(S, n_pages // pps),
            in_specs=[page_spec(r) for r in range(pps)],
            out_specs=pl.BlockSpec((1, bps, W), lambda s, c, pt: (s, c, 0)),
        ),
        out_shape=jax.ShapeDtypeStruct((S, nb, W), F32),
        compiler_params=_cparams(2),
        name="kmean_paged",
    )(page_table.reshape(-1), *([cache_k] * pps))


def _gate_sample_kernel(q_ref, km_ref, ind_ref, top_ref):
    nb = km_ref.shape[1]
    prod = km_ref[0].astype(BF16).astype(F32) * q_ref[0].astype(BF16).astype(F32)
    p_hi, p_lo = _split_bf16(prod)
    ind = ind_ref[...]
    gate = _dot(p_hi, ind) + _dot(p_lo, ind)
    blk = lax.broadcasted_iota(jnp.int32, gate.shape, 0).astype(F32)
    picks = []
    cur = gate
    for _ in range(MOBA_TOPK):
        mx = jnp.max(cur, axis=0, keepdims=True)
        first = jnp.min(jnp.where(cur == mx, blk, float(nb)), axis=0, keepdims=True)
        picks.append(first)
        cur = jnp.where(blk == first, NEG, cur)
    top_ref[0] = jnp.concatenate(picks, axis=0).astype(jnp.int32)


def _gate_sample(q, kmean):
    S, nb, W = kmean.shape
    ind = (jnp.arange(W)[:, None] // HEAD_DIM == jnp.arange(N_HEADS)[None, :]).astype(BF16)
    return pl.pallas_call(
        _gate_sample_kernel,
        grid=(S,),
        in_specs=[pl.BlockSpec((1, 1, W), lambda s: (s, 0, 0)),
                  pl.BlockSpec((1, nb, W), lambda s: (s, 0, 0)),
                  pl.BlockSpec((W, N_HEADS), lambda s: (0, 0))],
        out_specs=pl.BlockSpec((1, MOBA_TOPK, N_HEADS), lambda s: (s, 0, 0)),
        out_shape=jax.ShapeDtypeStruct((S, MOBA_TOPK, N_HEADS), jnp.int32),
        compiler_params=_cparams(1),
        name="gate_sample",
    )(q.reshape(S, 1, W), kmean, ind)


def _attn_sample_kernel(pt_ref, top_ref, q_ref, kn_ref, vn_ref, slope_ref, *refs, n_sel, page, past_len):
    k_refs, v_refs, o_ref = refs[:n_sel], refs[n_sel:2 * n_sel], refs[-1]
    s_i = pl.program_id(0)
    h = pl.program_id(1)
    lane = lax.broadcasted_iota(jnp.int32, (1, 2 * HEAD_DIM), 1)
    mine = (lane // HEAD_DIM) == (h % 2)
    q = jnp.where(mine, q_ref[0], 0.0)
    q8 = jnp.broadcast_to(q, (8, 2 * HEAD_DIM)).astype(BF16)
    slope = slope_ref[0]
    scale = HEAD_DIM ** -0.5
    ppb = MOBA_BLOCK // page
    off = lax.broadcasted_iota(jnp.int32, (1, page), 1)
    scores = []
    for r in range(n_sel):
        blk = top_ref[(s_i * MOBA_TOPK + r // ppb) * N_HEADS + h]
        kpos = blk * MOBA_BLOCK + (r % ppb) * page + off
        dist = (past_len - kpos).astype(F32)
        s = _dot_nt(q8, k_refs[r][0].astype(BF16))[0:1, :] * scale - slope[:, :page] * dist
        scores.append(s)
    s_own = jnp.sum(q * kn_ref[0], axis=1, keepdims=True) * scale
    m = s_own
    for s in scores:
        m = jnp.maximum(m, jnp.max(s, axis=1, keepdims=True))
    p_own = jnp.exp(s_own - m)
    l = p_own
    acc = p_own * vn_ref[0]
    for r in range(n_sel):
        p = jnp.exp(scores[r] - m)
        l = l + jnp.sum(p, axis=1, keepdims=True)
        p8 = jnp.broadcast_to(p, (8, page)).astype(BF16)
        acc = acc + _dot(p8, v_refs[r][0].astype(BF16))[0:1, :]
    o_ref[0, 0] = jnp.where(mine, acc / l, 0.0)


def _attn_sample(q, k_new, v_new, cache_k, cache_v, page_table, top, slopes_lane):
    n_pool, page, W = cache_k.shape
    S, n_pages = page_table.shape
    ppb = MOBA_BLOCK // page
    n_sel = MOBA_TOPK * ppb
    past_len = n_pages * page
    slab = 2 * HEAD_DIM

    def page_spec(r):
        def imap(s, h, pt, tp):
            blk = tp[(s * MOBA_TOPK + r // ppb) * N_HEADS + h]
            return (pt[s * n_pages + blk * ppb + r % ppb], 0, h // 2)
        return pl.BlockSpec((1, page, slab), imap)

    pair = pl.BlockSpec((1, 1, slab), lambda s, h, pt, tp: (s, 0, h // 2))
    out = pl.pallas_call(
        functools.partial(_attn_sample_kernel, n_sel=n_sel, page=page, past_len=past_len),
        grid_spec=pltpu.PrefetchScalarGridSpec(
            num_scalar_prefetch=2,
            grid=(S, N_HEADS),
            in_specs=[pair, pair, pair, pl.BlockSpec((1, 1, slab), lambda s, h, pt, tp: (h, 0, 0))]
            + [page_spec(r) for r in range(n_sel)] * 2,
            out_specs=pl.BlockSpec((1, 1, 1, slab), lambda s, h, pt, tp: (s, h, 0, 0)),
        ),
        out_shape=jax.ShapeDtypeStruct((S, N_HEADS, 1, slab), F32),
        compiler_params=_cparams(2),
        name="attn_sample",
    )(page_table.reshape(-1), top.reshape(-1), q.reshape(S, 1, W), k_new.reshape(S, 1, W),
      v_new.reshape(S, 1, W), slopes_lane, *([cache_k] * n_sel), *([cache_v] * n_sel))
    out = out.reshape(S, N_HEADS // 2, 2, 2, HEAD_DIM)
    return (out[:, :, 0] + out[:, :, 1]).reshape(S, W)


def _ssm_sample_kernel(u_ref, sre_ref, sim_ref, are_ref, aim_ref, bbre_ref, bbim_ref, cre_ref, cim_ref, d_ref,
                       y_ref, nre_ref, nim_ref):
    u = u_ref[...]
    ub = u.astype(BF16)
    ar, ai = are_ref[...], aim_ref[...]
    s0r, s0i = sre_ref[...], sim_ref[...]
    nr = ar * s0r - ai * s0i + _dot(ub, bbre_ref[...])
    ni = ar * s0i + ai * s0r + _dot(ub, bbim_ref[...])
    nre_ref[...] = nr
    nim_ref[...] = ni
    y_ref[...] = (_dot(nr.astype(BF16), cre_ref[...]) - _dot(ni.astype(BF16), cim_ref[...])
                  + d_ref[...] * u)


def _block_diag(x):
    G, r, c = x.shape
    eye = jnp.eye(G, dtype=x.dtype)
    return (x[:, :, None, :] * eye[:, None, :, None]).reshape(G * r, G * c)


def _ssm_sample(u, s_re, s_im, tb, c_re, c_im, d):
    S = u.shape[0]
    GN = N_SSM_GROUPS * SSM_STATE
    bb_re = _block_diag(jnp.transpose(tb["bb_re"], (0, 2, 1))).astype(BF16)
    bb_im = _block_diag(jnp.transpose(tb["bb_im"], (0, 2, 1))).astype(BF16)
    cc_re = _block_diag(jnp.transpose(c_re, (0, 2, 1))).astype(BF16)
    cc_im = _block_diag(jnp.transpose(c_im, (0, 2, 1))).astype(BF16)
    st = jax.ShapeDtypeStruct((S, GN), F32)
    return pl.pallas_call(
        _ssm_sample_kernel,
        out_shape=[jax.ShapeDtypeStruct((S, SSM_WIDTH), F32), st, st],
        compiler_params=pltpu.CompilerParams(vmem_limit_bytes=V7X_VMEM_LIMIT),
        name="ssm_sample",
    )(u, s_re.reshape(S, GN), s_im.reshape(S, GN), tb["abar_re"].reshape(1, GN), tb["abar_im"].reshape(1, GN),
      bb_re, bb_im, cc_re, cc_im, d.reshape(1, SSM_WIDTH))


def _layer_prompt(x, lw):
    L = x.shape[0]
    nb = L // MOBA_BLOCK
    H, Dh, B = N_HEADS, HEAD_DIM, MOBA_BLOCK
    k, v, u_bf, qs_bf, k_bf, v_bf, sel = _inproj_prompt(x, lw["g_mix_pre"], lw["w_in"])
    qt = jnp.transpose(qs_bf.reshape(nb, B, H, Dh), (2, 0, 3, 1))
    vt = jnp.transpose(v_bf.reshape(nb, B, H, Dh), (2, 0, 3, 1))
    kb = jnp.transpose(k_bf.reshape(nb, B, H, Dh), (2, 0, 1, 3))
    sel4 = jnp.transpose(sel.reshape(H, nb, nb, B), (0, 2, 1, 3))
    ot = _attn_prompt(qt, kb, vt, sel4, lw["slopes_blk"])
    attn_bf = jnp.transpose(ot, (1, 3, 0, 2)).reshape(L, ATTN_WIDTH)

    nc = L // SSM_CHUNK
    GP = N_SSM_GROUPS // 2
    u_pair = jnp.transpose(u_bf.reshape(nc, SSM_CHUNK, GP, 2, SSM_GROUP), (2, 0, 3, 1, 4)).reshape(GP, nc, -1)
    y_pair, f_re, f_im = _ssm_prompt(u_pair, *lw["ssm_chunk"])
    y = jnp.transpose(y_pair.reshape(GP, nc, 2, SSM_CHUNK, SSM_GROUP), (1, 3, 0, 2, 4)).reshape(L, SSM_WIDTH)

    out, conv = _ffn_prompt(x, attn_bf, y, lw["ffn"])
    return (out, k, v, f_re.reshape(N_SSM_GROUPS, SSM_STATE), f_im.reshape(N_SSM_GROUPS, SSM_STATE),
            conv[8 - (CONV_W - 1):])


def _layer_sample(x, cache_k, cache_v, page_table, s_re, s_im, conv_buf, lw):
    S = x.shape[0]
    n_pool, page = cache_k.shape[:2]
    proj = _inproj_sample(x, lw["g_mix_pre"], lw["w_in"])
    q = proj[:, :ATTN_WIDTH]
    k = proj[:, ATTN_WIDTH:2 * ATTN_WIDTH]
    v = proj[:, 2 * ATTN_WIDTH:3 * ATTN_WIDTH]
    u = proj[:, 3 * ATTN_WIDTH:]
    ck = cache_k.reshape(n_pool, page, ATTN_WIDTH)
    cv = cache_v.reshape(n_pool, page, ATTN_WIDTH)
    kmean = _kmean_paged(ck, page_table)
    top = _gate_sample(q, kmean)
    attn = _attn_sample(q, k, v, ck, cv, page_table, top, lw["slopes_lane"])
    y, n_re, n_im = _ssm_sample(u, s_re, s_im, lw["ssm_tb"], lw["c_re"], lw["c_im"], lw["d"])
    out, g = _ffn_sample(x, attn.astype(BF16), y, conv_buf[:, 0], conv_buf[:, 1], lw["ffn"])
    conv_new = jnp.stack([conv_buf[:, 1], g], axis=1)
    return (out, k, v, n_re.reshape(S, N_SSM_GROUPS, SSM_STATE), n_im.reshape(S, N_SSM_GROUPS, SSM_STATE),
            conv_new)


def kernel(x_prompt, x_sample, cache_k, cache_v, page_table, state_ssm_re, state_ssm_im, state_conv,
           norm_mix_pre, norm_mix_post, w_in, ssm_a_re, ssm_a_im, ssm_log_step, ssm_b_re, ssm_b_im,
           ssm_c_re, ssm_c_im, ssm_d, w_glu, b_glu, w_out, norm_ffn_pre, norm_ffn_post,
           w_gate, w_up, conv_w, conv_b, w_down):
    depth = w_in.shape[0]
    bp, lp_len = x_prompt.shape[:2]
    bs, ls_len = x_sample.shape[:2]
    assert bp == 1 and ls_len == 1 and lp_len % 512 == 0
    slopes = jnp.exp2(-8.0 * jnp.arange(1, N_HEADS + 1, dtype=F32) / N_HEADS)
    hp = x_prompt[0]
    hs = x_sample[:, 0]
    outs = [[] for _ in range(10)]
    for l in range(depth):
        tb = _ssm_tables(ssm_a_re[l], ssm_a_im[l], ssm_log_step[l], ssm_b_re[l], ssm_b_im[l],
                         ssm_c_re[l], ssm_c_im[l], ssm_d[l])
        row = lambda a: a[l].reshape(1, -1)
        lw = dict(
            g_mix_pre=row(norm_mix_pre), w_in=w_in[l].astype(BF16),
            slopes_blk=jnp.broadcast_to((slopes * LOG2E)[:, None, None], (N_HEADS, 1, MOBA_BLOCK)),
            slopes_lane=jnp.broadcast_to(slopes[:, None, None], (N_HEADS, 1, 2 * HEAD_DIM)),
            ssm_tb=tb, c_re=ssm_c_re[l], c_im=ssm_c_im[l], d=ssm_d[l],
            ssm_chunk=_ssm_chunk_tables(tb, ssm_c_re[l], ssm_c_im[l], ssm_d[l]),
            ffn=[w_glu[l].astype(BF16), row(b_glu), w_out[l, :ATTN_WIDTH].astype(BF16),
                 w_out[l, ATTN_WIDTH:].astype(BF16), row(norm_mix_post), row(norm_ffn_pre),
                 w_gate[l].astype(BF16), w_up[l].astype(BF16), conv_w[l], row(conv_b),
                 w_down[l].astype(BF16), row(norm_ffn_post)],
        )
        hp, kp, vp, sr, si, cp = _layer_prompt(hp, lw)
        outs[0].append(kp.reshape(bp, lp_len, N_HEADS, HEAD_DIM))
        outs[1].append(vp.reshape(bp, lp_len, N_HEADS, HEAD_DIM))
        outs[4].append(sr[None])
        outs[5].append(si[None])
        outs[8].append(cp[None])
        hs, ks, vs, sr, si, cs = _layer_sample(hs, cache_k[l], cache_v[l], page_table, state_ssm_re[l],
                                               state_ssm_im[l], state_conv[l], lw)
        outs[2].append(ks.reshape(bs, ls_len, N_HEADS, HEAD_DIM))
        outs[3].append(vs.reshape(bs, ls_len, N_HEADS, HEAD_DIM))
        outs[6].append(sr)
        outs[7].append(si)
        outs[9].append(cs)
    return (hp[None], hs[:, None], *[jnp.stack(o) for o in outs])
```

```python
import functools
import math

import jax
import jax.numpy as jnp
from jax import lax
from jax.experimental import pallas as pl
from jax.experimental.pallas import tpu as pltpu

F32 = jnp.float32
BF16 = jnp.bfloat16

D_MODEL = 1024
N_HEADS = 8
HEAD_DIM = 64
ATTN_WIDTH = N_HEADS * HEAD_DIM
SSM_WIDTH = D_MODEL - ATTN_WIDTH
MOBA_BLOCK = 256
MOBA_TOPK = 3
SSM_GROUP = 16
N_SSM_GROUPS = SSM_WIDTH // SSM_GROUP
SSM_STATE = 64
SSM_CHUNK = 16
CONV_W = 3
RMS_EPS = 1e-6
NEG = -1e30
LOG2E = 1.4426950408889634
LANES = 128
V7X_VMEM_LIMIT = 56 * 1024 * 1024
HI = lax.Precision.HIGHEST

ROW_TILE = 512
V_ROWS = 80
SLABS = SSM_WIDTH // LANES
GROUPS_PER_SLAB = LANES // SSM_GROUP


def _cparams(n_axes, vmem=None):
    return pltpu.CompilerParams(dimension_semantics=("arbitrary",) * n_axes, vmem_limit_bytes=vmem)


def _rms(x, g):
    return x * lax.rsqrt(jnp.mean(x * x, axis=-1, keepdims=True) + RMS_EPS) * g


def _gelu_tanh(x):
    return 0.5 * x * (1.0 + jnp.tanh(math.sqrt(2.0 / math.pi) * (x + 0.044715 * (x * x * x))))


def _sigmoid(x):
    return 1.0 / (1.0 + jnp.exp(-x))


def _split_bf16(a):
    hi = a.astype(BF16)
    lo = (a - hi.astype(F32)).astype(BF16)
    return hi, lo


def _round_bf16(a):
    return a.astype(BF16).astype(F32)


def _dot(a, b):
    return jnp.dot(a, b, preferred_element_type=F32)


def _dot_nt(a, b):
    return lax.dot_general(a, b, (((1,), (1,)), ((), ())), preferred_element_type=F32)


def _top_blocks(cur, blk, n):
    picks = []
    for _ in range(MOBA_TOPK):
        mx = jnp.max(cur, axis=0, keepdims=True)
        first = jnp.min(jnp.where(cur == mx, blk, float(n)), axis=0, keepdims=True)
        picks.append((first, mx))
        cur = jnp.where(blk == first, NEG, cur)
    return picks


def _inproj_prompt_kernel(x_ref, g_ref, w_ref, aug_ref, kt_ref, vt_ref, u_ref, qa_ref, ka_ref, va_ref, sel_ref,
                          kmean_s, *, tl, nb):
    i = pl.program_id(0)
    bpt = tl // MOBA_BLOCK
    H, Dh, B = N_HEADS, HEAD_DIM, MOBA_BLOCK

    @pl.when(i == 0)
    def _():
        kmean_s[...] = jnp.zeros_like(kmean_s)

    h = _rms(x_ref[...], g_ref[...]).astype(BF16)
    proj = _dot(h, w_ref[...])
    q = proj[:, :ATTN_WIDTH]
    k = proj[:, ATTN_WIDTH:2 * ATTN_WIDTH]
    v = proj[:, 2 * ATTN_WIDTH:3 * ATTN_WIDTH]
    for s in range(SLABS):
        u_ref[s] = proj[:, 3 * ATTN_WIDTH + s * LANES:3 * ATTN_WIDTH + (s + 1) * LANES]

    kt = k.T
    vt = v.T
    qt = (q * (Dh ** -0.5 * LOG2E)).T.astype(BF16)
    kt_ref[...] = kt.reshape(H, Dh, tl)
    vt_ref[...] = vt.reshape(H, Dh, tl)

    sub = lax.broadcasted_iota(jnp.int32, (Dh, tl), 0)
    ones3 = jnp.where(sub < 3, 1.0, 0.0).astype(BF16)
    kb = k.astype(BF16)
    lane_hi = lax.broadcasted_iota(jnp.int32, (B, LANES), 1) >= Dh
    vtb = vt.astype(BF16)
    ones_rows = jnp.ones((V_ROWS - Dh, B), BF16)
    for hd in range(H):
        qh = qt[hd * Dh:(hd + 1) * Dh, :]
        odd = hd % 2 == 1
        qa_ref[hd] = jnp.concatenate([ones3, qh] if odd else [qh, ones3], axis=0)
        for b in range(bpt):
            slab = kb[b * B:(b + 1) * B, (hd // 2) * LANES:(hd // 2 + 1) * LANES]
            ka_ref[hd, b] = jnp.where(lane_hi == odd, slab, aug_ref[hd])
            va_ref[hd, b] = jnp.concatenate([vtb[hd * Dh:(hd + 1) * Dh, b * B:(b + 1) * B], ones_rows], axis=0)

    row = lax.broadcasted_iota(jnp.int32, kmean_s.shape, 0)
    km = kmean_s[...]
    for b in range(bpt):
        kmb = jnp.mean(k[b * B:(b + 1) * B, :], axis=0, keepdims=True)
        km = jnp.where(row == i * bpt + b, kmb, km)
    kmean_s[...] = km

    blk = lax.broadcasted_iota(jnp.int32, (nb, tl), 0).astype(F32)
    own = ((i * tl + lax.broadcasted_iota(jnp.int32, (1, tl), 1)) // B).astype(F32)
    kmb16 = km.astype(BF16)
    qb16 = q.astype(BF16)
    for hd in range(H):
        sl = slice(hd * Dh, (hd + 1) * Dh)
        gate = _dot_nt(kmb16[:, sl], qb16[:, sl])
        chosen = jnp.zeros((nb, tl), F32)
        for first, mx in _top_blocks(jnp.where(blk < own, gate, NEG), blk, nb):
            chosen = jnp.where((blk == first) & (mx > 0.5 * NEG), 1.0, chosen)
        sel_ref[hd] = jnp.where(chosen > 0.5, 0.0, NEG)


def _inproj_prompt(x, g, w_bf, aug):
    L = x.shape[0]
    tl = ROW_TILE
    nb = L // MOBA_BLOCK
    bpt = tl // MOBA_BLOCK
    H, Dh, B = N_HEADS, HEAD_DIM, MOBA_BLOCK
    tcol = pl.BlockSpec((H, Dh, tl), lambda i: (0, 0, i))
    return pl.pallas_call(
        functools.partial(_inproj_prompt_kernel, tl=tl, nb=nb),
        grid=(L // tl,),
        in_specs=[pl.BlockSpec((tl, D_MODEL), lambda i: (i, 0)),
                  pl.BlockSpec((1, D_MODEL), lambda i: (0, 0)),
                  pl.BlockSpec((D_MODEL, 4 * ATTN_WIDTH), lambda i: (0, 0)),
                  pl.BlockSpec((H, B, LANES), lambda i: (0, 0, 0))],
        out_specs=[tcol, tcol,
                   pl.BlockSpec((SLABS, tl, LANES), lambda i: (0, i, 0)),
                   pl.BlockSpec((H, 2 * Dh, tl), lambda i: (0, 0, i)),
                   pl.BlockSpec((H, bpt, B, LANES), lambda i: (0, i, 0, 0)),
                   pl.BlockSpec((H, bpt, V_ROWS, B), lambda i: (0, i, 0, 0)),
                   pl.BlockSpec((H, nb, tl), lambda i: (0, 0, i))],
        out_shape=[jax.ShapeDtypeStruct((H, Dh, L), F32), jax.ShapeDtypeStruct((H, Dh, L), F32),
                   jax.ShapeDtypeStruct((SLABS, L, LANES), F32),
                   jax.ShapeDtypeStruct((H, 2 * Dh, L), BF16),
                   jax.ShapeDtypeStruct((H, nb, B, LANES), BF16),
                   jax.ShapeDtypeStruct((H, nb, V_ROWS, B), BF16),
                   jax.ShapeDtypeStruct((H, nb, L), F32)],
        scratch_shapes=[pltpu.VMEM((nb, ATTN_WIDTH), F32)],
        compiler_params=_cparams(1, V7X_VMEM_LIMIT),
        name="inproj_prompt",
    )(x, g, w_bf, aug)


HEADS_PER_STEP = 2


def _attn_prompt_kernel(qa_ref, ka_ref, va_ref, sel_ref, slope_ref, o_ref, s_scr, acc_scr, *, tq):
    qi = pl.program_id(1)
    B, Dh = MOBA_BLOCK, HEAD_DIM
    bpq = tq // B
    units = [(e, cb) for e in range(HEADS_PER_STEP) for cb in range(bpq)]
    n_off = qi * bpq + (bpq - 1)
    lane = lax.broadcasted_iota(jnp.int32, (1, B), 1).astype(F32)
    causal = lax.broadcasted_iota(jnp.int32, (B, B), 0) <= lax.broadcasted_iota(jnp.int32, (B, B), 1)

    def q_of(e, cb):
        return qa_ref[e, :, cb * B:(cb + 1) * B]

    def slope_of(e):
        return slope_ref[e, :, :B]

    ms = []
    for u, (e, cb) in enumerate(units):
        own = qi * bpq + cb
        s = _dot(ka_ref[e, own], q_of(e, cb))
        s = jnp.where(causal, s - slope_of(e) * lane, NEG)
        m = jnp.max(s, axis=0, keepdims=True)
        acc_scr[u] = _dot(va_ref[e, own], jnp.exp2(s - m).astype(BF16))
        s_scr[u] = _dot(ka_ref[e, 0], q_of(e, cb))
        ms.append(m)

    def body(j, ms):
        nxt = jnp.minimum(j + 1, n_off - 1)
        out = []
        for u, (e, cb) in enumerate(units):
            s = s_scr[u]
            dist = lane + ((qi * bpq + cb - j) * B).astype(F32)
            col = sel_ref[e, pl.ds(j, 1), cb * B:(cb + 1) * B] - slope_of(e) * dist
            m_new = jnp.maximum(ms[u], jnp.max(s, axis=0, keepdims=True) + col)
            p = jnp.exp2(s - (m_new - col)).astype(BF16)
            acc_scr[u] = jnp.exp2(ms[u] - m_new) * acc_scr[u] + _dot(va_ref[e, j], p)
            s_scr[u] = _dot(ka_ref[e, nxt], q_of(e, cb))
            out.append(m_new)
        return tuple(out)

    lax.fori_loop(0, n_off, body, tuple(ms))
    for cb in range(bpq):
        outs = []
        for e in range(HEADS_PER_STEP):
            acc = acc_scr[e * bpq + cb]
            outs.append((acc[:Dh, :] / acc[Dh:Dh + 1, :]).T)
        o_ref[0, cb * B:(cb + 1) * B, :] = jnp.concatenate(outs, axis=1).astype(o_ref.dtype)


def _attn_prompt(qa, ka, va, sel, slopes):
    H, nb, B, _ = ka.shape
    L = qa.shape[2]
    tq = ROW_TILE
    hp = HEADS_PER_STEP
    return pl.pallas_call(
        functools.partial(_attn_prompt_kernel, tq=tq),
        grid=(H // hp, L // tq),
        in_specs=[pl.BlockSpec((hp, 2 * HEAD_DIM, tq), lambda h, i: (h, 0, i)),
                  pl.BlockSpec((hp, nb, B, LANES), lambda h, i: (h, 0, 0, 0)),
                  pl.BlockSpec((hp, nb, V_ROWS, B), lambda h, i: (h, 0, 0, 0)),
                  pl.BlockSpec((hp, nb, tq), lambda h, i: (h, 0, i)),
                  pl.BlockSpec((hp, 1, tq), lambda h, i: (h, 0, 0))],
        out_specs=pl.BlockSpec((1, tq, hp * HEAD_DIM), lambda h, i: (h, i, 0)),
        out_shape=jax.ShapeDtypeStruct((H // hp, L, hp * HEAD_DIM), BF16),
        scratch_shapes=[pltpu.VMEM((hp * tq // B, B, B), F32), pltpu.VMEM((hp * tq // B, V_ROWS, B), F32)],
        compiler_params=_cparams(2, V7X_VMEM_LIMIT),
        name="attn_prompt",
    )(qa, ka, va, sel, slopes)


def _ssm_tables(a_re, a_im, log_step, b_re, b_im):
    T = SSM_CHUNK
    dt = jnp.exp(log_step)[:, None]
    j = jnp.arange(T + 1, dtype=F32)[:, None, None]
    mag = jnp.exp(a_re * dt * j)
    pw_re = mag * jnp.cos(a_im * dt * j)
    pw_im = mag * jnp.sin(a_im * dt * j)
    abar_re, abar_im = pw_re[1], pw_im[1]
    den = a_re * a_re + a_im * a_im
    nr = abar_re - 1.0
    ni = abar_im
    coef_re = (nr * a_re + ni * a_im) / den
    coef_im = (ni * a_re - nr * a_im) / den
    bb_re = coef_re[..., None] * b_re - coef_im[..., None] * b_im
    bb_im = coef_re[..., None] * b_im + coef_im[..., None] * b_re
    return dict(pw_re=pw_re, pw_im=pw_im, bb_re=bb_re, bb_im=bb_im, abar_re=abar_re, abar_im=abar_im)


def _slab_diag(x):
    lead = x.shape[:-3]
    r, c = x.shape[-2:]
    gs = GROUPS_PER_SLAB
    x = x.reshape(lead + (SLABS, gs, r, c))
    eye = jnp.eye(gs, dtype=x.dtype)
    y = x[..., :, :, None, :] * eye[:, None, :, None]
    return y.reshape(lead + (SLABS, gs * r, gs * c))


def _ssm_chunk_tables(tb, c_re, c_im, d):
    pw_re, pw_im, bb_re, bb_im = tb["pw_re"], tb["pw_im"], tb["bb_re"], tb["bb_im"]
    T = SSM_CHUNK
    G, N, P = bb_re.shape
    x_re = pw_re[:T, :, :, None] * bb_re[None] - pw_im[:T, :, :, None] * bb_im[None]
    x_im = pw_re[:T, :, :, None] * bb_im[None] + pw_im[:T, :, :, None] * bb_re[None]
    kj = (jnp.einsum("gpn,jgnq->jgqp", c_re, x_re, precision=HI)
          - jnp.einsum("gpn,jgnq->jgqp", c_im, x_im, precision=HI))
    kj = kj.at[0].add(jnp.eye(P, dtype=F32)[None] * d[:, :, None])
    kbd = _slab_diag(kj)
    kpad = jnp.concatenate([jnp.zeros_like(kbd[:1]), kbd], axis=0)
    lag = jnp.concatenate([kpad[:T], kpad[1:]], axis=-1)
    lag = jnp.transpose(lag, (1, 0, 2, 3)).reshape(SLABS, T * LANES, 2 * LANES).astype(BF16)
    f_re = _slab_diag(jnp.transpose(x_re[::-1], (0, 1, 3, 2)))
    f_im = _slab_diag(jnp.transpose(x_im[::-1], (0, 1, 3, 2)))
    f = jnp.concatenate([f_re, f_im], axis=-1)
    f = jnp.transpose(f, (1, 0, 2, 3)).reshape(SLABS, T * LANES, 2 * GROUPS_PER_SLAB * N).astype(BF16)
    cr = jnp.transpose(c_re, (0, 2, 1))[None]
    ci = jnp.transpose(c_im, (0, 2, 1))[None]
    ar = pw_re[1:T + 1, :, :, None]
    ai = pw_im[1:T + 1, :, :, None]
    e = jnp.concatenate([_slab_diag(cr * ar - ci * ai), _slab_diag(-(cr * ai + ci * ar))], axis=2)
    e = e.reshape(T // 2, 2, SLABS, e.shape[2], LANES)
    e = jnp.transpose(e, (2, 0, 3, 1, 4)).reshape(SLABS, T // 2, e.shape[3], 2 * LANES).astype(BF16)
    a16_re = pw_re[T].reshape(1, G * N)
    a16_im = pw_im[T].reshape(1, G * N)
    return lag, f, e, a16_re, a16_im


def _chunk_steps(u_ref, rows):
    return [u_ref[0, pl.ds(s, rows, stride=SSM_CHUNK), :].astype(BF16) for s in range(SSM_CHUNK)]


def _ssm_chunk_in_kernel(u_ref, f_ref, bre_ref, bim_ref, *, rows):
    b = _dot(jnp.concatenate(_chunk_steps(u_ref, rows), axis=1), f_ref[0])
    half = b.shape[1] // 2
    bre_ref[...] = b[:, :half]
    bim_ref[...] = b[:, half:]


def _ssm_scan_kernel(bre_ref, bim_ref, are_ref, aim_ref, sre_ref, sim_ref, fre_ref, fim_ref):
    nc = bre_ref.shape[0]
    ar = are_ref[...]
    ai = aim_ref[...]

    def body(c8, carry):
        sr, si = carry
        r0 = pl.multiple_of(c8 * 8, 8)
        br = bre_ref[pl.ds(r0, 8), :]
        bi = bim_ref[pl.ds(r0, 8), :]
        rows_r, rows_i = [], []
        for r in range(8):
            rows_r.append(sr)
            rows_i.append(si)
            sr, si = (ar * sr - ai * si + br[r:r + 1, :], ar * si + ai * sr + bi[r:r + 1, :])
        sre_ref[pl.ds(r0, 8), :] = jnp.concatenate(rows_r, axis=0)
        sim_ref[pl.ds(r0, 8), :] = jnp.concatenate(rows_i, axis=0)
        return sr, si

    z = jnp.zeros(are_ref.shape, F32)
    sr, si = lax.fori_loop(0, nc // 8, body, (z, z))
    fre_ref[...] = sr
    fim_ref[...] = si


def _ssm_chunk_out_kernel(u_ref, lag_ref, e_ref, sre_ref, sim_ref, y_ref, *, rows):
    us = _chunk_steps(u_ref, rows)
    s = jnp.concatenate([sre_ref[...], sim_ref[...]], axis=1).astype(BF16)
    for pair in range(SSM_CHUNK // 2):
        tau = 2 * pair
        lhs = jnp.concatenate(us[tau + 1::-1], axis=1)
        y2 = _dot(lhs, lag_ref[0, :LANES * (tau + 2), :]) + _dot(s, e_ref[0, pair])
        y_ref[0, pl.ds(tau, rows, stride=SSM_CHUNK), :] = y2[:, :LANES]
        y_ref[0, pl.ds(tau + 1, rows, stride=SSM_CHUNK), :] = y2[:, LANES:]


SSM_ROWS = 512


def _ssm_prompt(u4, lag, f, e, a16_re, a16_im):
    L = u4.shape[1]
    nc = L // SSM_CHUNK
    rows = min(SSM_ROWS, nc)
    GN = N_SSM_GROUPS * SSM_STATE
    SW = GROUPS_PER_SLAB * SSM_STATE
    st = jax.ShapeDtypeStruct((nc, GN), F32)
    slab_rows = pl.BlockSpec((1, rows * SSM_CHUNK, LANES), lambda s, r: (s, r, 0))
    state_cols = pl.BlockSpec((rows, SW), lambda s, r: (r, s))
    b_re, b_im = pl.pallas_call(
        functools.partial(_ssm_chunk_in_kernel, rows=rows),
        grid=(SLABS, nc // rows),
        in_specs=[slab_rows, pl.BlockSpec((1,) + f.shape[1:], lambda s, r: (s, 0, 0))],
        out_specs=[state_cols, state_cols],
        out_shape=[st, st],
        compiler_params=_cparams(2, V7X_VMEM_LIMIT),
        name="ssm_chunk_in",
    )(u4, f)
    fin = jax.ShapeDtypeStruct((1, GN), F32)
    s_re, s_im, f_re, f_im = pl.pallas_call(
        _ssm_scan_kernel,
        out_shape=[st, st, fin, fin],
        compiler_params=pltpu.CompilerParams(vmem_limit_bytes=V7X_VMEM_LIMIT),
        name="ssm_scan",
    )(b_re, b_im, a16_re, a16_im)
    y4 = pl.pallas_call(
        functools.partial(_ssm_chunk_out_kernel, rows=rows),
        grid=(SLABS, nc // rows),
        in_specs=[slab_rows,
                  pl.BlockSpec((1,) + lag.shape[1:], lambda s, r: (s, 0, 0)),
                  pl.BlockSpec((1,) + e.shape[1:], lambda s, r: (s, 0, 0, 0)),
                  state_cols, state_cols],
        out_specs=slab_rows,
        out_shape=jax.ShapeDtypeStruct(u4.shape, F32),
        compiler_params=_cparams(2, V7X_VMEM_LIMIT),
        name="ssm_chunk_out",
    )(u4, lag, e, s_re, s_im)
    return y4, f_re, f_im


FF_CHUNK = 256


def _mix_and_prenorm(x, attn_bf, y, wglu_ref, bglu_ref, wouta_ref, wouts_ref, gpost_ref, gpre_ref):
    z = _gelu_tanh(y)
    ssm = z * _sigmoid(_dot(z.astype(BF16), wglu_ref[...]) + bglu_ref[...])
    mix = _dot(attn_bf, wouta_ref[...]) + _dot(ssm.astype(BF16), wouts_ref[...])
    x1 = x + _rms(mix, gpost_ref[...])
    h2 = _rms(x1, gpre_ref[...]).astype(BF16)
    return x1, h2


def _ffn_prompt_kernel(x_ref, attn_ref, y_ref, wglu_ref, bglu_ref, wouta_ref, wouts_ref, gpost_ref, gpre_ref,
                       wgate_ref, wup_ref, cw_ref, cb_ref, wdown_ref, gfpost_ref, out_ref, conv_ref, tail_s,
                       *, tl, dff):
    i = pl.program_id(0)

    @pl.when(i == 0)
    def _():
        tail_s[...] = jnp.zeros_like(tail_s)

    attn = jnp.concatenate([attn_ref[s] for s in range(attn_ref.shape[0])], axis=1)
    y = jnp.concatenate([y_ref[s] for s in range(SLABS)], axis=1)
    x1, h2 = _mix_and_prenorm(x_ref[...], attn, y, wglu_ref, bglu_ref, wouta_ref, wouts_ref, gpost_ref, gpre_ref)
    row = lax.broadcasted_iota(jnp.int32, (tl, FF_CHUNK), 0)
    f = jnp.zeros((tl, D_MODEL), F32)
    for c in range(dff // FF_CHUNK):
        cs = slice(c * FF_CHUNK, (c + 1) * FF_CHUNK)
        g = _dot(h2, wgate_ref[:, cs])
        up = _dot(h2, wup_ref[:, cs])
        tail = tail_s[c]
        p1 = tail[7:8, :]
        p2 = tail[6:7, :]
        g1 = jnp.where(row == 0, p1, pltpu.roll(g, 1, 0))
        g2 = jnp.where(row == 0, p2, jnp.where(row == 1, p1, pltpu.roll(g, 2, 0)))
        gc = cw_ref[0:1, cs] * g2 + cw_ref[1:2, cs] * g1 + cw_ref[2:3, cs] * g + cb_ref[:, cs]
        act = (_gelu_tanh(gc) * up).astype(BF16)
        f = f + _dot(act, wdown_ref[cs, :])
        tail_s[c] = g[tl - 8:, :]
        conv_ref[:, cs] = g[tl - 8:, :]
    out_ref[...] = x1 + _rms(f, gfpost_ref[...])


def _ffn_sample_kernel(x_ref, attn_ref, y_ref, b0_ref, b1_ref, wglu_ref, bglu_ref, wouta_ref, wouts_ref,
                       gpost_ref, gpre_ref, wgate_ref, wup_ref, cw_ref, cb_ref, wdown_ref, gfpost_ref,
                       out_ref, g_ref, *, dff):
    x1, h2 = _mix_and_prenorm(x_ref[...], attn_ref[...], y_ref[...], wglu_ref, bglu_ref, wouta_ref,
                              wouts_ref, gpost_ref, gpre_ref)
    f = jnp.zeros(x1.shape, F32)
    for c in range(dff // FF_CHUNK):
        cs = slice(c * FF_CHUNK, (c + 1) * FF_CHUNK)
        g = _dot(h2, wgate_ref[:, cs])
        up = _dot(h2, wup_ref[:, cs])
        gc = (cw_ref[0:1, cs] * b0_ref[:, cs] + cw_ref[1:2, cs] * b1_ref[:, cs] + cw_ref[2:3, cs] * g
              + cb_ref[:, cs])
        act = (_gelu_tanh(gc) * up).astype(BF16)
        f = f + _dot(act, wdown_ref[cs, :])
        g_ref[:, cs] = g
    out_ref[...] = x1 + _rms(f, gfpost_ref[...])


def _weight_specs(dff):
    c2 = lambda *_: (0, 0)
    full = lambda r, c: pl.BlockSpec((r, c), c2, pipeline_mode=pl.Buffered(1))
    return [full(SSM_WIDTH, SSM_WIDTH), full(1, SSM_WIDTH), full(ATTN_WIDTH, D_MODEL), full(SSM_WIDTH, D_MODEL),
            full(1, D_MODEL), full(1, D_MODEL), full(D_MODEL, dff), full(D_MODEL, dff), full(CONV_W, dff),
            full(1, dff), full(dff, D_MODEL), full(1, D_MODEL)]


def _ffn_prompt(x, attn2, y4, weights):
    L = x.shape[0]
    tl = ROW_TILE
    dff = weights[6].shape[1]
    rows = lambda w: pl.BlockSpec((tl, w), lambda i: (i, 0))
    slabs = lambda a: pl.BlockSpec((a.shape[0], tl, LANES), lambda i: (0, i, 0))
    return pl.pallas_call(
        functools.partial(_ffn_prompt_kernel, tl=tl, dff=dff),
        grid=(L // tl,),
        in_specs=[rows(D_MODEL), slabs(attn2), slabs(y4)] + _weight_specs(dff),
        out_specs=[rows(D_MODEL), pl.BlockSpec((8, dff), lambda i: (0, 0))],
        out_shape=[jax.ShapeDtypeStruct((L, D_MODEL), F32), jax.ShapeDtypeStruct((8, dff), F32)],
        scratch_shapes=[pltpu.VMEM((dff // FF_CHUNK, 8, FF_CHUNK), F32)],
        compiler_params=_cparams(1, V7X_VMEM_LIMIT),
        name="ffn_prompt",
    )(x, attn2, y4, *weights)


def _ffn_sample(x, attn_bf, y, buf0, buf1, weights):
    nb = x.shape[0]
    dff = weights[6].shape[1]
    rows = lambda w: pl.BlockSpec((nb, w), lambda i: (0, 0))
    return pl.pallas_call(
        functools.partial(_ffn_sample_kernel, dff=dff),
        grid=(1,),
        in_specs=[rows(D_MODEL), rows(ATTN_WIDTH), rows(SSM_WIDTH), rows(dff), rows(dff)] + _weight_specs(dff),
        out_specs=[rows(D_MODEL), rows(dff)],
        out_shape=[jax.ShapeDtypeStruct((nb, D_MODEL), F32), jax.ShapeDtypeStruct((nb, dff), F32)],
        compiler_params=_cparams(1, V7X_VMEM_LIMIT),
        name="ffn_sample",
    )(x, attn_bf, y, buf0, buf1, *weights)


def _inproj_sample_kernel(x_ref, g_ref, w_ref, o_ref):
    o_ref[...] = _dot(_rms(x_ref[...], g_ref[...]).astype(BF16), w_ref[...])


def _inproj_sample(x, g, w_bf):
    nb = x.shape[0]
    return pl.pallas_call(
        _inproj_sample_kernel,
        out_shape=jax.ShapeDtypeStruct((nb, w_bf.shape[1]), F32),
        compiler_params=pltpu.CompilerParams(vmem_limit_bytes=V7X_VMEM_LIMIT),
        name="inproj_sample",
    )(x, g, w_bf)


PAGES_PER_STEP = 16


def _gate_paged_kernel(pt_ref, q_ref, ind_ref, *refs, ppb):
    page_refs, o_ref = refs[:-1], refs[-1]
    qb = _round_bf16(q_ref[0])
    lane = lax.broadcasted_iota(jnp.int32, (ATTN_WIDTH, LANES), 1)
    prods = jnp.zeros((ATTN_WIDTH, LANES), F32)
    for b in range(len(page_refs) // ppb):
        tot = page_refs[b * ppb][0].reshape(ATTN_WIDTH, -1)
        for r in range(1, ppb):
            tot = tot + page_refs[b * ppb + r][0].reshape(ATTN_WIDTH, -1)
        kmean = jnp.sum(tot, axis=1, keepdims=True) * (1.0 / MOBA_BLOCK)
        prods = jnp.where(lane == b, _round_bf16(kmean) * qb, prods)
    p_hi, p_lo = _split_bf16(prods)
    o_ref[0, 0] = _dot(ind_ref[...], p_hi) + _dot(ind_ref[...], p_lo)


def _gate_paged(q, ck, page_table):
    n_pool, H, Dh, page = ck.shape
    S, n_pages = page_table.shape
    ppb = MOBA_BLOCK // page
    pps = PAGES_PER_STEP
    bps = pps // ppb
    nb = n_pages // ppb
    W = H * Dh
    ind = (jnp.arange(H)[:, None] == jnp.arange(W)[None, :] // Dh).astype(BF16)

    def page_spec(r):
        return pl.BlockSpec((1, H, Dh, page), lambda s, c, pt: (pt[s * n_pages + c * pps + r], 0, 0, 0))

    out = pl.pallas_call(
        functools.partial(_gate_paged_kernel, ppb=ppb),
        grid_spec=pltpu.PrefetchScalarGridSpec(
            num_scalar_prefetch=1,
            grid=(S, n_pages // pps),
            in_specs=[pl.BlockSpec((1, W, 1), lambda s, c, pt: (s, 0, 0)),
                      pl.BlockSpec((H, W), lambda s, c, pt: (0, 0))]
            + [page_spec(r) for r in range(pps)],
            out_specs=pl.BlockSpec((1, 1, H, LANES), lambda s, c, pt: (s, c, 0, 0)),
        ),
        out_shape=jax.ShapeDtypeStruct((S, n_pages // pps, H, LANES), F32),
        compiler_params=_cparams(2),
        name="gate_paged",
    )(page_table.reshape(-1), q.reshape(S, W, 1), ind, *([ck] * pps))
    return jnp.transpose(out[..., :bps], (0, 1, 3, 2)).reshape(S, nb, H)


def _top_sample_kernel(g_ref, top_ref):
    gate = g_ref[0]
    nb = gate.shape[0]
    blk = lax.broadcasted_iota(jnp.int32, gate.shape, 0).astype(F32)
    picks = [first for first, _ in _top_blocks(gate, blk, nb)]
    top_ref[0] = jnp.concatenate(picks, axis=0).astype(jnp.int32)


def _top_sample(gates):
    S, nb, H = gates.shape
    return pl.pallas_call(
        _top_sample_kernel,
        grid=(S,),
        in_specs=[pl.BlockSpec((1, nb, H), lambda s: (s, 0, 0))],
        out_specs=pl.BlockSpec((1, MOBA_TOPK, H), lambda s: (s, 0, 0)),
        out_shape=jax.ShapeDtypeStruct((S, MOBA_TOPK, H), jnp.int32),
        compiler_params=_cparams(1),
        name="top_sample",
    )(gates)


def _attn_sample_kernel(pt_ref, top_ref, q_ref, kn_ref, vn_ref, slope_ref, *refs, n_sel, page, past_len):
    k_refs, v_refs, o_ref = refs[:n_sel], refs[n_sel:2 * n_sel], refs[-1]
    s_i = pl.program_id(0)
    h = pl.program_id(1)
    qb = _round_bf16(q_ref[0, 0])
    slope = slope_ref[0]
    scale = HEAD_DIM ** -0.5
    ppb = MOBA_BLOCK // page
    off = lax.broadcasted_iota(jnp.int32, (1, page), 1)
    scores = []
    for r in range(n_sel):
        blk = top_ref[(s_i * MOBA_TOPK + r // ppb) * N_HEADS + h]
        dist = (past_len - (blk * MOBA_BLOCK + (r % ppb) * page + off)).astype(F32)
        qk = jnp.sum(_round_bf16(k_refs[r][0, 0]) * qb, axis=0, keepdims=True)
        scores.append(qk * scale - slope * dist)
    s_own = jnp.sum(qb * _round_bf16(kn_ref[0, 0]), axis=0, keepdims=True) * scale
    m = s_own
    for s in scores:
        m = jnp.maximum(m, jnp.max(s, axis=1, keepdims=True))
    p_own = jnp.exp(s_own - m)
    ps = [jnp.exp(s - m) for s in scores]
    l = p_own
    for p in ps:
        l = l + jnp.sum(p, axis=1, keepdims=True)
    inv = 1.0 / l
    acc = jnp.zeros((HEAD_DIM, page), F32)
    for r in range(n_sel):
        acc = acc + _round_bf16(ps[r] * inv) * _round_bf16(v_refs[r][0, 0])
    o_ref[0, 0] = (jnp.sum(acc, axis=1, keepdims=True)
                   + _round_bf16(p_own * inv) * _round_bf16(vn_ref[0, 0]))


def _attn_sample(q, k_new, v_new, ck, cv, page_table, top, slopes_page):
    n_pool, H, Dh, page = ck.shape
    S, n_pages = page_table.shape
    ppb = MOBA_BLOCK // page
    n_sel = MOBA_TOPK * ppb
    past_len = n_pages * page

    def page_spec(r):
        def imap(s, h, pt, tp):
            blk = tp[(s * MOBA_TOPK + r // ppb) * H + h]
            return (pt[s * n_pages + blk * ppb + r % ppb], h, 0, 0)
        return pl.BlockSpec((1, 1, Dh, page), imap)

    col = pl.BlockSpec((1, 1, Dh, 1), lambda s, h, pt, tp: (s, h, 0, 0))
    cols = lambda a: a.reshape(S, H, Dh, 1)
    out = pl.pallas_call(
        functools.partial(_attn_sample_kernel, n_sel=n_sel, page=page, past_len=past_len),
        grid_spec=pltpu.PrefetchScalarGridSpec(
            num_scalar_prefetch=2,
            grid=(S, H),
            in_specs=[col, col, col, pl.BlockSpec((1, 1, page), lambda s, h, pt, tp: (h, 0, 0))]
            + [page_spec(r) for r in range(n_sel)] * 2,
            out_specs=col,
        ),
        out_shape=jax.ShapeDtypeStruct((S, H, Dh, 1), F32),
        compiler_params=_cparams(2),
        name="attn_sample",
    )(page_table.reshape(-1), top.reshape(-1), cols(q), cols(k_new), cols(v_new), slopes_page,
      *([ck] * n_sel), *([cv] * n_sel))
    return out.reshape(S, H * Dh)


def _ssm_sample_kernel(u_ref, sre_ref, sim_ref, are_ref, aim_ref, bbre_ref, bbim_ref, cre_ref, cim_ref, d_ref,
                       y_ref, nre_ref, nim_ref):
    u = u_ref[...]
    ub = u.astype(BF16)
    ar, ai = are_ref[...], aim_ref[...]
    s0r, s0i = sre_ref[...], sim_ref[...]
    nr = ar * s0r - ai * s0i + _dot(ub, bbre_ref[...])
    ni = ar * s0i + ai * s0r + _dot(ub, bbim_ref[...])
    nre_ref[...] = nr
    nim_ref[...] = ni
    y_ref[...] = (_dot(nr.astype(BF16), cre_ref[...]) - _dot(ni.astype(BF16), cim_ref[...])
                  + d_ref[...] * u)


def _block_diag(x):
    G, r, c = x.shape
    eye = jnp.eye(G, dtype=x.dtype)
    return (x[:, :, None, :] * eye[:, None, :, None]).reshape(G * r, G * c)


def _ssm_sample(u, s_re, s_im, tb, c_re, c_im, d):
    S = u.shape[0]
    GN = N_SSM_GROUPS * SSM_STATE
    bb_re = _block_diag(jnp.transpose(tb["bb_re"], (0, 2, 1))).astype(BF16)
    bb_im = _block_diag(jnp.transpose(tb["bb_im"], (0, 2, 1))).astype(BF16)
    cc_re = _block_diag(jnp.transpose(c_re, (0, 2, 1))).astype(BF16)
    cc_im = _block_diag(jnp.transpose(c_im, (0, 2, 1))).astype(BF16)
    st = jax.ShapeDtypeStruct((S, GN), F32)
    return pl.pallas_call(
        _ssm_sample_kernel,
        out_shape=[jax.ShapeDtypeStruct((S, SSM_WIDTH), F32), st, st],
        compiler_params=pltpu.CompilerParams(vmem_limit_bytes=V7X_VMEM_LIMIT),
        name="ssm_sample",
    )(u, s_re.reshape(S, GN), s_im.reshape(S, GN), tb["abar_re"].reshape(1, GN), tb["abar_im"].reshape(1, GN),
      bb_re, bb_im, cc_re, cc_im, d.reshape(1, SSM_WIDTH))


def _layer_prompt(x, lw):
    kt, vt, u4, qa, ka, va, sel = _inproj_prompt(x, lw["g_mix_pre"], lw["w_in"], lw["k_aug"])
    attn2 = _attn_prompt(qa, ka, va, sel, lw["slopes_q"])
    y4, f_re, f_im = _ssm_prompt(u4, *lw["ssm_chunk"])
    out, conv = _ffn_prompt(x, attn2, y4, lw["ffn"])
    return (out, kt, vt, f_re.reshape(N_SSM_GROUPS, SSM_STATE), f_im.reshape(N_SSM_GROUPS, SSM_STATE),
            conv[8 - (CONV_W - 1):])


def _layer_sample(x, cache_k, cache_v, page_table, s_re, s_im, conv_buf, lw):
    S = x.shape[0]
    proj = _inproj_sample(x, lw["g_mix_pre"], lw["w_in"])
    q = proj[:, :ATTN_WIDTH]
    k = proj[:, ATTN_WIDTH:2 * ATTN_WIDTH]
    v = proj[:, 2 * ATTN_WIDTH:3 * ATTN_WIDTH]
    u = proj[:, 3 * ATTN_WIDTH:]
    ck = jnp.transpose(cache_k, (0, 2, 3, 1))
    cv = jnp.transpose(cache_v, (0, 2, 3, 1))
    top = _top_sample(_gate_paged(q, ck, page_table))
    attn = _attn_sample(q, k, v, ck, cv, page_table, top, lw["slopes_page"])
    y, n_re, n_im = _ssm_sample(u, s_re, s_im, lw["ssm_tb"], lw["c_re"], lw["c_im"], lw["d"])
    out, g = _ffn_sample(x, attn.astype(BF16), y, conv_buf[:, 0], conv_buf[:, 1], lw["ffn"])
    conv_new = jnp.stack([conv_buf[:, 1], g], axis=1)
    return (out, k, v, n_re.reshape(S, N_SSM_GROUPS, SSM_STATE), n_im.reshape(S, N_SSM_GROUPS, SSM_STATE),
            conv_new)


def _alibi_key_table(slopes):
    off = jnp.arange(MOBA_BLOCK, dtype=F32)[None, :] * (slopes * LOG2E)[:, None]
    to_bf16 = lambda a: lax.reduce_precision(a, exponent_bits=8, mantissa_bits=7)
    t0 = to_bf16(off)
    t1 = to_bf16(off - t0)
    t2 = to_bf16(off - t0 - t1)
    terms = jnp.stack([t0, t1, t2], axis=-1).astype(BF16)
    half = jnp.pad(terms, ((0, 0), (0, 0), (0, HEAD_DIM - 3)))
    zero = jnp.zeros_like(half)
    odd = (jnp.arange(N_HEADS) % 2 == 1)[:, None, None]
    return jnp.where(odd, jnp.concatenate([half, zero], axis=-1), jnp.concatenate([zero, half], axis=-1))


def kernel(x_prompt, x_sample, cache_k, cache_v, page_table, state_ssm_re, state_ssm_im, state_conv,
           norm_mix_pre, norm_mix_post, w_in, ssm_a_re, ssm_a_im, ssm_log_step, ssm_b_re, ssm_b_im,
           ssm_c_re, ssm_c_im, ssm_d, w_glu, b_glu, w_out, norm_ffn_pre, norm_ffn_post,
           w_gate, w_up, conv_w, conv_b, w_down):
    depth = w_in.shape[0]
    bp, lp_len = x_prompt.shape[:2]
    bs, ls_len = x_sample.shape[:2]
    page = cache_k.shape[2]
    assert bp == 1 and ls_len == 1 and lp_len % (ROW_TILE * SSM_CHUNK // 8) == 0 and page == LANES
    assert page_table.shape[1] % PAGES_PER_STEP == 0
    slopes = jnp.exp2(-8.0 * jnp.arange(1, N_HEADS + 1, dtype=F32) / N_HEADS)
    hp = x_prompt[0]
    hs = x_sample[:, 0]
    outs = [[] for _ in range(10)]
    for l in range(depth):
        tb = _ssm_tables(ssm_a_re[l], ssm_a_im[l], ssm_log_step[l], ssm_b_re[l], ssm_b_im[l])
        row = lambda a: a[l].reshape(1, -1)
        lw = dict(
            g_mix_pre=row(norm_mix_pre), w_in=w_in[l].astype(BF16),
            k_aug=_alibi_key_table(slopes),
            slopes_q=jnp.broadcast_to((slopes * LOG2E)[:, None, None], (N_HEADS, 1, ROW_TILE)),
            slopes_page=jnp.broadcast_to(slopes[:, None, None], (N_HEADS, 1, page)),
            ssm_tb=tb, c_re=ssm_c_re[l], c_im=ssm_c_im[l], d=ssm_d[l],
            ssm_chunk=_ssm_chunk_tables(tb, ssm_c_re[l], ssm_c_im[l], ssm_d[l]),
            ffn=[w_glu[l].astype(BF16), row(b_glu), w_out[l, :ATTN_WIDTH].astype(BF16),
                 w_out[l, ATTN_WIDTH:].astype(BF16), row(norm_mix_post), row(norm_ffn_pre),
                 w_gate[l].astype(BF16), w_up[l].astype(BF16), conv_w[l], row(conv_b),
                 w_down[l].astype(BF16), row(norm_ffn_post)],
        )
        hp, ktp, vtp, sr, si, cp = _layer_prompt(hp, lw)
        outs[0].append(jnp.transpose(ktp, (2, 0, 1))[None])
        outs[1].append(jnp.transpose(vtp, (2, 0, 1))[None])
        outs[4].append(sr[None])
        outs[5].append(si[None])
        outs[8].append(cp[None])
        hs, ks, vs, sr, si, cs = _layer_sample(hs, cache_k[l], cache_v[l], page_table, state_ssm_re[l],
                                               state_ssm_im[l], state_conv[l], lw)
        outs[2].append(ks.reshape(bs, ls_len, N_HEADS, HEAD_DIM))
        outs[3].append(vs.reshape(bs, ls_len, N_HEADS, HEAD_DIM))
        outs[6].append(sr)
        outs[7].append(si)
        outs[9].append(cs)
    return (hp[None], hs[:, None], *[jnp.stack(o) for o in outs])
```

```python
import functools
import math

import jax
import jax.numpy as jnp
from jax import lax
from jax.experimental import pallas as pl
from jax.experimental.pallas import tpu as pltpu

F32 = jnp.float32
BF16 = jnp.bfloat16

D_MODEL = 1024
N_HEADS = 8
HEAD_DIM = 64
ATTN_WIDTH = N_HEADS * HEAD_DIM
SSM_WIDTH = D_MODEL - ATTN_WIDTH
MOBA_BLOCK = 256
MOBA_TOPK = 3
SSM_GROUP = 16
N_SSM_GROUPS = SSM_WIDTH // SSM_GROUP
SSM_STATE = 64
SSM_CHUNK = 16
CONV_W = 3
RMS_EPS = 1e-6
NEG = -1e30
LOG2E = 1.4426950408889634
LANES = 128
V7X_VMEM_LIMIT = 56 * 1024 * 1024
HI = lax.Precision.HIGHEST

ROW_TILE = 512
V_ROWS = 80
SLABS = SSM_WIDTH // LANES
GROUPS_PER_SLAB = LANES // SSM_GROUP


def _cparams(n_axes, vmem=None):
    return pltpu.CompilerParams(dimension_semantics=("arbitrary",) * n_axes, vmem_limit_bytes=vmem)


def _rms(x, g):
    return x * lax.rsqrt(jnp.mean(x * x, axis=-1, keepdims=True) + RMS_EPS) * g


def _gelu_tanh(x):
    return 0.5 * x * (1.0 + jnp.tanh(math.sqrt(2.0 / math.pi) * (x + 0.044715 * (x * x * x))))


def _sigmoid(x):
    return 1.0 / (1.0 + jnp.exp(-x))


def _split_bf16(a):
    hi = a.astype(BF16)
    lo = (a - hi.astype(F32)).astype(BF16)
    return hi, lo


def _round_bf16(a):
    return a.astype(BF16).astype(F32)


def _dot(a, b):
    return jnp.dot(a, b, preferred_element_type=F32)


def _dot_nt(a, b):
    return lax.dot_general(a, b, (((1,), (1,)), ((), ())), preferred_element_type=F32)


def _top_blocks(cur, blk, n):
    picks = []
    for _ in range(MOBA_TOPK):
        mx = jnp.max(cur, axis=0, keepdims=True)
        first = jnp.min(jnp.where(cur == mx, blk, float(n)), axis=0, keepdims=True)
        picks.append((first, mx))
        cur = jnp.where(blk == first, NEG, cur)
    return picks


def _inproj_prompt_kernel(x_ref, g_ref, w_ref, aug_ref, kt_ref, vt_ref, u_ref, qa_ref, ka_ref, va_ref, sel_ref,
                          qn_ref, kn_ref, kmean_s, *, tl, nb):
    i = pl.program_id(0)
    bpt = tl // MOBA_BLOCK
    H, Dh, B = N_HEADS, HEAD_DIM, MOBA_BLOCK

    @pl.when(i == 0)
    def _():
        kmean_s[...] = jnp.zeros_like(kmean_s)
        qn_ref[...] = jnp.zeros_like(qn_ref)
        kn_ref[...] = jnp.zeros_like(kn_ref)

    h = _rms(x_ref[...], g_ref[...]).astype(BF16)
    proj = _dot(h, w_ref[...])
    q = proj[:, :ATTN_WIDTH]
    k = proj[:, ATTN_WIDTH:2 * ATTN_WIDTH]
    v = proj[:, 2 * ATTN_WIDTH:3 * ATTN_WIDTH]
    for s in range(SLABS):
        u_ref[s] = proj[:, 3 * ATTN_WIDTH + s * LANES:3 * ATTN_WIDTH + (s + 1) * LANES]

    kt = k.T
    vt = v.T
    qt = (q * (Dh ** -0.5 * LOG2E)).T.astype(BF16)
    kt_ref[...] = kt.reshape(H, Dh, tl)
    vt_ref[...] = vt.reshape(H, Dh, tl)

    def head_norm2(t):
        t = t.astype(F32)
        n2 = jnp.sum((t * t).reshape(H, Dh, tl), axis=1)
        return jnp.broadcast_to(jnp.max(n2, axis=1, keepdims=True), (H, LANES))

    qn_ref[...] = jnp.maximum(qn_ref[...], head_norm2(qt))
    kn_ref[...] = jnp.maximum(kn_ref[...], head_norm2(kt))

    sub = lax.broadcasted_iota(jnp.int32, (Dh, tl), 0)
    ones3 = jnp.where(sub < 3, 1.0, 0.0).astype(BF16)
    kb = k.astype(BF16)
    lane_hi = lax.broadcasted_iota(jnp.int32, (B, LANES), 1) >= Dh
    vtb = vt.astype(BF16)
    ones_rows = jnp.ones((V_ROWS - Dh, B), BF16)
    for hd in range(H):
        qh = qt[hd * Dh:(hd + 1) * Dh, :]
        odd = hd % 2 == 1
        qa_ref[hd] = jnp.concatenate([ones3, qh] if odd else [qh, ones3], axis=0)
        for b in range(bpt):
            slab = kb[b * B:(b + 1) * B, (hd // 2) * LANES:(hd // 2 + 1) * LANES]
            ka_ref[hd, b] = jnp.where(lane_hi == odd, slab, aug_ref[hd])
            va_ref[hd, b] = jnp.concatenate([vtb[hd * Dh:(hd + 1) * Dh, b * B:(b + 1) * B], ones_rows], axis=0)

    row = lax.broadcasted_iota(jnp.int32, kmean_s.shape, 0)
    km = kmean_s[...]
    for b in range(bpt):
        kmb = jnp.mean(k[b * B:(b + 1) * B, :], axis=0, keepdims=True)
        km = jnp.where(row == i * bpt + b, kmb, km)
    kmean_s[...] = km

    blk = lax.broadcasted_iota(jnp.int32, (nb, tl), 0).astype(F32)
    own = ((i * tl + lax.broadcasted_iota(jnp.int32, (1, tl), 1)) // B).astype(F32)
    kmb16 = km.astype(BF16)
    qb16 = q.astype(BF16)
    for hd in range(H):
        sl = slice(hd * Dh, (hd + 1) * Dh)
        gate = _dot_nt(kmb16[:, sl], qb16[:, sl])
        chosen = jnp.zeros((nb, tl), F32)
        for first, mx in _top_blocks(jnp.where(blk < own, gate, NEG), blk, nb):
            chosen = jnp.where((blk == first) & (mx > 0.5 * NEG), 1.0, chosen)
        sel_ref[hd] = jnp.where(chosen > 0.5, 0.0, NEG)


def _inproj_prompt(x, g, w_bf, aug):
    L = x.shape[0]
    tl = ROW_TILE
    nb = L // MOBA_BLOCK
    bpt = tl // MOBA_BLOCK
    H, Dh, B = N_HEADS, HEAD_DIM, MOBA_BLOCK
    tcol = pl.BlockSpec((H, Dh, tl), lambda i: (0, 0, i))
    return pl.pallas_call(
        functools.partial(_inproj_prompt_kernel, tl=tl, nb=nb),
        grid=(L // tl,),
        in_specs=[pl.BlockSpec((tl, D_MODEL), lambda i: (i, 0)),
                  pl.BlockSpec((1, D_MODEL), lambda i: (0, 0)),
                  pl.BlockSpec((D_MODEL, 4 * ATTN_WIDTH), lambda i: (0, 0)),
                  pl.BlockSpec((H, B, LANES), lambda i: (0, 0, 0))],
        out_specs=[tcol, tcol,
                   pl.BlockSpec((SLABS, tl, LANES), lambda i: (0, i, 0)),
                   pl.BlockSpec((H, 2 * Dh, tl), lambda i: (0, 0, i)),
                   pl.BlockSpec((H, bpt, B, LANES), lambda i: (0, i, 0, 0)),
                   pl.BlockSpec((H, bpt, V_ROWS, B), lambda i: (0, i, 0, 0)),
                   pl.BlockSpec((H, nb, tl), lambda i: (0, 0, i)),
                   pl.BlockSpec((H, LANES), lambda i: (0, 0)),
                   pl.BlockSpec((H, LANES), lambda i: (0, 0))],
        out_shape=[jax.ShapeDtypeStruct((H, Dh, L), F32), jax.ShapeDtypeStruct((H, Dh, L), F32),
                   jax.ShapeDtypeStruct((SLABS, L, LANES), F32),
                   jax.ShapeDtypeStruct((H, 2 * Dh, L), BF16),
                   jax.ShapeDtypeStruct((H, nb, B, LANES), BF16),
                   jax.ShapeDtypeStruct((H, nb, V_ROWS, B), BF16),
                   jax.ShapeDtypeStruct((H, nb, L), F32),
                   jax.ShapeDtypeStruct((H, LANES), F32), jax.ShapeDtypeStruct((H, LANES), F32)],
        scratch_shapes=[pltpu.VMEM((nb, ATTN_WIDTH), F32)],
        compiler_params=_cparams(1, V7X_VMEM_LIMIT),
        name="inproj_prompt",
    )(x, g, w_bf, aug)


HEADS_PER_STEP = 2
ATTN_TQ = 1024
UNDERFLOW_LOG2 = 160.0


def _attn_prompt_kernel(w_ref, qa_ref, ka_ref, va_ref, sel_ref, slope_ref, o_ref, s_scr, acc_scr, *, tq):
    hp = pl.program_id(0)
    qi = pl.program_id(1)
    B, Dh = MOBA_BLOCK, HEAD_DIM
    bpq = tq // B
    units = [(e, cb) for e in range(HEADS_PER_STEP) for cb in range(bpq)]
    n_off = qi * bpq + (bpq - 1)
    keep = w_ref[hp * HEADS_PER_STEP]
    for e in range(1, HEADS_PER_STEP):
        keep = jnp.maximum(keep, w_ref[hp * HEADS_PER_STEP + e])
    j_start = jnp.maximum(qi * bpq - keep, 0)
    lane = lax.broadcasted_iota(jnp.int32, (1, B), 1).astype(F32)
    causal = lax.broadcasted_iota(jnp.int32, (B, B), 0) <= lax.broadcasted_iota(jnp.int32, (B, B), 1)

    def q_of(e, cb):
        return qa_ref[e, :, cb * B:(cb + 1) * B]

    def slope_of(e):
        return slope_ref[e]

    ms = []
    for u, (e, cb) in enumerate(units):
        own = qi * bpq + cb
        s = _dot(ka_ref[e, own], q_of(e, cb))
        s = jnp.where(causal, s - slope_of(e) * lane, NEG)
        m = jnp.max(s, axis=0, keepdims=True)
        acc_scr[u] = _dot(va_ref[e, own], jnp.exp2(s - m).astype(BF16))
        s_scr[u] = _dot(ka_ref[e, j_start], q_of(e, cb))
        ms.append(m)

    def body(j, ms):
        nxt = jnp.minimum(j + 1, n_off - 1)
        out = []
        for u, (e, cb) in enumerate(units):
            s = s_scr[u]
            dist = lane + ((qi * bpq + cb - j) * B).astype(F32)
            col = sel_ref[e, pl.ds(j, 1), cb * B:(cb + 1) * B] - slope_of(e) * dist
            m_new = jnp.maximum(ms[u], jnp.max(s, axis=0, keepdims=True) + col)
            p = jnp.exp2(s - (m_new - col)).astype(BF16)
            acc_scr[u] = jnp.exp2(ms[u] - m_new) * acc_scr[u] + _dot(va_ref[e, j], p)
            s_scr[u] = _dot(ka_ref[e, nxt], q_of(e, cb))
            out.append(m_new)
        return tuple(out)

    lax.fori_loop(j_start, n_off, body, tuple(ms))
    for cb in range(bpq):
        outs = []
        for e in range(HEADS_PER_STEP):
            acc = acc_scr[e * bpq + cb]
            outs.append((acc[:Dh, :] / acc[Dh:Dh + 1, :]).T)
        o_ref[0, cb * B:(cb + 1) * B, :] = jnp.concatenate(outs, axis=1).astype(o_ref.dtype)


def _alibi_keep_blocks(slopes, qn2, kn2, nb):
    qk = jnp.sqrt(qn2 * kn2) * 1.05
    need = (2.0 * qk + UNDERFLOW_LOG2) / (slopes * LOG2E)
    w = jnp.ceil((need - 1.0) / MOBA_BLOCK)
    return jnp.clip(w, 1.0, float(nb)).astype(jnp.int32)


def _attn_prompt(keep, qa, ka, va, sel, slopes):
    H, nb, B, _ = ka.shape
    L = qa.shape[2]
    tq = ATTN_TQ
    hp = HEADS_PER_STEP
    return pl.pallas_call(
        functools.partial(_attn_prompt_kernel, tq=tq),
        grid_spec=pltpu.PrefetchScalarGridSpec(
            num_scalar_prefetch=1,
            grid=(H // hp, L // tq),
            in_specs=[pl.BlockSpec((hp, 2 * HEAD_DIM, tq), lambda h, i, w: (h, 0, i)),
                      pl.BlockSpec((hp, nb, B, LANES), lambda h, i, w: (h, 0, 0, 0)),
                      pl.BlockSpec((hp, nb, V_ROWS, B), lambda h, i, w: (h, 0, 0, 0)),
                      pl.BlockSpec((hp, nb, tq), lambda h, i, w: (h, 0, i)),
                      pl.BlockSpec((hp, 1, B), lambda h, i, w: (h, 0, 0))],
            out_specs=pl.BlockSpec((1, tq, hp * HEAD_DIM), lambda h, i, w: (h, i, 0)),
            scratch_shapes=[pltpu.VMEM((hp * tq // B, B, B), F32), pltpu.VMEM((hp * tq // B, V_ROWS, B), F32)],
        ),
        out_shape=jax.ShapeDtypeStruct((H // hp, L, hp * HEAD_DIM), BF16),
        compiler_params=_cparams(2, V7X_VMEM_LIMIT),
        name="attn_prompt",
    )(keep, qa, ka, va, sel, slopes)


def _ssm_tables(a_re, a_im, log_step, b_re, b_im):
    T = SSM_CHUNK
    dt = jnp.exp(log_step)[:, None]
    j = jnp.arange(T + 1, dtype=F32)[:, None, None]
    mag = jnp.exp(a_re * dt * j)
    pw_re = mag * jnp.cos(a_im * dt * j)
    pw_im = mag * jnp.sin(a_im * dt * j)
    abar_re, abar_im = pw_re[1], pw_im[1]
    den = a_re * a_re + a_im * a_im
    nr = abar_re - 1.0
    ni = abar_im
    coef_re = (nr * a_re + ni * a_im) / den
    coef_im = (ni * a_re - nr * a_im) / den
    bb_re = coef_re[..., None] * b_re - coef_im[..., None] * b_im
    bb_im = coef_re[..., None] * b_im + coef_im[..., None] * b_re
    return dict(pw_re=pw_re, pw_im=pw_im, bb_re=bb_re, bb_im=bb_im, abar_re=abar_re, abar_im=abar_im)


def _slab_diag(x):
    lead = x.shape[:-3]
    r, c = x.shape[-2:]
    gs = GROUPS_PER_SLAB
    x = x.reshape(lead + (SLABS, gs, r, c))
    eye = jnp.eye(gs, dtype=x.dtype)
    y = x[..., :, :, None, :] * eye[:, None, :, None]
    return y.reshape(lead + (SLABS, gs * r, gs * c))


def _ssm_chunk_tables(tb, c_re, c_im, d):
    pw_re, pw_im, bb_re, bb_im = tb["pw_re"], tb["pw_im"], tb["bb_re"], tb["bb_im"]
    T = SSM_CHUNK
    G, N, P = bb_re.shape
    x_re = pw_re[:T, :, :, None] * bb_re[None] - pw_im[:T, :, :, None] * bb_im[None]
    x_im = pw_re[:T, :, :, None] * bb_im[None] + pw_im[:T, :, :, None] * bb_re[None]
    kj = (jnp.einsum("gpn,jgnq->jgqp", c_re, x_re, precision=HI)
          - jnp.einsum("gpn,jgnq->jgqp", c_im, x_im, precision=HI))
    kj = kj.at[0].add(jnp.eye(P, dtype=F32)[None] * d[:, :, None])
    kbd = _slab_diag(kj)
    kpad = jnp.concatenate([jnp.zeros_like(kbd[:1]), kbd], axis=0)
    lag = jnp.concatenate([kpad[:T], kpad[1:]], axis=-1)
    lag = jnp.transpose(lag, (1, 0, 2, 3)).reshape(SLABS, T * LANES, 2 * LANES).astype(BF16)
    f_re = _slab_diag(jnp.transpose(x_re[::-1], (0, 1, 3, 2)))
    f_im = _slab_diag(jnp.transpose(x_im[::-1], (0, 1, 3, 2)))
    f = jnp.concatenate([f_re, f_im], axis=-1)
    f = jnp.transpose(f, (1, 0, 2, 3)).reshape(SLABS, T * LANES, 2 * GROUPS_PER_SLAB * N).astype(BF16)
    cr = jnp.transpose(c_re, (0, 2, 1))[None]
    ci = jnp.transpose(c_im, (0, 2, 1))[None]
    ar = pw_re[1:T + 1, :, :, None]
    ai = pw_im[1:T + 1, :, :, None]
    e = jnp.concatenate([_slab_diag(cr * ar - ci * ai), _slab_diag(-(cr * ai + ci * ar))], axis=2)
    e = e.reshape(T // 2, 2, SLABS, e.shape[2], LANES)
    e = jnp.transpose(e, (2, 0, 3, 1, 4)).reshape(SLABS, T // 2, e.shape[3], 2 * LANES).astype(BF16)
    a16_re = pw_re[T].reshape(1, G * N)
    a16_im = pw_im[T].reshape(1, G * N)
    return lag, f, e, a16_re, a16_im


def _chunk_steps(u_ref, rows):
    return [u_ref[0, pl.ds(s, rows, stride=SSM_CHUNK), :].astype(BF16) for s in range(SSM_CHUNK)]


def _ssm_chunk_in_kernel(u_ref, f_ref, bre_ref, bim_ref, *, rows):
    b = _dot(jnp.concatenate(_chunk_steps(u_ref, rows), axis=1), f_ref[0])
    half = b.shape[1] // 2
    bre_ref[...] = b[:, :half]
    bim_ref[...] = b[:, half:]


def _ssm_scan_kernel(bre_ref, bim_ref, are_ref, aim_ref, sre_ref, sim_ref, fre_ref, fim_ref):
    nc = bre_ref.shape[0]
    ar = are_ref[...]
    ai = aim_ref[...]

    def body(c8, carry):
        sr, si = carry
        r0 = pl.multiple_of(c8 * 8, 8)
        br = bre_ref[pl.ds(r0, 8), :]
        bi = bim_ref[pl.ds(r0, 8), :]
        rows_r, rows_i = [], []
        for r in range(8):
            rows_r.append(sr)
            rows_i.append(si)
            sr, si = (ar * sr - ai * si + br[r:r + 1, :], ar * si + ai * sr + bi[r:r + 1, :])
        sre_ref[pl.ds(r0, 8), :] = jnp.concatenate(rows_r, axis=0)
        sim_ref[pl.ds(r0, 8), :] = jnp.concatenate(rows_i, axis=0)
        return sr, si

    z = jnp.zeros(are_ref.shape, F32)
    sr, si = lax.fori_loop(0, nc // 8, body, (z, z))
    fre_ref[...] = sr
    fim_ref[...] = si


def _ssm_chunk_out_kernel(u_ref, lag_ref, e_ref, sre_ref, sim_ref, y_ref, *, rows):
    us = _chunk_steps(u_ref, rows)
    s = jnp.concatenate([sre_ref[...], sim_ref[...]], axis=1).astype(BF16)
    for pair in range(SSM_CHUNK // 2):
        tau = 2 * pair
        lhs = jnp.concatenate(us[tau + 1::-1], axis=1)
        y2 = _dot(lhs, lag_ref[0, :LANES * (tau + 2), :]) + _dot(s, e_ref[0, pair])
        y_ref[0, pl.ds(tau, rows, stride=SSM_CHUNK), :] = y2[:, :LANES]
        y_ref[0, pl.ds(tau + 1, rows, stride=SSM_CHUNK), :] = y2[:, LANES:]


SSM_ROWS = 512


def _ssm_prompt(u4, lag, f, e, a16_re, a16_im):
    L = u4.shape[1]
    nc = L // SSM_CHUNK
    rows = min(SSM_ROWS, nc)
    GN = N_SSM_GROUPS * SSM_STATE
    SW = GROUPS_PER_SLAB * SSM_STATE
    st = jax.ShapeDtypeStruct((nc, GN), F32)
    slab_rows = pl.BlockSpec((1, rows * SSM_CHUNK, LANES), lambda s, r: (s, r, 0))
    state_cols = pl.BlockSpec((rows, SW), lambda s, r: (r, s))
    b_re, b_im = pl.pallas_call(
        functools.partial(_ssm_chunk_in_kernel, rows=rows),
        grid=(SLABS, nc // rows),
        in_specs=[slab_rows, pl.BlockSpec((1,) + f.shape[1:], lambda s, r: (s, 0, 0))],
        out_specs=[state_cols, state_cols],
        out_shape=[st, st],
        compiler_params=_cparams(2, V7X_VMEM_LIMIT),
        name="ssm_chunk_in",
    )(u4, f)
    fin = jax.ShapeDtypeStruct((1, GN), F32)
    s_re, s_im, f_re, f_im = pl.pallas_call(
        _ssm_scan_kernel,
        out_shape=[st, st, fin, fin],
        compiler_params=pltpu.CompilerParams(vmem_limit_bytes=V7X_VMEM_LIMIT),
        name="ssm_scan",
    )(b_re, b_im, a16_re, a16_im)
    y4 = pl.pallas_call(
        functools.partial(_ssm_chunk_out_kernel, rows=rows),
        grid=(SLABS, nc // rows),
        in_specs=[slab_rows,
                  pl.BlockSpec((1,) + lag.shape[1:], lambda s, r: (s, 0, 0)),
                  pl.BlockSpec((1,) + e.shape[1:], lambda s, r: (s, 0, 0, 0)),
                  state_cols, state_cols],
        out_specs=slab_rows,
        out_shape=jax.ShapeDtypeStruct(u4.shape, F32),
        compiler_params=_cparams(2, V7X_VMEM_LIMIT),
        name="ssm_chunk_out",
    )(u4, lag, e, s_re, s_im)
    return y4, f_re, f_im


FF_CHUNK = 256


def _mix_and_prenorm(x, attn_bf, y, wglu_ref, bglu_ref, wouta_ref, wouts_ref, gpost_ref, gpre_ref):
    z = _gelu_tanh(y)
    ssm = z * _sigmoid(_dot(z.astype(BF16), wglu_ref[...]) + bglu_ref[...])
    mix = _dot(attn_bf, wouta_ref[...]) + _dot(ssm.astype(BF16), wouts_ref[...])
    x1 = x + _rms(mix, gpost_ref[...])
    h2 = _rms(x1, gpre_ref[...]).astype(BF16)
    return x1, h2


def _ffn_prompt_kernel(x_ref, attn_ref, y_ref, wglu_ref, bglu_ref, wouta_ref, wouts_ref, gpost_ref, gpre_ref,
                       wgate_ref, wup_ref, cw_ref, cb_ref, wdown_ref, gfpost_ref, out_ref, conv_ref, tail_s,
                       *, tl, dff):
    i = pl.program_id(0)

    @pl.when(i == 0)
    def _():
        tail_s[...] = jnp.zeros_like(tail_s)

    attn = jnp.concatenate([attn_ref[s] for s in range(attn_ref.shape[0])], axis=1)
    y = jnp.concatenate([y_ref[s] for s in range(SLABS)], axis=1)
    x1, h2 = _mix_and_prenorm(x_ref[...], attn, y, wglu_ref, bglu_ref, wouta_ref, wouts_ref, gpost_ref, gpre_ref)
    row = lax.broadcasted_iota(jnp.int32, (tl, FF_CHUNK), 0)
    f = jnp.zeros((tl, D_MODEL), F32)
    for c in range(dff // FF_CHUNK):
        cs = slice(c * FF_CHUNK, (c + 1) * FF_CHUNK)
        g = _dot(h2, wgate_ref[:, cs])
        up = _dot(h2, wup_ref[:, cs])
        tail = tail_s[c]
        p1 = tail[7:8, :]
        p2 = tail[6:7, :]
        g1 = jnp.where(row == 0, p1, pltpu.roll(g, 1, 0))
        g2 = jnp.where(row == 0, p2, jnp.where(row == 1, p1, pltpu.roll(g, 2, 0)))
        gc = cw_ref[0:1, cs] * g2 + cw_ref[1:2, cs] * g1 + cw_ref[2:3, cs] * g + cb_ref[:, cs]
        act = (_gelu_tanh(gc) * up).astype(BF16)
        f = f + _dot(act, wdown_ref[cs, :])
        tail_s[c] = g[tl - 8:, :]
        conv_ref[:, cs] = g[tl - 8:, :]
    out_ref[...] = x1 + _rms(f, gfpost_ref[...])


def _ffn_sample_kernel(x_ref, attn_ref, y_ref, b0_ref, b1_ref, wglu_ref, bglu_ref, wouta_ref, wouts_ref,
                       gpost_ref, gpre_ref, wgate_ref, wup_ref, cw_ref, cb_ref, wdown_ref, gfpost_ref,
                       out_ref, g_ref, *, dff):
    x1, h2 = _mix_and_prenorm(x_ref[...], attn_ref[...], y_ref[...], wglu_ref, bglu_ref, wouta_ref,
                              wouts_ref, gpost_ref, gpre_ref)
    f = jnp.zeros(x1.shape, F32)
    for c in range(dff // FF_CHUNK):
        cs = slice(c * FF_CHUNK, (c + 1) * FF_CHUNK)
        g = _dot(h2, wgate_ref[:, cs])
        up = _dot(h2, wup_ref[:, cs])
        gc = (cw_ref[0:1, cs] * b0_ref[:, cs] + cw_ref[1:2, cs] * b1_ref[:, cs] + cw_ref[2:3, cs] * g
              + cb_ref[:, cs])
        act = (_gelu_tanh(gc) * up).astype(BF16)
        f = f + _dot(act, wdown_ref[cs, :])
        g_ref[:, cs] = g
    out_ref[...] = x1 + _rms(f, gfpost_ref[...])


def _weight_specs(dff):
    c2 = lambda *_: (0, 0)
    full = lambda r, c: pl.BlockSpec((r, c), c2, pipeline_mode=pl.Buffered(1))
    return [full(SSM_WIDTH, SSM_WIDTH), full(1, SSM_WIDTH), full(ATTN_WIDTH, D_MODEL), full(SSM_WIDTH, D_MODEL),
            full(1, D_MODEL), full(1, D_MODEL), full(D_MODEL, dff), full(D_MODEL, dff), full(CONV_W, dff),
            full(1, dff), full(dff, D_MODEL), full(1, D_MODEL)]


def _ffn_prompt(x, attn2, y4, weights):
    L = x.shape[0]
    tl = ROW_TILE
    dff = weights[6].shape[1]
    rows = lambda w: pl.BlockSpec((tl, w), lambda i: (i, 0))
    slabs = lambda a: pl.BlockSpec((a.shape[0], tl, LANES), lambda i: (0, i, 0))
    return pl.pallas_call(
        functools.partial(_ffn_prompt_kernel, tl=tl, dff=dff),
        grid=(L // tl,),
        in_specs=[rows(D_MODEL), slabs(attn2), slabs(y4)] + _weight_specs(dff),
        out_specs=[rows(D_MODEL), pl.BlockSpec((8, dff), lambda i: (0, 0))],
        out_shape=[jax.ShapeDtypeStruct((L, D_MODEL), F32), jax.ShapeDtypeStruct((8, dff), F32)],
        scratch_shapes=[pltpu.VMEM((dff // FF_CHUNK, 8, FF_CHUNK), F32)],
        compiler_params=_cparams(1, V7X_VMEM_LIMIT),
        name="ffn_prompt",
    )(x, attn2, y4, *weights)


def _ffn_sample(x, attn_bf, y, buf0, buf1, weights):
    nb = x.shape[0]
    dff = weights[6].shape[1]
    rows = lambda w: pl.BlockSpec((nb, w), lambda i: (0, 0))
    return pl.pallas_call(
        functools.partial(_ffn_sample_kernel, dff=dff),
        grid=(1,),
        in_specs=[rows(D_MODEL), rows(ATTN_WIDTH), rows(SSM_WIDTH), rows(dff), rows(dff)] + _weight_specs(dff),
        out_specs=[rows(D_MODEL), rows(dff)],
        out_shape=[jax.ShapeDtypeStruct((nb, D_MODEL), F32), jax.ShapeDtypeStruct((nb, dff), F32)],
        compiler_params=_cparams(1, V7X_VMEM_LIMIT),
        name="ffn_sample",
    )(x, attn_bf, y, buf0, buf1, *weights)


def _inproj_sample_kernel(x_ref, g_ref, w_ref, o_ref):
    o_ref[...] = _dot(_rms(x_ref[...], g_ref[...]).astype(BF16), w_ref[...])


def _inproj_sample(x, g, w_bf):
    nb = x.shape[0]
    return pl.pallas_call(
        _inproj_sample_kernel,
        out_shape=jax.ShapeDtypeStruct((nb, w_bf.shape[1]), F32),
        compiler_params=pltpu.CompilerParams(vmem_limit_bytes=V7X_VMEM_LIMIT),
        name="inproj_sample",
    )(x, g, w_bf)


PAGES_PER_STEP = 16


def _gate_paged_kernel(pt_ref, q_ref, ind_ref, *refs, ppb):
    page_refs, o_ref = refs[:-1], refs[-1]
    qb = _round_bf16(q_ref[0])
    lane = lax.broadcasted_iota(jnp.int32, (ATTN_WIDTH, LANES), 1)
    prods = jnp.zeros((ATTN_WIDTH, LANES), F32)
    for b in range(len(page_refs) // ppb):
        tot = page_refs[b * ppb][0].reshape(ATTN_WIDTH, -1)
        for r in range(1, ppb):
            tot = tot + page_refs[b * ppb + r][0].reshape(ATTN_WIDTH, -1)
        kmean = jnp.sum(tot, axis=1, keepdims=True) * (1.0 / MOBA_BLOCK)
        prods = jnp.where(lane == b, _round_bf16(kmean) * qb, prods)
    p_hi, p_lo = _split_bf16(prods)
    o_ref[0, 0] = _dot(ind_ref[...], p_hi) + _dot(ind_ref[...], p_lo)


def _gate_paged(q, ck, page_table):
    n_pool, H, Dh, page = ck.shape
    S, n_pages = page_table.shape
    ppb = MOBA_BLOCK // page
    pps = PAGES_PER_STEP
    bps = pps // ppb
    nb = n_pages // ppb
    W = H * Dh
    ind = (jnp.arange(H)[:, None] == jnp.arange(W)[None, :] // Dh).astype(BF16)

    def page_spec(r):
        return pl.BlockSpec((1, H, Dh, page), lambda s, c, pt: (pt[s * n_pages + c * pps + r], 0, 0, 0))

    out = pl.pallas_call(
        functools.partial(_gate_paged_kernel, ppb=ppb),
        grid_spec=pltpu.PrefetchScalarGridSpec(
            num_scalar_prefetch=1,
            grid=(S, n_pages // pps),
            in_specs=[pl.BlockSpec((1, W, LANES), lambda s, c, pt: (s, 0, 0)),
                      pl.BlockSpec((H, W), lambda s, c, pt: (0, 0))]
            + [page_spec(r) for r in range(pps)],
            out_specs=pl.BlockSpec((1, 1, H, LANES), lambda s, c, pt: (s, c, 0, 0)),
        ),
        out_shape=jax.ShapeDtypeStruct((S, n_pages // pps, H, LANES), F32),
        compiler_params=_cparams(2),
        name="gate_paged",
    )(page_table.reshape(-1), jnp.broadcast_to(q.reshape(S, W, 1), (S, W, LANES)), ind, *([ck] * pps))
    return jnp.transpose(out[..., :bps], (0, 1, 3, 2)).reshape(S, nb, H)


def _top_sample_kernel(g_ref, top_ref):
    gate = g_ref[0]
    nb = gate.shape[0]
    blk = lax.broadcasted_iota(jnp.int32, gate.shape, 0).astype(F32)
    picks = [first for first, _ in _top_blocks(gate, blk, nb)]
    top_ref[0] = jnp.concatenate(picks, axis=0).astype(jnp.int32)


def _top_sample(gates):
    S, nb, H = gates.shape
    return pl.pallas_call(
        _top_sample_kernel,
        grid=(S,),
        in_specs=[pl.BlockSpec((1, nb, H), lambda s: (s, 0, 0))],
        out_specs=pl.BlockSpec((1, MOBA_TOPK, H), lambda s: (s, 0, 0)),
        out_shape=jax.ShapeDtypeStruct((S, MOBA_TOPK, H), jnp.int32),
        compiler_params=_cparams(1),
        name="top_sample",
    )(gates)


def _attn_sample_kernel(pt_ref, top_ref, q_ref, kn_ref, vn_ref, slope_ref, ck_ref, cv_ref, o_ref,
                        kbuf, vbuf, sems, *, n_sel, n_pages, page, past_len):
    H = N_HEADS
    s_i = pl.program_id(0)
    n_seq = pl.num_programs(0)
    scale = HEAD_DIM ** -0.5
    ppb = MOBA_BLOCK // page
    off = lax.broadcasted_iota(jnp.int32, (1, page), 1)

    def block_of(seq, h, r):
        return top_ref[(seq * MOBA_TOPK + r // ppb) * H + h]

    def copies(seq, slot):
        out = []
        for h in range(H):
            for r in range(n_sel):
                pg = pt_ref[seq * n_pages + block_of(seq, h, r) * ppb + r % ppb]
                out.append(pltpu.make_async_copy(ck_ref.at[pg, h], kbuf.at[slot, h * n_sel + r], sems.at[0, slot]))
                out.append(pltpu.make_async_copy(cv_ref.at[pg, h], vbuf.at[slot, h * n_sel + r], sems.at[1, slot]))
        return out

    slot = s_i % 2

    @pl.when(s_i == 0)
    def _():
        for c in copies(s_i, slot):
            c.start()

    @pl.when(s_i + 1 < n_seq)
    def _():
        for c in copies(s_i + 1, 1 - slot):
            c.start()

    for c in copies(s_i, slot):
        c.wait()

    qbs = [_round_bf16(q_ref[0, h]) for h in range(H)]
    scores = []
    for r in range(n_sel):
        qk_rows, dist_rows = [], []
        for h in range(H):
            blk = block_of(s_i, h, r)
            dist_rows.append((past_len - (blk * MOBA_BLOCK + (r % ppb) * page + off)).astype(F32))
            qk_rows.append(jnp.sum(_round_bf16(kbuf[slot, h * n_sel + r]) * qbs[h], axis=0, keepdims=True))
        scores.append(jnp.concatenate(qk_rows, axis=0) * scale
                      - slope_ref[...] * jnp.concatenate(dist_rows, axis=0))
    s_own = jnp.concatenate([jnp.sum(qbs[h] * _round_bf16(kn_ref[0, h]), axis=0, keepdims=True)
                             for h in range(H)], axis=0) * scale
    m = s_own
    for s in scores:
        m = jnp.maximum(m, jnp.max(s, axis=1, keepdims=True))
    p_own = jnp.exp(s_own - m)
    ps = [jnp.exp(s - m) for s in scores]
    l = p_own
    for p in ps:
        l = l + jnp.sum(p, axis=1, keepdims=True)
    inv = 1.0 / l
    pn = [_round_bf16(p * inv) for p in ps]
    pn_own = _round_bf16(p_own * inv)
    for h in range(H):
        acc = pn[0][h:h + 1, :] * _round_bf16(vbuf[slot, h * n_sel])
        for r in range(1, n_sel):
            acc = acc + pn[r][h:h + 1, :] * _round_bf16(vbuf[slot, h * n_sel + r])
        o_ref[0, h] = jnp.sum(acc, axis=1, keepdims=True) + pn_own[h:h + 1, :] * _round_bf16(vn_ref[0, h])


def _attn_sample(q, k_new, v_new, ck, cv, page_table, top, slopes_page):
    n_pool, H, Dh, page = ck.shape
    S, n_pages = page_table.shape
    ppb = MOBA_BLOCK // page
    n_sel = MOBA_TOPK * ppb
    past_len = n_pages * page

    col = pl.BlockSpec((1, H, Dh, 1), lambda s, pt, tp: (s, 0, 0, 0))
    cols = lambda a: a.reshape(S, H, Dh, 1)
    hbm = pl.BlockSpec(memory_space=pl.ANY)
    tiles = pltpu.VMEM((2, H * n_sel, Dh, page), F32)
    out = pl.pallas_call(
        functools.partial(_attn_sample_kernel, n_sel=n_sel, n_pages=n_pages, page=page, past_len=past_len),
        grid_spec=pltpu.PrefetchScalarGridSpec(
            num_scalar_prefetch=2,
            grid=(S,),
            in_specs=[col, col, col, pl.BlockSpec((H, page), lambda s, pt, tp: (0, 0)), hbm, hbm],
            out_specs=col,
            scratch_shapes=[tiles, tiles, pltpu.SemaphoreType.DMA((2, 2))],
        ),
        out_shape=jax.ShapeDtypeStruct((S, H, Dh, 1), F32),
        compiler_params=_cparams(1),
        name="attn_sample",
    )(page_table.reshape(-1), top.reshape(-1), cols(q), cols(k_new), cols(v_new), slopes_page, ck, cv)
    return out.reshape(S, H * Dh)


def _ssm_sample_kernel(u_ref, sre_ref, sim_ref, are_ref, aim_ref, bbre_ref, bbim_ref, cre_ref, cim_ref, d_ref,
                       y_ref, nre_ref, nim_ref):
    u = u_ref[...]
    ub = u.astype(BF16)
    ar, ai = are_ref[...], aim_ref[...]
    s0r, s0i = sre_ref[...], sim_ref[...]
    nr = ar * s0r - ai * s0i + _dot(ub, bbre_ref[...])
    ni = ar * s0i + ai * s0r + _dot(ub, bbim_ref[...])
    nre_ref[...] = nr
    nim_ref[...] = ni
    y_ref[...] = (_dot(nr.astype(BF16), cre_ref[...]) - _dot(ni.astype(BF16), cim_ref[...])
                  + d_ref[...] * u)


def _block_diag(x):
    G, r, c = x.shape
    eye = jnp.eye(G, dtype=x.dtype)
    return (x[:, :, None, :] * eye[:, None, :, None]).reshape(G * r, G * c)


def _ssm_sample(u, s_re, s_im, tb, c_re, c_im, d):
    S = u.shape[0]
    GN = N_SSM_GROUPS * SSM_STATE
    bb_re = _block_diag(jnp.transpose(tb["bb_re"], (0, 2, 1))).astype(BF16)
    bb_im = _block_diag(jnp.transpose(tb["bb_im"], (0, 2, 1))).astype(BF16)
    cc_re = _block_diag(jnp.transpose(c_re, (0, 2, 1))).astype(BF16)
    cc_im = _block_diag(jnp.transpose(c_im, (0, 2, 1))).astype(BF16)
    st = jax.ShapeDtypeStruct((S, GN), F32)
    return pl.pallas_call(
        _ssm_sample_kernel,
        out_shape=[jax.ShapeDtypeStruct((S, SSM_WIDTH), F32), st, st],
        compiler_params=pltpu.CompilerParams(vmem_limit_bytes=V7X_VMEM_LIMIT),
        name="ssm_sample",
    )(u, s_re.reshape(S, GN), s_im.reshape(S, GN), tb["abar_re"].reshape(1, GN), tb["abar_im"].reshape(1, GN),
      bb_re, bb_im, cc_re, cc_im, d.reshape(1, SSM_WIDTH))


def _layer_prompt(x, lw):
    kt, vt, u4, qa, ka, va, sel, qn2, kn2 = _inproj_prompt(x, lw["g_mix_pre"], lw["w_in"], lw["k_aug"])
    keep = _alibi_keep_blocks(lw["slopes"], qn2[:, 0], kn2[:, 0], ka.shape[1])
    attn2 = _attn_prompt(keep, qa, ka, va, sel, lw["slopes_q"])
    y4, f_re, f_im = _ssm_prompt(u4, *lw["ssm_chunk"])
    out, conv = _ffn_prompt(x, attn2, y4, lw["ffn"])
    return (out, kt, vt, f_re.reshape(N_SSM_GROUPS, SSM_STATE), f_im.reshape(N_SSM_GROUPS, SSM_STATE),
            conv[8 - (CONV_W - 1):])


def _layer_sample(x, cache_k, cache_v, page_table, s_re, s_im, conv_buf, lw):
    S = x.shape[0]
    proj = _inproj_sample(x, lw["g_mix_pre"], lw["w_in"])
    q = proj[:, :ATTN_WIDTH]
    k = proj[:, ATTN_WIDTH:2 * ATTN_WIDTH]
    v = proj[:, 2 * ATTN_WIDTH:3 * ATTN_WIDTH]
    u = proj[:, 3 * ATTN_WIDTH:]
    ck = jnp.transpose(cache_k, (0, 2, 3, 1))
    cv = jnp.transpose(cache_v, (0, 2, 3, 1))
    top = _top_sample(_gate_paged(q, ck, page_table))
    attn = _attn_sample(q, k, v, ck, cv, page_table, top, lw["slopes_page"])
    y, n_re, n_im = _ssm_sample(u, s_re, s_im, lw["ssm_tb"], lw["c_re"], lw["c_im"], lw["d"])
    out, g = _ffn_sample(x, attn.astype(BF16), y, conv_buf[:, 0], conv_buf[:, 1], lw["ffn"])
    conv_new = jnp.stack([conv_buf[:, 1], g], axis=1)
    return (out, k, v, n_re.reshape(S, N_SSM_GROUPS, SSM_STATE), n_im.reshape(S, N_SSM_GROUPS, SSM_STATE),
            conv_new)


def _alibi_key_table(slopes):
    off = jnp.arange(MOBA_BLOCK, dtype=F32)[None, :] * (slopes * LOG2E)[:, None]
    to_bf16 = lambda a: lax.reduce_precision(a, exponent_bits=8, mantissa_bits=7)
    t0 = to_bf16(off)
    t1 = to_bf16(off - t0)
    t2 = to_bf16(off - t0 - t1)
    terms = jnp.stack([t0, t1, t2], axis=-1).astype(BF16)
    half = jnp.pad(terms, ((0, 0), (0, 0), (0, HEAD_DIM - 3)))
    zero = jnp.zeros_like(half)
    odd = (jnp.arange(N_HEADS) % 2 == 1)[:, None, None]
    return jnp.where(odd, jnp.concatenate([half, zero], axis=-1), jnp.concatenate([zero, half], axis=-1))


def kernel(x_prompt, x_sample, cache_k, cache_v, page_table, state_ssm_re, state_ssm_im, state_conv,
           norm_mix_pre, norm_mix_post, w_in, ssm_a_re, ssm_a_im, ssm_log_step, ssm_b_re, ssm_b_im,
           ssm_c_re, ssm_c_im, ssm_d, w_glu, b_glu, w_out, norm_ffn_pre, norm_ffn_post,
           w_gate, w_up, conv_w, conv_b, w_down):
    depth = w_in.shape[0]
    bp, lp_len = x_prompt.shape[:2]
    bs, ls_len = x_sample.shape[:2]
    page = cache_k.shape[2]
    assert bp == 1 and ls_len == 1 and lp_len % ATTN_TQ == 0 and page == LANES
    assert page_table.shape[1] % PAGES_PER_STEP == 0
    slopes = jnp.exp2(-8.0 * jnp.arange(1, N_HEADS + 1, dtype=F32) / N_HEADS)
    hp = x_prompt[0]
    hs = x_sample[:, 0]
    outs = [[] for _ in range(10)]
    for l in range(depth):
        tb = _ssm_tables(ssm_a_re[l], ssm_a_im[l], ssm_log_step[l], ssm_b_re[l], ssm_b_im[l])
        row = lambda a: a[l].reshape(1, -1)
        lw = dict(
            g_mix_pre=row(norm_mix_pre), w_in=w_in[l].astype(BF16),
            k_aug=_alibi_key_table(slopes),
            slopes=slopes,
            slopes_q=jnp.broadcast_to((slopes * LOG2E)[:, None, None], (N_HEADS, 1, MOBA_BLOCK)),
            slopes_page=jnp.broadcast_to(slopes[:, None], (N_HEADS, page)),
            ssm_tb=tb, c_re=ssm_c_re[l], c_im=ssm_c_im[l], d=ssm_d[l],
            ssm_chunk=_ssm_chunk_tables(tb, ssm_c_re[l], ssm_c_im[l], ssm_d[l]),
            ffn=[w_glu[l].astype(BF16), row(b_glu), w_out[l, :ATTN_WIDTH].astype(BF16),
                 w_out[l, ATTN_WIDTH:].astype(BF16), row(norm_mix_post), row(norm_ffn_pre),
                 w_gate[l].astype(BF16), w_up[l].astype(BF16), conv_w[l], row(conv_b),
                 w_down[l].astype(BF16), row(norm_ffn_post)],
        )
        hp, ktp, vtp, sr, si, cp = _layer_prompt(hp, lw)
        outs[0].append(jnp.transpose(ktp, (2, 0, 1))[None])
        outs[1].append(jnp.transpose(vtp, (2, 0, 1))[None])
        outs[4].append(sr[None])
        outs[5].append(si[None])
        outs[8].append(cp[None])
        hs, ks, vs, sr, si, cs = _layer_sample(hs, cache_k[l], cache_v[l], page_table, state_ssm_re[l],
                                               state_ssm_im[l], state_conv[l], lw)
        outs[2].append(ks.reshape(bs, ls_len, N_HEADS, HEAD_DIM))
        outs[3].append(vs.reshape(bs, ls_len, N_HEADS, HEAD_DIM))
        outs[6].append(sr)
        outs[7].append(si)
        outs[9].append(cs)
    return (hp[None], hs[:, None], *[jnp.stack(o) for o in outs])
```

```python
import functools
import math

import jax
import jax.numpy as jnp
from jax import lax
from jax.experimental import pallas as pl
from jax.experimental.pallas import tpu as pltpu

F32 = jnp.float32
BF16 = jnp.bfloat16

D_MODEL = 1024
N_HEADS = 8
HEAD_DIM = 64
ATTN_WIDTH = N_HEADS * HEAD_DIM
SSM_WIDTH = D_MODEL - ATTN_WIDTH
MOBA_BLOCK = 256
MOBA_TOPK = 3
SSM_GROUP = 16
N_SSM_GROUPS = SSM_WIDTH // SSM_GROUP
SSM_STATE = 64
SSM_CHUNK = 16
CONV_W = 3
RMS_EPS = 1e-6
NEG = -1e30
LOG2E = 1.4426950408889634
LANES = 128
V7X_VMEM_LIMIT = 56 * 1024 * 1024
HI = lax.Precision.HIGHEST

ROW_TILE = 512
V_ROWS = 80
SLABS = SSM_WIDTH // LANES
GROUPS_PER_SLAB = LANES // SSM_GROUP


def _cparams(n_axes, vmem=None):
    return pltpu.CompilerParams(dimension_semantics=("arbitrary",) * n_axes, vmem_limit_bytes=vmem)


def _rms(x, g):
    return x * lax.rsqrt(jnp.mean(x * x, axis=-1, keepdims=True) + RMS_EPS) * g


def _gelu_tanh(x):
    return 0.5 * x * (1.0 + jnp.tanh(math.sqrt(2.0 / math.pi) * (x + 0.044715 * (x * x * x))))


def _sigmoid(x):
    return 1.0 / (1.0 + jnp.exp(-x))


def _split_bf16(a):
    hi = a.astype(BF16)
    lo = (a - hi.astype(F32)).astype(BF16)
    return hi, lo


def _round_bf16(a):
    return a.astype(BF16).astype(F32)


def _dot(a, b):
    return jnp.dot(a, b, preferred_element_type=F32)


def _dot_nt(a, b):
    return lax.dot_general(a, b, (((1,), (1,)), ((), ())), preferred_element_type=F32)


def _top_blocks(cur, blk, n):
    picks = []
    for _ in range(MOBA_TOPK):
        mx = jnp.max(cur, axis=0, keepdims=True)
        first = jnp.min(jnp.where(cur == mx, blk, float(n)), axis=0, keepdims=True)
        picks.append((first, mx))
        cur = jnp.where(blk == first, NEG, cur)
    return picks


def _inproj_prompt_kernel(x_ref, g_ref, w_ref, aug_ref, kt_ref, vt_ref, u_ref, qa_ref, ka_ref, va_ref, sel_ref,
                          qn_ref, kn_ref, kmean_s, *, tl, nb):
    i = pl.program_id(0)
    bpt = tl // MOBA_BLOCK
    H, Dh, B = N_HEADS, HEAD_DIM, MOBA_BLOCK

    @pl.when(i == 0)
    def _():
        kmean_s[...] = jnp.zeros_like(kmean_s)
        qn_ref[...] = jnp.zeros_like(qn_ref)
        kn_ref[...] = jnp.zeros_like(kn_ref)

    h = _rms(x_ref[...], g_ref[...]).astype(BF16)
    proj = _dot(h, w_ref[...])
    q = proj[:, :ATTN_WIDTH]
    k = proj[:, ATTN_WIDTH:2 * ATTN_WIDTH]
    v = proj[:, 2 * ATTN_WIDTH:3 * ATTN_WIDTH]
    for s in range(SLABS):
        u_ref[s] = proj[:, 3 * ATTN_WIDTH + s * LANES:3 * ATTN_WIDTH + (s + 1) * LANES]

    kt = k.T
    vt = v.T
    qt = (q * (Dh ** -0.5 * LOG2E)).T.astype(BF16)
    kt_ref[...] = kt.reshape(H, Dh, tl)
    vt_ref[...] = vt.reshape(H, Dh, tl)

    def head_norm2(t):
        t = t.astype(F32)
        n2 = jnp.sum((t * t).reshape(H, Dh, tl), axis=1)
        return jnp.broadcast_to(jnp.max(n2, axis=1, keepdims=True), (H, LANES))

    qn_ref[...] = jnp.maximum(qn_ref[...], head_norm2(qt))
    kn_ref[...] = jnp.maximum(kn_ref[...], head_norm2(kt))

    sub = lax.broadcasted_iota(jnp.int32, (Dh, tl), 0)
    ones3 = jnp.where(sub < 3, 1.0, 0.0).astype(BF16)
    kb = k.astype(BF16)
    lane_hi = lax.broadcasted_iota(jnp.int32, (B, LANES), 1) >= Dh
    vtb = vt.astype(BF16)
    ones_rows = jnp.ones((V_ROWS - Dh, B), BF16)
    for hd in range(H):
        qh = qt[hd * Dh:(hd + 1) * Dh, :]
        odd = hd % 2 == 1
        qa_ref[hd] = jnp.concatenate([ones3, qh] if odd else [qh, ones3], axis=0)
        for b in range(bpt):
            slab = kb[b * B:(b + 1) * B, (hd // 2) * LANES:(hd // 2 + 1) * LANES]
            ka_ref[hd, b] = jnp.where(lane_hi == odd, slab, aug_ref[hd])
            va_ref[hd, b] = jnp.concatenate([vtb[hd * Dh:(hd + 1) * Dh, b * B:(b + 1) * B], ones_rows], axis=0)

    row = lax.broadcasted_iota(jnp.int32, kmean_s.shape, 0)
    km = kmean_s[...]
    for b in range(bpt):
        kmb = jnp.mean(k[b * B:(b + 1) * B, :], axis=0, keepdims=True)
        km = jnp.where(row == i * bpt + b, kmb, km)
    kmean_s[...] = km

    blk = lax.broadcasted_iota(jnp.int32, (nb, tl), 0).astype(F32)
    own = ((i * tl + lax.broadcasted_iota(jnp.int32, (1, tl), 1)) // B).astype(F32)
    kmb16 = km.astype(BF16)
    qb16 = q.astype(BF16)
    for hd in range(H):
        sl = slice(hd * Dh, (hd + 1) * Dh)
        gate = _dot_nt(kmb16[:, sl], qb16[:, sl])
        chosen = jnp.zeros((nb, tl), F32)
        for first, mx in _top_blocks(jnp.where(blk < own, gate, NEG), blk, nb):
            chosen = jnp.where((blk == first) & (mx > 0.5 * NEG), 1.0, chosen)
        sel_ref[hd] = jnp.where(chosen > 0.5, 0.0, NEG)


def _inproj_prompt(x, g, w_bf, aug):
    L = x.shape[0]
    tl = ROW_TILE
    nb = L // MOBA_BLOCK
    bpt = tl // MOBA_BLOCK
    H, Dh, B = N_HEADS, HEAD_DIM, MOBA_BLOCK
    tcol = pl.BlockSpec((H, Dh, tl), lambda i: (0, 0, i))
    return pl.pallas_call(
        functools.partial(_inproj_prompt_kernel, tl=tl, nb=nb),
        grid=(L // tl,),
        in_specs=[pl.BlockSpec((tl, D_MODEL), lambda i: (i, 0)),
                  pl.BlockSpec((1, D_MODEL), lambda i: (0, 0)),
                  pl.BlockSpec((D_MODEL, 4 * ATTN_WIDTH), lambda i: (0, 0)),
                  pl.BlockSpec((H, B, LANES), lambda i: (0, 0, 0))],
        out_specs=[tcol, tcol,
                   pl.BlockSpec((SLABS, tl, LANES), lambda i: (0, i, 0)),
                   pl.BlockSpec((H, 2 * Dh, tl), lambda i: (0, 0, i)),
                   pl.BlockSpec((H, bpt, B, LANES), lambda i: (0, i, 0, 0)),
                   pl.BlockSpec((H, bpt, V_ROWS, B), lambda i: (0, i, 0, 0)),
                   pl.BlockSpec((H, nb, tl), lambda i: (0, 0, i)),
                   pl.BlockSpec((H, LANES), lambda i: (0, 0)),
                   pl.BlockSpec((H, LANES), lambda i: (0, 0))],
        out_shape=[jax.ShapeDtypeStruct((H, Dh, L), F32), jax.ShapeDtypeStruct((H, Dh, L), F32),
                   jax.ShapeDtypeStruct((SLABS, L, LANES), F32),
                   jax.ShapeDtypeStruct((H, 2 * Dh, L), BF16),
                   jax.ShapeDtypeStruct((H, nb, B, LANES), BF16),
                   jax.ShapeDtypeStruct((H, nb, V_ROWS, B), BF16),
                   jax.ShapeDtypeStruct((H, nb, L), F32),
                   jax.ShapeDtypeStruct((H, LANES), F32), jax.ShapeDtypeStruct((H, LANES), F32)],
        scratch_shapes=[pltpu.VMEM((nb, ATTN_WIDTH), F32)],
        compiler_params=_cparams(1, V7X_VMEM_LIMIT),
        name="inproj_prompt",
    )(x, g, w_bf, aug)


HEADS_PER_STEP = 2
ATTN_TQ = 1024
UNDERFLOW_LOG2 = 160.0


def _attn_prompt_kernel(w_ref, qa_ref, ka_ref, va_ref, sel_ref, slope_ref, o_ref, s_scr, d_scr, acc_scr, *, tq):
    hp = pl.program_id(0)
    qi = pl.program_id(1)
    B, Dh = MOBA_BLOCK, HEAD_DIM
    bpq = tq // B
    units = [(e, cb) for e in range(HEADS_PER_STEP) for cb in range(bpq)]
    n_off = qi * bpq + (bpq - 1)
    keep = w_ref[hp * HEADS_PER_STEP]
    for e in range(1, HEADS_PER_STEP):
        keep = jnp.maximum(keep, w_ref[hp * HEADS_PER_STEP + e])
    j_start = jnp.maximum(qi * bpq - keep, 0)
    lane = lax.broadcasted_iota(jnp.int32, (1, B), 1).astype(F32)
    causal = lax.broadcasted_iota(jnp.int32, (B, B), 0) <= lax.broadcasted_iota(jnp.int32, (B, B), 1)

    def q_of(e, cb):
        return qa_ref[e, :, cb * B:(cb + 1) * B]

    def slope_of(e):
        return slope_ref[e]

    for u, (e, cb) in enumerate(units):
        d_scr[u] = _dot(ka_ref[e, qi * bpq + cb], q_of(e, cb))
        s_scr[u] = _dot(ka_ref[e, j_start], q_of(e, cb))
    ms = []
    for u, (e, cb) in enumerate(units):
        s = jnp.where(causal, d_scr[u] - slope_of(e) * lane, NEG)
        m = jnp.max(s, axis=0, keepdims=True)
        acc_scr[u] = _dot(va_ref[e, qi * bpq + cb], jnp.exp2(s - m).astype(BF16))
        ms.append(m)

    def step(j, ms):
        nxt = jnp.minimum(j + 1, n_off)
        out = []
        for u, (e, cb) in enumerate(units):
            s = s_scr[u]
            dist = lane + ((qi * bpq + cb - j) * B).astype(F32)
            col = sel_ref[e, pl.ds(j, 1), cb * B:(cb + 1) * B] - slope_of(e) * dist
            m_new = jnp.maximum(ms[u], jnp.max(s, axis=0, keepdims=True) + col)
            p = jnp.exp2(s - (m_new - col)).astype(BF16)
            acc_scr[u] = jnp.exp2(ms[u] - m_new) * acc_scr[u] + _dot(va_ref[e, j], p)
            s_scr[u] = _dot(ka_ref[e, nxt], q_of(e, cb))
            out.append(m_new)
        return tuple(out)

    def body(t, ms):
        j = j_start + 2 * t
        return step(j + 1, step(j, ms))

    lax.fori_loop(0, (n_off - j_start + 1) // 2, body, tuple(ms))
    for cb in range(bpq):
        outs = []
        for e in range(HEADS_PER_STEP):
            acc = acc_scr[e * bpq + cb]
            outs.append((acc[:Dh, :] / acc[Dh:Dh + 1, :]).T)
        o_ref[0, cb * B:(cb + 1) * B, :] = jnp.concatenate(outs, axis=1).astype(o_ref.dtype)


def _alibi_keep_blocks(slopes, qn2, kn2, nb):
    qk = jnp.sqrt(qn2 * kn2) * 1.05
    need = (2.0 * qk + UNDERFLOW_LOG2) / (slopes * LOG2E)
    w = jnp.ceil((need - 1.0) / MOBA_BLOCK)
    return jnp.clip(w, 1.0, float(nb)).astype(jnp.int32)


def _attn_prompt(keep, qa, ka, va, sel, slopes):
    H, nb, B, _ = ka.shape
    L = qa.shape[2]
    tq = ATTN_TQ
    hp = HEADS_PER_STEP
    return pl.pallas_call(
        functools.partial(_attn_prompt_kernel, tq=tq),
        grid_spec=pltpu.PrefetchScalarGridSpec(
            num_scalar_prefetch=1,
            grid=(H // hp, L // tq),
            in_specs=[pl.BlockSpec((hp, 2 * HEAD_DIM, tq), lambda h, i, w: (h, 0, i)),
                      pl.BlockSpec((hp, nb, B, LANES), lambda h, i, w: (h, 0, 0, 0)),
                      pl.BlockSpec((hp, nb, V_ROWS, B), lambda h, i, w: (h, 0, 0, 0)),
                      pl.BlockSpec((hp, nb, tq), lambda h, i, w: (h, 0, i)),
                      pl.BlockSpec((hp, 1, B), lambda h, i, w: (h, 0, 0))],
            out_specs=pl.BlockSpec((1, tq, hp * HEAD_DIM), lambda h, i, w: (h, i, 0)),
            scratch_shapes=[pltpu.VMEM((hp * tq // B, B, B), F32), pltpu.VMEM((hp * tq // B, B, B), F32),
                            pltpu.VMEM((hp * tq // B, V_ROWS, B), F32)],
        ),
        out_shape=jax.ShapeDtypeStruct((H // hp, L, hp * HEAD_DIM), BF16),
        compiler_params=_cparams(2, V7X_VMEM_LIMIT),
        name="attn_prompt",
    )(keep, qa, ka, va, sel, slopes)


def _ssm_tables(a_re, a_im, log_step, b_re, b_im):
    T = SSM_CHUNK
    dt = jnp.exp(log_step)[:, None]
    j = jnp.arange(T + 1, dtype=F32)[:, None, None]
    mag = jnp.exp(a_re * dt * j)
    pw_re = mag * jnp.cos(a_im * dt * j)
    pw_im = mag * jnp.sin(a_im * dt * j)
    abar_re, abar_im = pw_re[1], pw_im[1]
    den = a_re * a_re + a_im * a_im
    nr = abar_re - 1.0
    ni = abar_im
    coef_re = (nr * a_re + ni * a_im) / den
    coef_im = (ni * a_re - nr * a_im) / den
    bb_re = coef_re[..., None] * b_re - coef_im[..., None] * b_im
    bb_im = coef_re[..., None] * b_im + coef_im[..., None] * b_re
    return dict(pw_re=pw_re, pw_im=pw_im, bb_re=bb_re, bb_im=bb_im, abar_re=abar_re, abar_im=abar_im)


def _spread_groups(compact, rows_per_group, cols_per_group, col_outer):
    gs = GROUPS_PER_SLAB
    rows = compact.shape[-2]
    src = jnp.arange(col_outer * cols_per_group)
    dst = jnp.arange(col_outer * gs * cols_per_group)
    same_outer = src[:, None] // cols_per_group == dst[None, :] // (gs * cols_per_group)
    same_c = src[:, None] % cols_per_group == dst[None, :] % cols_per_group
    rep = (same_outer & same_c).astype(BF16)
    row_group = (jnp.arange(rows) // rows_per_group) % gs
    col_group = (dst // cols_per_group) % gs
    wide = jnp.dot(compact, rep, preferred_element_type=F32)
    return jnp.where(row_group[:, None] == col_group[None, :], wide, 0.0).astype(BF16)


def _ssm_chunk_tables(tb, c_re, c_im, d):
    pw_re, pw_im, bb_re, bb_im = tb["pw_re"], tb["pw_im"], tb["bb_re"], tb["bb_im"]
    T = SSM_CHUNK
    G, N, P = bb_re.shape
    x_re = pw_re[:T, :, :, None] * bb_re[None] - pw_im[:T, :, :, None] * bb_im[None]
    x_im = pw_re[:T, :, :, None] * bb_im[None] + pw_im[:T, :, :, None] * bb_re[None]
    kj = (jnp.einsum("gpn,jgnq->jgqp", c_re, x_re, precision=HI)
          - jnp.einsum("gpn,jgnq->jgqp", c_im, x_im, precision=HI))
    kj = kj.at[0].add(jnp.eye(P, dtype=F32)[None] * d[:, :, None])
    gs = GROUPS_PER_SLAB
    kpad = jnp.concatenate([jnp.zeros_like(kj[:1]), kj], axis=0)
    kc = jnp.stack([kpad[:T], kpad[1:]], axis=3).reshape(T, SLABS, gs, P, 2, P)
    kc = jnp.transpose(kc, (1, 0, 2, 3, 4, 5)).reshape(SLABS, T * LANES, 2 * P)
    lag = _spread_groups(kc.astype(BF16), P, P, 2)
    xc = jnp.stack([x_re[::-1], x_im[::-1]], axis=2).reshape(T, SLABS, gs, 2, N, P)
    xc = jnp.transpose(xc, (1, 0, 2, 5, 3, 4)).reshape(SLABS, T * LANES, 2 * N)
    f = _spread_groups(xc.astype(BF16), P, N, 2)
    cr = jnp.transpose(c_re, (0, 2, 1))[None]
    ci = jnp.transpose(c_im, (0, 2, 1))[None]
    ar = pw_re[1:T + 1, :, :, None]
    ai = pw_im[1:T + 1, :, :, None]
    ec = jnp.stack([cr * ar - ci * ai, -(cr * ai + ci * ar)], axis=1)
    ec = ec.reshape(T // 2, 2, 2, SLABS, gs, N, P)
    ec = jnp.transpose(ec, (3, 0, 2, 4, 5, 1, 6)).reshape(SLABS, T // 2, 2 * gs * N, 2 * P)
    e = _spread_groups(ec.astype(BF16), N, P, 2)
    a16_re = pw_re[T].reshape(1, G * N)
    a16_im = pw_im[T].reshape(1, G * N)
    return lag, f, e, a16_re, a16_im


def _chunk_steps(u_ref, rows):
    return [u_ref[0, pl.ds(s, rows, stride=SSM_CHUNK), :].astype(BF16) for s in range(SSM_CHUNK)]


def _ssm_chunk_in_kernel(u_ref, f_ref, bre_ref, bim_ref, *, rows):
    b = _dot(jnp.concatenate(_chunk_steps(u_ref, rows), axis=1), f_ref[0])
    half = b.shape[1] // 2
    bre_ref[...] = b[:, :half]
    bim_ref[...] = b[:, half:]


def _ssm_scan_kernel(bre_ref, bim_ref, are_ref, aim_ref, sre_ref, sim_ref, fre_ref, fim_ref):
    nc = bre_ref.shape[0]
    ar = are_ref[...]
    ai = aim_ref[...]

    def body(c8, carry):
        sr, si = carry
        r0 = pl.multiple_of(c8 * 8, 8)
        br = bre_ref[pl.ds(r0, 8), :]
        bi = bim_ref[pl.ds(r0, 8), :]
        rows_r, rows_i = [], []
        for r in range(8):
            rows_r.append(sr)
            rows_i.append(si)
            sr, si = (ar * sr - ai * si + br[r:r + 1, :], ar * si + ai * sr + bi[r:r + 1, :])
        sre_ref[pl.ds(r0, 8), :] = jnp.concatenate(rows_r, axis=0)
        sim_ref[pl.ds(r0, 8), :] = jnp.concatenate(rows_i, axis=0)
        return sr, si

    z = jnp.zeros(are_ref.shape, F32)
    sr, si = lax.fori_loop(0, nc // 8, body, (z, z))
    fre_ref[...] = sr
    fim_ref[...] = si


def _ssm_chunk_out_kernel(u_ref, lag_ref, e_ref, sre_ref, sim_ref, y_ref, *, rows):
    us = _chunk_steps(u_ref, rows)
    s = jnp.concatenate([sre_ref[...], sim_ref[...]], axis=1).astype(BF16)
    for pair in range(SSM_CHUNK // 2):
        tau = 2 * pair
        lhs = jnp.concatenate(us[tau + 1::-1], axis=1)
        y2 = _dot(lhs, lag_ref[0, :LANES * (tau + 2), :]) + _dot(s, e_ref[0, pair])
        y_ref[0, pl.ds(tau, rows, stride=SSM_CHUNK), :] = y2[:, :LANES]
        y_ref[0, pl.ds(tau + 1, rows, stride=SSM_CHUNK), :] = y2[:, LANES:]


SSM_ROWS = 512


def _ssm_prompt(u4, lag, f, e, a16_re, a16_im):
    L = u4.shape[1]
    nc = L // SSM_CHUNK
    rows = min(SSM_ROWS, nc)
    GN = N_SSM_GROUPS * SSM_STATE
    SW = GROUPS_PER_SLAB * SSM_STATE
    st = jax.ShapeDtypeStruct((nc, GN), F32)
    slab_rows = pl.BlockSpec((1, rows * SSM_CHUNK, LANES), lambda s, r: (s, r, 0))
    state_cols = pl.BlockSpec((rows, SW), lambda s, r: (r, s))
    b_re, b_im = pl.pallas_call(
        functools.partial(_ssm_chunk_in_kernel, rows=rows),
        grid=(SLABS, nc // rows),
        in_specs=[slab_rows, pl.BlockSpec((1,) + f.shape[1:], lambda s, r: (s, 0, 0))],
        out_specs=[state_cols, state_cols],
        out_shape=[st, st],
        compiler_params=_cparams(2, V7X_VMEM_LIMIT),
        name="ssm_chunk_in",
    )(u4, f)
    fin = jax.ShapeDtypeStruct((1, GN), F32)
    s_re, s_im, f_re, f_im = pl.pallas_call(
        _ssm_scan_kernel,
        out_shape=[st, st, fin, fin],
        compiler_params=pltpu.CompilerParams(vmem_limit_bytes=V7X_VMEM_LIMIT),
        name="ssm_scan",
    )(b_re, b_im, a16_re, a16_im)
    y4 = pl.pallas_call(
        functools.partial(_ssm_chunk_out_kernel, rows=rows),
        grid=(SLABS, nc // rows),
        in_specs=[slab_rows,
                  pl.BlockSpec((1,) + lag.shape[1:], lambda s, r: (s, 0, 0)),
                  pl.BlockSpec((1,) + e.shape[1:], lambda s, r: (s, 0, 0, 0)),
                  state_cols, state_cols],
        out_specs=slab_rows,
        out_shape=jax.ShapeDtypeStruct(u4.shape, F32),
        compiler_params=_cparams(2, V7X_VMEM_LIMIT),
        name="ssm_chunk_out",
    )(u4, lag, e, s_re, s_im)
    return y4, f_re, f_im


FF_CHUNK = 256


def _mix_and_prenorm(x, attn_bf, y, wglu_ref, bglu_ref, wouta_ref, wouts_ref, gpost_ref, gpre_ref):
    z = _gelu_tanh(y)
    ssm = z * _sigmoid(_dot(z.astype(BF16), wglu_ref[...]) + bglu_ref[...])
    mix = _dot(attn_bf, wouta_ref[...]) + _dot(ssm.astype(BF16), wouts_ref[...])
    x1 = x + _rms(mix, gpost_ref[...])
    h2 = _rms(x1, gpre_ref[...]).astype(BF16)
    return x1, h2


def _ffn_prompt_kernel(x_ref, attn_ref, y_ref, wglu_ref, bglu_ref, wouta_ref, wouts_ref, gpost_ref, gpre_ref,
                       wgate_ref, wup_ref, cw_ref, cb_ref, wdown_ref, gfpost_ref, out_ref, conv_ref, tail_s, act_s,
                       *, tl, dff):
    i = pl.program_id(0)

    @pl.when(i == 0)
    def _():
        tail_s[...] = jnp.zeros_like(tail_s)

    attn = jnp.concatenate([attn_ref[s] for s in range(attn_ref.shape[0])], axis=1)
    y = jnp.concatenate([y_ref[s] for s in range(SLABS)], axis=1)
    x1, h2 = _mix_and_prenorm(x_ref[...], attn, y, wglu_ref, bglu_ref, wouta_ref, wouts_ref, gpost_ref, gpre_ref)
    row = lax.broadcasted_iota(jnp.int32, (tl, FF_CHUNK), 0)
    for c in range(dff // FF_CHUNK):
        cs = slice(c * FF_CHUNK, (c + 1) * FF_CHUNK)
        g = _dot(h2, wgate_ref[:, cs])
        up = _dot(h2, wup_ref[:, cs])
        tail = tail_s[c]
        p1 = tail[7:8, :]
        p2 = tail[6:7, :]
        g1 = jnp.where(row == 0, p1, pltpu.roll(g, 1, 0))
        g2 = jnp.where(row == 0, p2, jnp.where(row == 1, p1, pltpu.roll(g, 2, 0)))
        gc = cw_ref[0:1, cs] * g2 + cw_ref[1:2, cs] * g1 + cw_ref[2:3, cs] * g + cb_ref[:, cs]
        act_s[:, cs] = (_gelu_tanh(gc) * up).astype(BF16)
        tail_s[c] = g[tl - 8:, :]
        conv_ref[:, cs] = g[tl - 8:, :]
    f = _dot(act_s[...], wdown_ref[...])
    out_ref[...] = x1 + _rms(f, gfpost_ref[...])


def _ffn_sample_kernel(x_ref, attn_ref, y_ref, b0_ref, b1_ref, wglu_ref, bglu_ref, wouta_ref, wouts_ref,
                       gpost_ref, gpre_ref, wgate_ref, wup_ref, cw_ref, cb_ref, wdown_ref, gfpost_ref,
                       out_ref, g_ref, *, dff):
    x1, h2 = _mix_and_prenorm(x_ref[...], attn_ref[...], y_ref[...], wglu_ref, bglu_ref, wouta_ref,
                              wouts_ref, gpost_ref, gpre_ref)
    f = jnp.zeros(x1.shape, F32)
    for c in range(dff // FF_CHUNK):
        cs = slice(c * FF_CHUNK, (c + 1) * FF_CHUNK)
        g = _dot(h2, wgate_ref[:, cs])
        up = _dot(h2, wup_ref[:, cs])
        gc = (cw_ref[0:1, cs] * b0_ref[:, cs] + cw_ref[1:2, cs] * b1_ref[:, cs] + cw_ref[2:3, cs] * g
              + cb_ref[:, cs])
        act = (_gelu_tanh(gc) * up).astype(BF16)
        f = f + _dot(act, wdown_ref[cs, :])
        g_ref[:, cs] = g
    out_ref[...] = x1 + _rms(f, gfpost_ref[...])


def _weight_specs(dff):
    c2 = lambda *_: (0, 0)
    full = lambda r, c: pl.BlockSpec((r, c), c2, pipeline_mode=pl.Buffered(1))
    return [full(SSM_WIDTH, SSM_WIDTH), full(1, SSM_WIDTH), full(ATTN_WIDTH, D_MODEL), full(SSM_WIDTH, D_MODEL),
            full(1, D_MODEL), full(1, D_MODEL), full(D_MODEL, dff), full(D_MODEL, dff), full(CONV_W, dff),
            full(1, dff), full(dff, D_MODEL), full(1, D_MODEL)]


def _ffn_prompt(x, attn2, y4, weights):
    L = x.shape[0]
    tl = ROW_TILE
    dff = weights[6].shape[1]
    rows = lambda w: pl.BlockSpec((tl, w), lambda i: (i, 0))
    slabs = lambda a: pl.BlockSpec((a.shape[0], tl, LANES), lambda i: (0, i, 0))
    return pl.pallas_call(
        functools.partial(_ffn_prompt_kernel, tl=tl, dff=dff),
        grid=(L // tl,),
        in_specs=[rows(D_MODEL), slabs(attn2), slabs(y4)] + _weight_specs(dff),
        out_specs=[rows(D_MODEL), pl.BlockSpec((8, dff), lambda i: (0, 0))],
        out_shape=[jax.ShapeDtypeStruct((L, D_MODEL), F32), jax.ShapeDtypeStruct((8, dff), F32)],
        scratch_shapes=[pltpu.VMEM((dff // FF_CHUNK, 8, FF_CHUNK), F32), pltpu.VMEM((tl, dff), BF16)],
        compiler_params=_cparams(1, V7X_VMEM_LIMIT),
        name="ffn_prompt",
    )(x, attn2, y4, *weights)


def _ffn_sample(x, attn_bf, y, buf0, buf1, weights):
    nb = x.shape[0]
    dff = weights[6].shape[1]
    rows = lambda w: pl.BlockSpec((nb, w), lambda i: (0, 0))
    return pl.pallas_call(
        functools.partial(_ffn_sample_kernel, dff=dff),
        grid=(1,),
        in_specs=[rows(D_MODEL), rows(ATTN_WIDTH), rows(SSM_WIDTH), rows(dff), rows(dff)] + _weight_specs(dff),
        out_specs=[rows(D_MODEL), rows(dff)],
        out_shape=[jax.ShapeDtypeStruct((nb, D_MODEL), F32), jax.ShapeDtypeStruct((nb, dff), F32)],
        compiler_params=_cparams(1, V7X_VMEM_LIMIT),
        name="ffn_sample",
    )(x, attn_bf, y, buf0, buf1, *weights)


def _inproj_sample_kernel(x_ref, g_ref, w_ref, o_ref):
    o_ref[...] = _dot(_rms(x_ref[...], g_ref[...]).astype(BF16), w_ref[...])


def _inproj_sample(x, g, w_bf):
    nb = x.shape[0]
    return pl.pallas_call(
        _inproj_sample_kernel,
        out_shape=jax.ShapeDtypeStruct((nb, w_bf.shape[1]), F32),
        compiler_params=pltpu.CompilerParams(vmem_limit_bytes=V7X_VMEM_LIMIT),
        name="inproj_sample",
    )(x, g, w_bf)


PAGES_PER_STEP = 32


def _gate_paged_kernel(pt_ref, q_ref, ind_ref, *refs, ppb):
    page_refs, o_ref = refs[:-1], refs[-1]
    qb = _round_bf16(q_ref[0])
    lane = lax.broadcasted_iota(jnp.int32, (ATTN_WIDTH, LANES), 1)
    prods = jnp.zeros((ATTN_WIDTH, LANES), F32)
    for b in range(len(page_refs) // ppb):
        tot = page_refs[b * ppb][0].reshape(ATTN_WIDTH, -1)
        for r in range(1, ppb):
            tot = tot + page_refs[b * ppb + r][0].reshape(ATTN_WIDTH, -1)
        kmean = jnp.sum(tot, axis=1, keepdims=True) * (1.0 / MOBA_BLOCK)
        prods = jnp.where(lane == b, _round_bf16(kmean) * qb, prods)
    p_hi, p_lo = _split_bf16(prods)
    o_ref[0, 0] = _dot(ind_ref[...], p_hi) + _dot(ind_ref[...], p_lo)


def _gate_paged(q, ck, page_table):
    n_pool, H, Dh, page = ck.shape
    S, n_pages = page_table.shape
    ppb = MOBA_BLOCK // page
    pps = PAGES_PER_STEP
    bps = pps // ppb
    nb = n_pages // ppb
    W = H * Dh
    ind = (jnp.arange(H)[:, None] == jnp.arange(W)[None, :] // Dh).astype(BF16)

    def page_spec(r):
        return pl.BlockSpec((1, H, Dh, page), lambda s, c, pt: (pt[s * n_pages + c * pps + r], 0, 0, 0))

    out = pl.pallas_call(
        functools.partial(_gate_paged_kernel, ppb=ppb),
        grid_spec=pltpu.PrefetchScalarGridSpec(
            num_scalar_prefetch=1,
            grid=(S, n_pages // pps),
            in_specs=[pl.BlockSpec((1, W, LANES), lambda s, c, pt: (s, 0, 0)),
                      pl.BlockSpec((H, W), lambda s, c, pt: (0, 0))]
            + [page_spec(r) for r in range(pps)],
            out_specs=pl.BlockSpec((1, 1, H, LANES), lambda s, c, pt: (s, c, 0, 0)),
        ),
        out_shape=jax.ShapeDtypeStruct((S, n_pages // pps, H, LANES), F32),
        compiler_params=_cparams(2, V7X_VMEM_LIMIT),
        name="gate_paged",
    )(page_table.reshape(-1), jnp.broadcast_to(q.reshape(S, W, 1), (S, W, LANES)), ind, *([ck] * pps))
    return jnp.transpose(out[..., :bps], (0, 1, 3, 2)).reshape(S, nb, H)


def _top_sample_kernel(g_ref, top_ref):
    gate = g_ref[0]
    nb = gate.shape[0]
    blk = lax.broadcasted_iota(jnp.int32, gate.shape, 0).astype(F32)
    picks = [first for first, _ in _top_blocks(gate, blk, nb)]
    top_ref[0] = jnp.concatenate(picks, axis=0).astype(jnp.int32)


def _top_sample(gates):
    S, nb, H = gates.shape
    return pl.pallas_call(
        _top_sample_kernel,
        grid=(S,),
        in_specs=[pl.BlockSpec((1, nb, H), lambda s: (s, 0, 0))],
        out_specs=pl.BlockSpec((1, MOBA_TOPK, H), lambda s: (s, 0, 0)),
        out_shape=jax.ShapeDtypeStruct((S, MOBA_TOPK, H), jnp.int32),
        compiler_params=_cparams(1),
        name="top_sample",
    )(gates)


def _attn_sample_kernel(pt_ref, top_ref, q_ref, kn_ref, vn_ref, slope_ref, ck_ref, cv_ref, o_ref,
                        kbuf, vbuf, sems, *, n_sel, n_pages, page, past_len):
    H = N_HEADS
    s_i = pl.program_id(0)
    n_seq = pl.num_programs(0)
    scale = HEAD_DIM ** -0.5
    ppb = MOBA_BLOCK // page
    off = lax.broadcasted_iota(jnp.int32, (1, page), 1)

    def block_of(seq, h, r):
        return top_ref[(seq * MOBA_TOPK + r // ppb) * H + h]

    def copies(seq, slot):
        out = []
        for h in range(H):
            for r in range(n_sel):
                pg = pt_ref[seq * n_pages + block_of(seq, h, r) * ppb + r % ppb]
                out.append(pltpu.make_async_copy(ck_ref.at[pg, h], kbuf.at[slot, h * n_sel + r], sems.at[0, slot]))
                out.append(pltpu.make_async_copy(cv_ref.at[pg, h], vbuf.at[slot, h * n_sel + r], sems.at[1, slot]))
        return out

    slot = s_i % 2

    @pl.when(s_i == 0)
    def _():
        for c in copies(s_i, slot):
            c.start()

    @pl.when(s_i + 1 < n_seq)
    def _():
        for c in copies(s_i + 1, 1 - slot):
            c.start()

    for c in copies(s_i, slot):
        c.wait()

    qbs = [_round_bf16(q_ref[0, h]) for h in range(H)]
    scores = []
    for r in range(n_sel):
        qk_rows, dist_rows = [], []
        for h in range(H):
            blk = block_of(s_i, h, r)
            dist_rows.append((past_len - (blk * MOBA_BLOCK + (r % ppb) * page + off)).astype(F32))
            qk_rows.append(jnp.sum(_round_bf16(kbuf[slot, h * n_sel + r]) * qbs[h], axis=0, keepdims=True))
        scores.append(jnp.concatenate(qk_rows, axis=0) * scale
                      - slope_ref[...] * jnp.concatenate(dist_rows, axis=0))
    s_own = jnp.concatenate([jnp.sum(qbs[h] * _round_bf16(kn_ref[0, h]), axis=0, keepdims=True)
                             for h in range(H)], axis=0) * scale
    m = s_own
    for s in scores:
        m = jnp.maximum(m, jnp.max(s, axis=1, keepdims=True))
    p_own = jnp.exp(s_own - m)
    ps = [jnp.exp(s - m) for s in scores]
    l = p_own
    for p in ps:
        l = l + jnp.sum(p, axis=1, keepdims=True)
    inv = 1.0 / l
    pn = [_round_bf16(p * inv) for p in ps]
    pn_own = _round_bf16(p_own * inv)
    for h in range(H):
        acc = pn[0][h:h + 1, :] * _round_bf16(vbuf[slot, h * n_sel])
        for r in range(1, n_sel):
            acc = acc + pn[r][h:h + 1, :] * _round_bf16(vbuf[slot, h * n_sel + r])
        o_ref[0, h] = jnp.sum(acc, axis=1, keepdims=True) + pn_own[h:h + 1, :] * _round_bf16(vn_ref[0, h])


def _attn_sample(q, k_new, v_new, ck, cv, page_table, top, slopes_page):
    n_pool, H, Dh, page = ck.shape
    S, n_pages = page_table.shape
    ppb = MOBA_BLOCK // page
    n_sel = MOBA_TOPK * ppb
    past_len = n_pages * page

    col = pl.BlockSpec((1, H, Dh, 1), lambda s, pt, tp: (s, 0, 0, 0))
    cols = lambda a: a.reshape(S, H, Dh, 1)
    hbm = pl.BlockSpec(memory_space=pl.ANY)
    tiles = pltpu.VMEM((2, H * n_sel, Dh, page), F32)
    out = pl.pallas_call(
        functools.partial(_attn_sample_kernel, n_sel=n_sel, n_pages=n_pages, page=page, past_len=past_len),
        grid_spec=pltpu.PrefetchScalarGridSpec(
            num_scalar_prefetch=2,
            grid=(S,),
            in_specs=[col, col, col, pl.BlockSpec((H, page), lambda s, pt, tp: (0, 0)), hbm, hbm],
            out_specs=col,
            scratch_shapes=[tiles, tiles, pltpu.SemaphoreType.DMA((2, 2))],
        ),
        out_shape=jax.ShapeDtypeStruct((S, H, Dh, 1), F32),
        compiler_params=_cparams(1),
        name="attn_sample",
    )(page_table.reshape(-1), top.reshape(-1), cols(q), cols(k_new), cols(v_new), slopes_page, ck, cv)
    return out.reshape(S, H * Dh)


def _ssm_sample_kernel(u_ref, sre_ref, sim_ref, are_ref, aim_ref, bbre_ref, bbim_ref, cre_ref, cim_ref, d_ref,
                       y_ref, nre_ref, nim_ref):
    u = u_ref[...]
    ub = u.astype(BF16)
    ar, ai = are_ref[...], aim_ref[...]
    s0r, s0i = sre_ref[...], sim_ref[...]
    nr = ar * s0r - ai * s0i + _dot(ub, bbre_ref[...])
    ni = ar * s0i + ai * s0r + _dot(ub, bbim_ref[...])
    nre_ref[...] = nr
    nim_ref[...] = ni
    y_ref[...] = (_dot(nr.astype(BF16), cre_ref[...]) - _dot(ni.astype(BF16), cim_ref[...])
                  + d_ref[...] * u)


def _block_diag(x):
    G, r, c = x.shape
    eye = jnp.eye(G, dtype=x.dtype)
    return (x[:, :, None, :] * eye[:, None, :, None]).reshape(G * r, G * c)


def _ssm_sample(u, s_re, s_im, tb, c_re, c_im, d):
    S = u.shape[0]
    GN = N_SSM_GROUPS * SSM_STATE
    bb_re = _block_diag(jnp.transpose(tb["bb_re"], (0, 2, 1))).astype(BF16)
    bb_im = _block_diag(jnp.transpose(tb["bb_im"], (0, 2, 1))).astype(BF16)
    cc_re = _block_diag(jnp.transpose(c_re, (0, 2, 1))).astype(BF16)
    cc_im = _block_diag(jnp.transpose(c_im, (0, 2, 1))).astype(BF16)
    st = jax.ShapeDtypeStruct((S, GN), F32)
    return pl.pallas_call(
        _ssm_sample_kernel,
        out_shape=[jax.ShapeDtypeStruct((S, SSM_WIDTH), F32), st, st],
        compiler_params=pltpu.CompilerParams(vmem_limit_bytes=V7X_VMEM_LIMIT),
        name="ssm_sample",
    )(u, s_re.reshape(S, GN), s_im.reshape(S, GN), tb["abar_re"].reshape(1, GN), tb["abar_im"].reshape(1, GN),
      bb_re, bb_im, cc_re, cc_im, d.reshape(1, SSM_WIDTH))


def _layer_prompt(x, lw):
    kt, vt, u4, qa, ka, va, sel, qn2, kn2 = _inproj_prompt(x, lw["g_mix_pre"], lw["w_in"], lw["k_aug"])
    keep = _alibi_keep_blocks(lw["slopes"], qn2[:, 0], kn2[:, 0], ka.shape[1])
    attn2 = _attn_prompt(keep, qa, ka, va, sel, lw["slopes_q"])
    y4, f_re, f_im = _ssm_prompt(u4, *lw["ssm_chunk"])
    out, conv = _ffn_prompt(x, attn2, y4, lw["ffn"])
    return (out, kt, vt, f_re.reshape(N_SSM_GROUPS, SSM_STATE), f_im.reshape(N_SSM_GROUPS, SSM_STATE),
            conv[8 - (CONV_W - 1):])


def _layer_sample(x, cache_k, cache_v, page_table, s_re, s_im, conv_buf, lw):
    S = x.shape[0]
    proj = _inproj_sample(x, lw["g_mix_pre"], lw["w_in"])
    q = proj[:, :ATTN_WIDTH]
    k = proj[:, ATTN_WIDTH:2 * ATTN_WIDTH]
    v = proj[:, 2 * ATTN_WIDTH:3 * ATTN_WIDTH]
    u = proj[:, 3 * ATTN_WIDTH:]
    ck = jnp.transpose(cache_k, (0, 2, 3, 1))
    cv = jnp.transpose(cache_v, (0, 2, 3, 1))
    top = _top_sample(_gate_paged(q, ck, page_table))
    attn = _attn_sample(q, k, v, ck, cv, page_table, top, lw["slopes_page"])
    y, n_re, n_im = _ssm_sample(u, s_re, s_im, lw["ssm_tb"], lw["c_re"], lw["c_im"], lw["d"])
    out, g = _ffn_sample(x, attn.astype(BF16), y, conv_buf[:, 0], conv_buf[:, 1], lw["ffn"])
    conv_new = jnp.stack([conv_buf[:, 1], g], axis=1)
    return (out, k, v, n_re.reshape(S, N_SSM_GROUPS, SSM_STATE), n_im.reshape(S, N_SSM_GROUPS, SSM_STATE),
            conv_new)


def _alibi_key_table(slopes):
    off = jnp.arange(MOBA_BLOCK, dtype=F32)[None, :] * (slopes * LOG2E)[:, None]
    to_bf16 = lambda a: lax.reduce_precision(a, exponent_bits=8, mantissa_bits=7)
    t0 = to_bf16(off)
    t1 = to_bf16(off - t0)
    t2 = to_bf16(off - t0 - t1)
    terms = jnp.stack([t0, t1, t2], axis=-1).astype(BF16)
    half = jnp.pad(terms, ((0, 0), (0, 0), (0, HEAD_DIM - 3)))
    zero = jnp.zeros_like(half)
    odd = (jnp.arange(N_HEADS) % 2 == 1)[:, None, None]
    return jnp.where(odd, jnp.concatenate([half, zero], axis=-1), jnp.concatenate([zero, half], axis=-1))


def kernel(x_prompt, x_sample, cache_k, cache_v, page_table, state_ssm_re, state_ssm_im, state_conv,
           norm_mix_pre, norm_mix_post, w_in, ssm_a_re, ssm_a_im, ssm_log_step, ssm_b_re, ssm_b_im,
           ssm_c_re, ssm_c_im, ssm_d, w_glu, b_glu, w_out, norm_ffn_pre, norm_ffn_post,
           w_gate, w_up, conv_w, conv_b, w_down):
    depth = w_in.shape[0]
    bp, lp_len = x_prompt.shape[:2]
    bs, ls_len = x_sample.shape[:2]
    page = cache_k.shape[2]
    assert bp == 1 and ls_len == 1 and lp_len % ATTN_TQ == 0 and page == LANES
    assert page_table.shape[1] % PAGES_PER_STEP == 0
    slopes = jnp.exp2(-8.0 * jnp.arange(1, N_HEADS + 1, dtype=F32) / N_HEADS)
    hp = x_prompt[0]
    hs = x_sample[:, 0]
    outs = [[] for _ in range(10)]
    for l in range(depth):
        tb = _ssm_tables(ssm_a_re[l], ssm_a_im[l], ssm_log_step[l], ssm_b_re[l], ssm_b_im[l])
        row = lambda a: a[l].reshape(1, -1)
        lw = dict(
            g_mix_pre=row(norm_mix_pre), w_in=w_in[l].astype(BF16),
            k_aug=_alibi_key_table(slopes),
            slopes=slopes,
            slopes_q=jnp.broadcast_to((slopes * LOG2E)[:, None, None], (N_HEADS, 1, MOBA_BLOCK)),
            slopes_page=jnp.broadcast_to(slopes[:, None], (N_HEADS, page)),
            ssm_tb=tb, c_re=ssm_c_re[l], c_im=ssm_c_im[l], d=ssm_d[l],
            ssm_chunk=_ssm_chunk_tables(tb, ssm_c_re[l], ssm_c_im[l], ssm_d[l]),
            ffn=[w_glu[l].astype(BF16), row(b_glu), w_out[l, :ATTN_WIDTH].astype(BF16),
                 w_out[l, ATTN_WIDTH:].astype(BF16), row(norm_mix_post), row(norm_ffn_pre),
                 w_gate[l].astype(BF16), w_up[l].astype(BF16), conv_w[l], row(conv_b),
                 w_down[l].astype(BF16), row(norm_ffn_post)],
        )
        hp, ktp, vtp, sr, si, cp = _layer_prompt(hp, lw)
        outs[0].append(jnp.transpose(ktp, (2, 0, 1))[None])
        outs[1].append(jnp.transpose(vtp, (2, 0, 1))[None])
        outs[4].append(sr[None])
        outs[5].append(si[None])
        outs[8].append(cp[None])
        hs, ks, vs, sr, si, cs = _layer_sample(hs, cache_k[l], cache_v[l], page_table, state_ssm_re[l],
                                               state_ssm_im[l], state_conv[l], lw)
        outs[2].append(ks.reshape(bs, ls_len, N_HEADS, HEAD_DIM))
        outs[3].append(vs.reshape(bs, ls_len, N_HEADS, HEAD_DIM))
        outs[6].append(sr)
        outs[7].append(si)
        outs[9].append(cs)
    return (hp[None], hs[:, None], *[jnp.stack(o) for o in outs])
```

```python
import functools
import math

import jax
import jax.numpy as jnp
from jax import lax
from jax.experimental import pallas as pl
from jax.experimental.pallas import tpu as pltpu

F32 = jnp.float32
BF16 = jnp.bfloat16

D_MODEL = 1024
N_HEADS = 8
HEAD_DIM = 64
ATTN_WIDTH = N_HEADS * HEAD_DIM
SSM_WIDTH = D_MODEL - ATTN_WIDTH
MOBA_BLOCK = 256
MOBA_TOPK = 3
SSM_GROUP = 16
N_SSM_GROUPS = SSM_WIDTH // SSM_GROUP
SSM_STATE = 64
SSM_CHUNK = 16
CONV_W = 3
RMS_EPS = 1e-6
NEG = -1e30
LOG2E = 1.4426950408889634
LANES = 128
V7X_VMEM_LIMIT = 56 * 1024 * 1024
HI = lax.Precision.HIGHEST

ROW_TILE = 512
V_ROWS = 80
SLABS = SSM_WIDTH // LANES
GROUPS_PER_SLAB = LANES // SSM_GROUP


def _cparams(n_axes, vmem=None):
    return pltpu.CompilerParams(dimension_semantics=("arbitrary",) * n_axes, vmem_limit_bytes=vmem)


def _rms(x, g):
    return x * lax.rsqrt(jnp.mean(x * x, axis=-1, keepdims=True) + RMS_EPS) * g


def _gelu_tanh(x):
    return 0.5 * x * (1.0 + jnp.tanh(math.sqrt(2.0 / math.pi) * (x + 0.044715 * (x * x * x))))


def _sigmoid(x):
    return 1.0 / (1.0 + jnp.exp(-x))


def _split_bf16(a):
    hi = a.astype(BF16)
    lo = (a - hi.astype(F32)).astype(BF16)
    return hi, lo


def _round_bf16(a):
    return a.astype(BF16).astype(F32)


def _dot(a, b):
    return jnp.dot(a, b, preferred_element_type=F32)


def _dot_nt(a, b):
    return lax.dot_general(a, b, (((1,), (1,)), ((), ())), preferred_element_type=F32)


def _top_blocks(cur, blk, n, axis=0):
    picks = []
    for _ in range(MOBA_TOPK):
        mx = jnp.max(cur, axis=axis, keepdims=True)
        first = jnp.min(jnp.where(cur == mx, blk, float(n)), axis=axis, keepdims=True)
        picks.append((first, mx))
        cur = jnp.where(blk == first, NEG, cur)
    return picks


def _inproj_prompt_kernel(x_ref, g_ref, w_ref, aug_ref, kt_ref, vt_ref, u_ref, qa_ref, ka_ref, va_ref, sel_ref,
                          qn_ref, kn_ref, kmean_s, *, tl, nb):
    i = pl.program_id(0)
    bpt = tl // MOBA_BLOCK
    H, Dh, B = N_HEADS, HEAD_DIM, MOBA_BLOCK

    @pl.when(i == 0)
    def _():
        kmean_s[...] = jnp.zeros_like(kmean_s)
        qn_ref[...] = jnp.zeros_like(qn_ref)
        kn_ref[...] = jnp.zeros_like(kn_ref)

    h = _rms(x_ref[...], g_ref[...]).astype(BF16)
    proj = _dot(h, w_ref[...])
    q = proj[:, :ATTN_WIDTH]
    k = proj[:, ATTN_WIDTH:2 * ATTN_WIDTH]
    v = proj[:, 2 * ATTN_WIDTH:3 * ATTN_WIDTH]
    for s in range(SLABS):
        u_ref[s] = proj[:, 3 * ATTN_WIDTH + s * LANES:3 * ATTN_WIDTH + (s + 1) * LANES]

    kt = k.T
    vt = v.T
    qt = (q * (Dh ** -0.5 * LOG2E)).T.astype(BF16)
    kt_ref[...] = kt.reshape(H, Dh, tl)
    vt_ref[...] = vt.reshape(H, Dh, tl)

    def head_norm2(t):
        t = t.astype(F32)
        n2 = jnp.sum((t * t).reshape(H, Dh, tl), axis=1)
        return jnp.broadcast_to(jnp.max(n2, axis=1, keepdims=True), (H, LANES))

    qn_ref[...] = jnp.maximum(qn_ref[...], head_norm2(qt))
    kn_ref[...] = jnp.maximum(kn_ref[...], head_norm2(kt))

    sub = lax.broadcasted_iota(jnp.int32, (Dh, tl), 0)
    ones3 = jnp.where(sub < 3, 1.0, 0.0).astype(BF16)
    kb = k.astype(BF16)
    lane_hi = lax.broadcasted_iota(jnp.int32, (B, LANES), 1) >= Dh
    vtb = vt.astype(BF16)
    ones_rows = jnp.ones((V_ROWS - Dh, B), BF16)
    for hd in range(H):
        qh = qt[hd * Dh:(hd + 1) * Dh, :]
        odd = hd % 2 == 1
        qa_ref[hd] = jnp.concatenate([ones3, qh] if odd else [qh, ones3], axis=0)
        for b in range(bpt):
            slab = kb[b * B:(b + 1) * B, (hd // 2) * LANES:(hd // 2 + 1) * LANES]
            ka_ref[hd, b] = jnp.where(lane_hi == odd, slab, aug_ref[hd])
            va_ref[hd, b] = jnp.concatenate([vtb[hd * Dh:(hd + 1) * Dh, b * B:(b + 1) * B], ones_rows], axis=0)

    row = lax.broadcasted_iota(jnp.int32, kmean_s.shape, 0)
    km = kmean_s[...]
    for b in range(bpt):
        kmb = jnp.mean(k[b * B:(b + 1) * B, :], axis=0, keepdims=True)
        km = jnp.where(row == i * bpt + b, kmb, km)
    kmean_s[...] = km

    blk = lax.broadcasted_iota(jnp.int32, (nb, tl), 0).astype(F32)
    own = ((i * tl + lax.broadcasted_iota(jnp.int32, (1, tl), 1)) // B).astype(F32)
    kmb16 = km.astype(BF16)
    qb16 = q.astype(BF16)
    for hd in range(H):
        sl = slice(hd * Dh, (hd + 1) * Dh)
        gate = _dot_nt(kmb16[:, sl], qb16[:, sl])
        chosen = jnp.zeros((nb, tl), F32)
        for first, mx in _top_blocks(jnp.where(blk < own, gate, NEG), blk, nb):
            chosen = jnp.where((blk == first) & (mx > 0.5 * NEG), 1.0, chosen)
        sel_ref[hd] = jnp.where(chosen > 0.5, 0.0, NEG)


def _inproj_prompt(x, g, w_bf, aug):
    L = x.shape[0]
    tl = ROW_TILE
    nb = L // MOBA_BLOCK
    bpt = tl // MOBA_BLOCK
    H, Dh, B = N_HEADS, HEAD_DIM, MOBA_BLOCK
    tcol = pl.BlockSpec((H, Dh, tl), lambda i: (0, 0, i))
    return pl.pallas_call(
        functools.partial(_inproj_prompt_kernel, tl=tl, nb=nb),
        grid=(L // tl,),
        in_specs=[pl.BlockSpec((tl, D_MODEL), lambda i: (i, 0)),
                  pl.BlockSpec((1, D_MODEL), lambda i: (0, 0)),
                  pl.BlockSpec((D_MODEL, 4 * ATTN_WIDTH), lambda i: (0, 0)),
                  pl.BlockSpec((H, B, LANES), lambda i: (0, 0, 0))],
        out_specs=[tcol, tcol,
                   pl.BlockSpec((SLABS, tl, LANES), lambda i: (0, i, 0)),
                   pl.BlockSpec((H, 2 * Dh, tl), lambda i: (0, 0, i)),
                   pl.BlockSpec((H, bpt, B, LANES), lambda i: (0, i, 0, 0)),
                   pl.BlockSpec((H, bpt, V_ROWS, B), lambda i: (0, i, 0, 0)),
                   pl.BlockSpec((H, nb, tl), lambda i: (0, 0, i)),
                   pl.BlockSpec((H, LANES), lambda i: (0, 0)),
                   pl.BlockSpec((H, LANES), lambda i: (0, 0))],
        out_shape=[jax.ShapeDtypeStruct((H, Dh, L), F32), jax.ShapeDtypeStruct((H, Dh, L), F32),
                   jax.ShapeDtypeStruct((SLABS, L, LANES), F32),
                   jax.ShapeDtypeStruct((H, 2 * Dh, L), BF16),
                   jax.ShapeDtypeStruct((H, nb, B, LANES), BF16),
                   jax.ShapeDtypeStruct((H, nb, V_ROWS, B), BF16),
                   jax.ShapeDtypeStruct((H, nb, L), F32),
                   jax.ShapeDtypeStruct((H, LANES), F32), jax.ShapeDtypeStruct((H, LANES), F32)],
        scratch_shapes=[pltpu.VMEM((nb, ATTN_WIDTH), F32)],
        compiler_params=_cparams(1, V7X_VMEM_LIMIT),
        name="inproj_prompt",
    )(x, g, w_bf, aug)


HEADS_PER_STEP = 2
ATTN_TQ = 1024
UNDERFLOW_LOG2 = 160.0


def _block_gates(page_of, n_pages, ppb, qb, ind):
    lane = lax.broadcasted_iota(jnp.int32, (ATTN_WIDTH, LANES), 1)
    prods = jnp.zeros((ATTN_WIDTH, LANES), F32)
    for b in range(n_pages // ppb):
        tot = page_of(b * ppb)
        for r in range(1, ppb):
            tot = tot + page_of(b * ppb + r)
        kmean = jnp.sum(tot, axis=1, keepdims=True) * (1.0 / MOBA_BLOCK)
        prods = jnp.where(lane == b, _round_bf16(kmean) * qb, prods)
    p_hi, p_lo = _split_bf16(prods)
    return _dot(ind, p_hi) + _dot(ind, p_lo)


def _attn_prompt_kernel(w_ref, pt_ref, qa_ref, ka_ref, va_ref, sel_ref, slope_ref, qs_ref, ind_ref, ck_ref,
                        o_ref, gate_ref, s_scr, d_scr, acc_scr, pbuf, psem, *, tq, pps, ppb):
    hp = pl.program_id(0)
    qi = pl.program_id(1)
    step_id = hp * pl.num_programs(1) + qi
    n_steps = pl.num_programs(0) * pl.num_programs(1)
    slot = step_id % 2

    def page_copies(step, slot):
        return [pltpu.make_async_copy(ck_ref.at[pt_ref[step * pps + r]], pbuf.at[slot, r], psem.at[slot])
                for r in range(pps)]

    @pl.when(step_id == 0)
    def _():
        for c in page_copies(step_id, slot):
            c.start()

    @pl.when(step_id + 1 < n_steps)
    def _():
        for c in page_copies(step_id + 1, 1 - slot):
            c.start()

    B, Dh = MOBA_BLOCK, HEAD_DIM
    bpq = tq // B
    units = [(e, cb) for e in range(HEADS_PER_STEP) for cb in range(bpq)]
    n_off = qi * bpq + (bpq - 1)
    keep = w_ref[hp * HEADS_PER_STEP]
    for e in range(1, HEADS_PER_STEP):
        keep = jnp.maximum(keep, w_ref[hp * HEADS_PER_STEP + e])
    j_start = jnp.maximum(qi * bpq - keep, 0)
    lane = lax.broadcasted_iota(jnp.int32, (1, B), 1).astype(F32)
    causal = lax.broadcasted_iota(jnp.int32, (B, B), 0) <= lax.broadcasted_iota(jnp.int32, (B, B), 1)

    def q_of(e, cb):
        return qa_ref[e, :, cb * B:(cb + 1) * B]

    def slope_of(e):
        return slope_ref[e]

    for u, (e, cb) in enumerate(units):
        d_scr[u] = _dot(ka_ref[e, qi * bpq + cb], q_of(e, cb))
        s_scr[u] = _dot(ka_ref[e, j_start], q_of(e, cb))
    ms = []
    for u, (e, cb) in enumerate(units):
        s = jnp.where(causal, d_scr[u] - slope_of(e) * lane, NEG)
        m = jnp.max(s, axis=0, keepdims=True)
        acc_scr[u] = _dot(va_ref[e, qi * bpq + cb], jnp.exp2(s - m).astype(BF16))
        ms.append(m)

    def step(j, ms):
        nxt = jnp.minimum(j + 1, n_off)
        out = []
        for u, (e, cb) in enumerate(units):
            s = s_scr[u]
            dist = lane + ((qi * bpq + cb - j) * B).astype(F32)
            col = sel_ref[e, pl.ds(j, 1), cb * B:(cb + 1) * B] - slope_of(e) * dist
            m_new = jnp.maximum(ms[u], jnp.max(s, axis=0, keepdims=True) + col)
            p = jnp.exp2(s - (m_new - col)).astype(BF16)
            acc_scr[u] = jnp.exp2(ms[u] - m_new) * acc_scr[u] + _dot(va_ref[e, j], p)
            s_scr[u] = _dot(ka_ref[e, nxt], q_of(e, cb))
            out.append(m_new)
        return tuple(out)

    def body(t, ms):
        j = j_start + 2 * t
        return step(j + 1, step(j, ms))

    lax.fori_loop(0, (n_off - j_start + 1) // 2, body, tuple(ms))
    for cb in range(bpq):
        outs = []
        for e in range(HEADS_PER_STEP):
            acc = acc_scr[e * bpq + cb]
            outs.append((acc[:Dh, :] / acc[Dh:Dh + 1, :]).T)
        o_ref[0, cb * B:(cb + 1) * B, :] = jnp.concatenate(outs, axis=1).astype(o_ref.dtype)

    for c in page_copies(step_id, slot):
        c.wait()
    gate_ref[0] = _block_gates(lambda r: pbuf[slot, r].reshape(ATTN_WIDTH, -1), pps, ppb,
                               _round_bf16(qs_ref[0]), ind_ref[...])


def _alibi_keep_blocks(slopes, qn2, kn2, nb):
    qk = jnp.sqrt(qn2 * kn2) * 1.05
    need = (2.0 * qk + UNDERFLOW_LOG2) / (slopes * LOG2E)
    w = jnp.ceil((need - 1.0) / MOBA_BLOCK)
    return jnp.clip(w, 1.0, float(nb)).astype(jnp.int32)


def _attn_prompt(keep, qa, ka, va, sel, slopes, q_sample, ck, page_table):
    H, nb, B, _ = ka.shape
    L = qa.shape[2]
    tq = ATTN_TQ
    hp = HEADS_PER_STEP
    nq = L // tq
    n_steps = (H // hp) * nq
    S, n_pages = page_table.shape
    page = ck.shape[3]
    ppb = MOBA_BLOCK // page
    pps = (S * n_pages) // n_steps
    assert pps * n_steps == S * n_pages and n_pages % pps == 0 and pps % ppb == 0 and pps // ppb <= LANES
    W = H * HEAD_DIM
    ind = (jnp.arange(H)[:, None] == jnp.arange(W)[None, :] // HEAD_DIM).astype(BF16)
    q_rep = jnp.broadcast_to(q_sample.reshape(S, W, 1), (S, W, LANES))
    once = pl.Buffered(1)
    attn, gates = pl.pallas_call(
        functools.partial(_attn_prompt_kernel, tq=tq, pps=pps, ppb=ppb),
        grid_spec=pltpu.PrefetchScalarGridSpec(
            num_scalar_prefetch=2,
            grid=(H // hp, nq),
            in_specs=[pl.BlockSpec((hp, 2 * HEAD_DIM, tq), lambda h, i, w, pt: (h, 0, i)),
                      pl.BlockSpec((hp, nb, B, LANES), lambda h, i, w, pt: (h, 0, 0, 0), pipeline_mode=once),
                      pl.BlockSpec((hp, nb, V_ROWS, B), lambda h, i, w, pt: (h, 0, 0, 0), pipeline_mode=once),
                      pl.BlockSpec((hp, nb, tq), lambda h, i, w, pt: (h, 0, i)),
                      pl.BlockSpec((hp, 1, B), lambda h, i, w, pt: (h, 0, 0)),
                      pl.BlockSpec((1, W, LANES), lambda h, i, w, pt: (((h * nq + i) * pps) // n_pages, 0, 0)),
                      pl.BlockSpec((H, W), lambda h, i, w, pt: (0, 0)),
                      pl.BlockSpec(memory_space=pl.ANY)],
            out_specs=[pl.BlockSpec((1, tq, hp * HEAD_DIM), lambda h, i, w, pt: (h, i, 0)),
                       pl.BlockSpec((1, H, LANES), lambda h, i, w, pt: (h * nq + i, 0, 0))],
            scratch_shapes=[pltpu.VMEM((hp * tq // B, B, B), F32), pltpu.VMEM((hp * tq // B, B, B), F32),
                            pltpu.VMEM((hp * tq // B, V_ROWS, B), F32),
                            pltpu.VMEM((2, pps) + ck.shape[1:], F32), pltpu.SemaphoreType.DMA((2,))],
        ),
        out_shape=[jax.ShapeDtypeStruct((H // hp, L, hp * HEAD_DIM), BF16),
                   jax.ShapeDtypeStruct((n_steps, H, LANES), F32)],
        compiler_params=_cparams(2, V7X_VMEM_LIMIT),
        name="attn_prompt",
    )(keep, page_table.reshape(-1), qa, ka, va, sel, slopes, q_rep, ind, ck)
    bps = pps // ppb
    gates = jnp.transpose(gates[:, :, :bps], (0, 2, 1)).reshape(S, n_pages // ppb, H)
    return attn, gates


def _ssm_tables(a_re, a_im, log_step, b_re, b_im):
    T = SSM_CHUNK
    dt = jnp.exp(log_step)[:, None]
    j = jnp.arange(T + 1, dtype=F32)[:, None, None]
    mag = jnp.exp(a_re * dt * j)
    pw_re = mag * jnp.cos(a_im * dt * j)
    pw_im = mag * jnp.sin(a_im * dt * j)
    abar_re, abar_im = pw_re[1], pw_im[1]
    den = a_re * a_re + a_im * a_im
    nr = abar_re - 1.0
    ni = abar_im
    coef_re = (nr * a_re + ni * a_im) / den
    coef_im = (ni * a_re - nr * a_im) / den
    bb_re = coef_re[..., None] * b_re - coef_im[..., None] * b_im
    bb_im = coef_re[..., None] * b_im + coef_im[..., None] * b_re
    return dict(pw_re=pw_re, pw_im=pw_im, bb_re=bb_re, bb_im=bb_im, abar_re=abar_re, abar_im=abar_im)


def _spread_groups(compact, rows_per_group, cols_per_group, col_outer):
    gs = GROUPS_PER_SLAB
    rows = compact.shape[-2]
    src = jnp.arange(col_outer * cols_per_group)
    dst = jnp.arange(col_outer * gs * cols_per_group)
    same_outer = src[:, None] // cols_per_group == dst[None, :] // (gs * cols_per_group)
    same_c = src[:, None] % cols_per_group == dst[None, :] % cols_per_group
    rep = (same_outer & same_c).astype(BF16)
    row_group = (jnp.arange(rows) // rows_per_group) % gs
    col_group = (dst // cols_per_group) % gs
    wide = jnp.dot(compact, rep, preferred_element_type=F32)
    return jnp.where(row_group[:, None] == col_group[None, :], wide, 0.0).astype(BF16)


def _ssm_chunk_tables(tb, c_re, c_im, d):
    pw_re, pw_im, bb_re, bb_im = tb["pw_re"], tb["pw_im"], tb["bb_re"], tb["bb_im"]
    T = SSM_CHUNK
    G, N, P = bb_re.shape
    x_re = pw_re[:T, :, :, None] * bb_re[None] - pw_im[:T, :, :, None] * bb_im[None]
    x_im = pw_re[:T, :, :, None] * bb_im[None] + pw_im[:T, :, :, None] * bb_re[None]
    kj = (jnp.einsum("gpn,jgnq->jgqp", c_re, x_re, precision=HI)
          - jnp.einsum("gpn,jgnq->jgqp", c_im, x_im, precision=HI))
    kj = kj.at[0].add(jnp.eye(P, dtype=F32)[None] * d[:, :, None])
    gs = GROUPS_PER_SLAB
    kpad = jnp.concatenate([jnp.zeros_like(kj[:1]), kj], axis=0)
    kc = jnp.stack([kpad[:T], kpad[1:]], axis=3).reshape(T, SLABS, gs, P, 2, P)
    kc = jnp.transpose(kc, (1, 0, 2, 3, 4, 5)).reshape(SLABS, T * LANES, 2 * P)
    lag = _spread_groups(kc.astype(BF16), P, P, 2)
    xc = jnp.stack([x_re[::-1], x_im[::-1]], axis=2).reshape(T, SLABS, gs, 2, N, P)
    xc = jnp.transpose(xc, (1, 0, 2, 5, 3, 4)).reshape(SLABS, T * LANES, 2 * N)
    f = _spread_groups(xc.astype(BF16), P, N, 2)
    cr = jnp.transpose(c_re, (0, 2, 1))[None]
    ci = jnp.transpose(c_im, (0, 2, 1))[None]
    ar = pw_re[1:T + 1, :, :, None]
    ai = pw_im[1:T + 1, :, :, None]
    ec = jnp.stack([cr * ar - ci * ai, -(cr * ai + ci * ar)], axis=1)
    ec = ec.reshape(T // 2, 2, 2, SLABS, gs, N, P)
    ec = jnp.transpose(ec, (3, 0, 2, 4, 5, 1, 6)).reshape(SLABS, T // 2, 2 * gs * N, 2 * P)
    e = _spread_groups(ec.astype(BF16), N, P, 2)
    a16_re = pw_re[T].reshape(1, G * N)
    a16_im = pw_im[T].reshape(1, G * N)
    return lag, f, e, a16_re, a16_im


def _chunk_steps(u_ref, rows):
    return [u_ref[0, pl.ds(s, rows, stride=SSM_CHUNK), :].astype(BF16) for s in range(SSM_CHUNK)]


def _ssm_chunk_in_kernel(u_ref, f_ref, bre_ref, bim_ref, *, rows):
    b = _dot(jnp.concatenate(_chunk_steps(u_ref, rows), axis=1), f_ref[0])
    half = b.shape[1] // 2
    bre_ref[...] = b[:, :half]
    bim_ref[...] = b[:, half:]


def _ssm_scan_kernel(bre_ref, bim_ref, are_ref, aim_ref, sre_ref, sim_ref, fre_ref, fim_ref):
    nc = bre_ref.shape[0]
    ar = are_ref[...]
    ai = aim_ref[...]

    def body(c8, carry):
        sr, si = carry
        r0 = pl.multiple_of(c8 * 8, 8)
        br = bre_ref[pl.ds(r0, 8), :]
        bi = bim_ref[pl.ds(r0, 8), :]
        rows_r, rows_i = [], []
        for r in range(8):
            rows_r.append(sr)
            rows_i.append(si)
            sr, si = (ar * sr - ai * si + br[r:r + 1, :], ar * si + ai * sr + bi[r:r + 1, :])
        sre_ref[pl.ds(r0, 8), :] = jnp.concatenate(rows_r, axis=0)
        sim_ref[pl.ds(r0, 8), :] = jnp.concatenate(rows_i, axis=0)
        return sr, si

    z = jnp.zeros(are_ref.shape, F32)
    sr, si = lax.fori_loop(0, nc // 8, body, (z, z))
    fre_ref[...] = sr
    fim_ref[...] = si


def _ssm_chunk_out_kernel(u_ref, lag_ref, e_ref, sre_ref, sim_ref, y_ref, *, rows):
    us = _chunk_steps(u_ref, rows)
    s = jnp.concatenate([sre_ref[...], sim_ref[...]], axis=1).astype(BF16)
    for pair in range(SSM_CHUNK // 2):
        tau = 2 * pair
        lhs = jnp.concatenate(us[tau + 1::-1], axis=1)
        y2 = _dot(lhs, lag_ref[0, :LANES * (tau + 2), :]) + _dot(s, e_ref[0, pair])
        y_ref[0, pl.ds(tau, rows, stride=SSM_CHUNK), :] = y2[:, :LANES]
        y_ref[0, pl.ds(tau + 1, rows, stride=SSM_CHUNK), :] = y2[:, LANES:]


SSM_ROWS = 512


def _ssm_prompt(u4, lag, f, e, a16_re, a16_im):
    L = u4.shape[1]
    nc = L // SSM_CHUNK
    rows = min(SSM_ROWS, nc)
    GN = N_SSM_GROUPS * SSM_STATE
    SW = GROUPS_PER_SLAB * SSM_STATE
    st = jax.ShapeDtypeStruct((nc, GN), F32)
    slab_rows = pl.BlockSpec((1, rows * SSM_CHUNK, LANES), lambda s, r: (s, r, 0))
    state_cols = pl.BlockSpec((rows, SW), lambda s, r: (r, s))
    b_re, b_im = pl.pallas_call(
        functools.partial(_ssm_chunk_in_kernel, rows=rows),
        grid=(SLABS, nc // rows),
        in_specs=[slab_rows, pl.BlockSpec((1,) + f.shape[1:], lambda s, r: (s, 0, 0))],
        out_specs=[state_cols, state_cols],
        out_shape=[st, st],
        compiler_params=_cparams(2, V7X_VMEM_LIMIT),
        name="ssm_chunk_in",
    )(u4, f)
    fin = jax.ShapeDtypeStruct((1, GN), F32)
    s_re, s_im, f_re, f_im = pl.pallas_call(
        _ssm_scan_kernel,
        out_shape=[st, st, fin, fin],
        compiler_params=pltpu.CompilerParams(vmem_limit_bytes=V7X_VMEM_LIMIT),
        name="ssm_scan",
    )(b_re, b_im, a16_re, a16_im)
    y4 = pl.pallas_call(
        functools.partial(_ssm_chunk_out_kernel, rows=rows),
        grid=(SLABS, nc // rows),
        in_specs=[slab_rows,
                  pl.BlockSpec((1,) + lag.shape[1:], lambda s, r: (s, 0, 0)),
                  pl.BlockSpec((1,) + e.shape[1:], lambda s, r: (s, 0, 0, 0)),
                  state_cols, state_cols],
        out_specs=slab_rows,
        out_shape=jax.ShapeDtypeStruct(u4.shape, F32),
        compiler_params=_cparams(2, V7X_VMEM_LIMIT),
        name="ssm_chunk_out",
    )(u4, lag, e, s_re, s_im)
    return y4, f_re, f_im


FF_CHUNK = 256


def _mix_and_prenorm(x, attn_bf, y, wglu_ref, bglu_ref, wouta_ref, wouts_ref, gpost_ref, gpre_ref):
    z = _gelu_tanh(y)
    ssm = z * _sigmoid(_dot(z.astype(BF16), wglu_ref[...]) + bglu_ref[...])
    mix = _dot(attn_bf, wouta_ref[...]) + _dot(ssm.astype(BF16), wouts_ref[...])
    x1 = x + _rms(mix, gpost_ref[...])
    h2 = _rms(x1, gpre_ref[...]).astype(BF16)
    return x1, h2


def _ffn_prompt_kernel(x_ref, attn_ref, y_ref, wglu_ref, bglu_ref, wouta_ref, wouts_ref, gpost_ref, gpre_ref,
                       wgate_ref, wup_ref, cw_ref, cb_ref, wdown_ref, gfpost_ref, out_ref, conv_ref, tail_s, act_s,
                       *, tl, dff):
    i = pl.program_id(0)

    @pl.when(i == 0)
    def _():
        tail_s[...] = jnp.zeros_like(tail_s)

    attn = jnp.concatenate([attn_ref[s] for s in range(attn_ref.shape[0])], axis=1)
    y = jnp.concatenate([y_ref[s] for s in range(SLABS)], axis=1)
    x1, h2 = _mix_and_prenorm(x_ref[...], attn, y, wglu_ref, bglu_ref, wouta_ref, wouts_ref, gpost_ref, gpre_ref)
    row = lax.broadcasted_iota(jnp.int32, (tl, FF_CHUNK), 0)
    for c in range(dff // FF_CHUNK):
        cs = slice(c * FF_CHUNK, (c + 1) * FF_CHUNK)
        g = _dot(h2, wgate_ref[:, cs])
        up = _dot(h2, wup_ref[:, cs])
        tail = tail_s[c]
        p1 = tail[7:8, :]
        p2 = tail[6:7, :]
        g1 = jnp.where(row == 0, p1, pltpu.roll(g, 1, 0))
        g2 = jnp.where(row == 0, p2, jnp.where(row == 1, p1, pltpu.roll(g, 2, 0)))
        gc = cw_ref[0:1, cs] * g2 + cw_ref[1:2, cs] * g1 + cw_ref[2:3, cs] * g + cb_ref[:, cs]
        act_s[:, cs] = (_gelu_tanh(gc) * up).astype(BF16)
        tail_s[c] = g[tl - 8:, :]
        conv_ref[:, cs] = g[tl - 8:, :]
    f = _dot(act_s[...], wdown_ref[...])
    out_ref[...] = x1 + _rms(f, gfpost_ref[...])


def _ffn_sample_kernel(x_ref, attn_ref, y_ref, b0_ref, b1_ref, wglu_ref, bglu_ref, wouta_ref, wouts_ref,
                       gpost_ref, gpre_ref, wgate_ref, wup_ref, cw_ref, cb_ref, wdown_ref, gfpost_ref,
                       out_ref, g_ref, *, dff):
    x1, h2 = _mix_and_prenorm(x_ref[...], attn_ref[...], y_ref[...], wglu_ref, bglu_ref, wouta_ref,
                              wouts_ref, gpost_ref, gpre_ref)
    f = jnp.zeros(x1.shape, F32)
    for c in range(dff // FF_CHUNK):
        cs = slice(c * FF_CHUNK, (c + 1) * FF_CHUNK)
        g = _dot(h2, wgate_ref[:, cs])
        up = _dot(h2, wup_ref[:, cs])
        gc = (cw_ref[0:1, cs] * b0_ref[:, cs] + cw_ref[1:2, cs] * b1_ref[:, cs] + cw_ref[2:3, cs] * g
              + cb_ref[:, cs])
        act = (_gelu_tanh(gc) * up).astype(BF16)
        f = f + _dot(act, wdown_ref[cs, :])
        g_ref[:, cs] = g
    out_ref[...] = x1 + _rms(f, gfpost_ref[...])


def _weight_specs(dff):
    c2 = lambda *_: (0, 0)
    full = lambda r, c: pl.BlockSpec((r, c), c2, pipeline_mode=pl.Buffered(1))
    return [full(SSM_WIDTH, SSM_WIDTH), full(1, SSM_WIDTH), full(ATTN_WIDTH, D_MODEL), full(SSM_WIDTH, D_MODEL),
            full(1, D_MODEL), full(1, D_MODEL), full(D_MODEL, dff), full(D_MODEL, dff), full(CONV_W, dff),
            full(1, dff), full(dff, D_MODEL), full(1, D_MODEL)]


def _ffn_prompt(x, attn2, y4, weights):
    L = x.shape[0]
    tl = ROW_TILE
    dff = weights[6].shape[1]
    rows = lambda w: pl.BlockSpec((tl, w), lambda i: (i, 0))
    slabs = lambda a: pl.BlockSpec((a.shape[0], tl, LANES), lambda i: (0, i, 0))
    return pl.pallas_call(
        functools.partial(_ffn_prompt_kernel, tl=tl, dff=dff),
        grid=(L // tl,),
        in_specs=[rows(D_MODEL), slabs(attn2), slabs(y4)] + _weight_specs(dff),
        out_specs=[rows(D_MODEL), pl.BlockSpec((8, dff), lambda i: (0, 0))],
        out_shape=[jax.ShapeDtypeStruct((L, D_MODEL), F32), jax.ShapeDtypeStruct((8, dff), F32)],
        scratch_shapes=[pltpu.VMEM((dff // FF_CHUNK, 8, FF_CHUNK), F32), pltpu.VMEM((tl, dff), BF16)],
        compiler_params=_cparams(1, V7X_VMEM_LIMIT),
        name="ffn_prompt",
    )(x, attn2, y4, *weights)


def _ffn_sample(x, attn_bf, y, buf0, buf1, weights):
    nb = x.shape[0]
    dff = weights[6].shape[1]
    rows = lambda w: pl.BlockSpec((nb, w), lambda i: (0, 0))
    return pl.pallas_call(
        functools.partial(_ffn_sample_kernel, dff=dff),
        grid=(1,),
        in_specs=[rows(D_MODEL), rows(ATTN_WIDTH), rows(SSM_WIDTH), rows(dff), rows(dff)] + _weight_specs(dff),
        out_specs=[rows(D_MODEL), rows(dff)],
        out_shape=[jax.ShapeDtypeStruct((nb, D_MODEL), F32), jax.ShapeDtypeStruct((nb, dff), F32)],
        compiler_params=_cparams(1, V7X_VMEM_LIMIT),
        name="ffn_sample",
    )(x, attn_bf, y, buf0, buf1, *weights)


def _inproj_sample_kernel(x_ref, g_ref, w_ref, o_ref):
    o_ref[...] = _dot(_rms(x_ref[...], g_ref[...]).astype(BF16), w_ref[...])


def _inproj_sample(x, g, w_bf):
    nb = x.shape[0]
    return pl.pallas_call(
        _inproj_sample_kernel,
        out_shape=jax.ShapeDtypeStruct((nb, w_bf.shape[1]), F32),
        compiler_params=pltpu.CompilerParams(vmem_limit_bytes=V7X_VMEM_LIMIT),
        name="inproj_sample",
    )(x, g, w_bf)


def _top_sample_kernel(g_ref, top_ref):
    gate = g_ref[...]
    nb = gate.shape[1]
    blk = lax.broadcasted_iota(jnp.int32, gate.shape, 1).astype(F32)
    picks = [first for first, _ in _top_blocks(gate, blk, nb, axis=1)]
    top_ref[...] = jnp.concatenate(picks, axis=1).astype(jnp.int32)


def _top_sample(gates):
    S, nb, H = gates.shape
    return pl.pallas_call(
        _top_sample_kernel,
        out_shape=jax.ShapeDtypeStruct((S, MOBA_TOPK, H), jnp.int32),
        name="top_sample",
    )(gates)


def _attn_sample_kernel(pt_ref, top_ref, q_ref, kn_ref, vn_ref, slope_ref, ck_ref, cv_ref, o_ref,
                        kbuf, vbuf, sems, *, n_sel, n_pages, page, past_len):
    H = N_HEADS
    s_i = pl.program_id(0)
    n_seq = pl.num_programs(0)
    scale = HEAD_DIM ** -0.5
    ppb = MOBA_BLOCK // page
    off = lax.broadcasted_iota(jnp.int32, (1, page), 1)

    def block_of(seq, h, r):
        return top_ref[(seq * MOBA_TOPK + r // ppb) * H + h]

    def copies(seq, slot):
        out = []
        for h in range(H):
            for r in range(n_sel):
                pg = pt_ref[seq * n_pages + block_of(seq, h, r) * ppb + r % ppb]
                out.append(pltpu.make_async_copy(ck_ref.at[pg, h], kbuf.at[slot, h * n_sel + r], sems.at[0, slot]))
                out.append(pltpu.make_async_copy(cv_ref.at[pg, h], vbuf.at[slot, h * n_sel + r], sems.at[1, slot]))
        return out

    slot = s_i % 2

    @pl.when(s_i == 0)
    def _():
        for c in copies(s_i, slot):
            c.start()

    @pl.when(s_i + 1 < n_seq)
    def _():
        for c in copies(s_i + 1, 1 - slot):
            c.start()

    for c in copies(s_i, slot):
        c.wait()

    qbs = [_round_bf16(q_ref[0, h]) for h in range(H)]
    scores = []
    for r in range(n_sel):
        qk_rows, dist_rows = [], []
        for h in range(H):
            blk = block_of(s_i, h, r)
            dist_rows.append((past_len - (blk * MOBA_BLOCK + (r % ppb) * page + off)).astype(F32))
            qk_rows.append(jnp.sum(_round_bf16(kbuf[slot, h * n_sel + r]) * qbs[h], axis=0, keepdims=True))
        scores.append(jnp.concatenate(qk_rows, axis=0) * scale
                      - slope_ref[...] * jnp.concatenate(dist_rows, axis=0))
    s_own = jnp.concatenate([jnp.sum(qbs[h] * _round_bf16(kn_ref[0, h]), axis=0, keepdims=True)
                             for h in range(H)], axis=0) * scale
    m = s_own
    for s in scores:
        m = jnp.maximum(m, jnp.max(s, axis=1, keepdims=True))
    p_own = jnp.exp(s_own - m)
    ps = [jnp.exp(s - m) for s in scores]
    l = p_own
    for p in ps:
        l = l + jnp.sum(p, axis=1, keepdims=True)
    inv = 1.0 / l
    pn = [_round_bf16(p * inv) for p in ps]
    pn_own = _round_bf16(p_own * inv)
    for h in range(H):
        acc = pn[0][h:h + 1, :] * _round_bf16(vbuf[slot, h * n_sel])
        for r in range(1, n_sel):
            acc = acc + pn[r][h:h + 1, :] * _round_bf16(vbuf[slot, h * n_sel + r])
        o_ref[0, h] = jnp.sum(acc, axis=1, keepdims=True) + pn_own[h:h + 1, :] * _round_bf16(vn_ref[0, h])


def _attn_sample(q, k_new, v_new, ck, cv, page_table, top, slopes_page):
    n_pool, H, Dh, page = ck.shape
    S, n_pages = page_table.shape
    ppb = MOBA_BLOCK // page
    n_sel = MOBA_TOPK * ppb
    past_len = n_pages * page

    col = pl.BlockSpec((1, H, Dh, 1), lambda s, pt, tp: (s, 0, 0, 0))
    cols = lambda a: a.reshape(S, H, Dh, 1)
    hbm = pl.BlockSpec(memory_space=pl.ANY)
    tiles = pltpu.VMEM((2, H * n_sel, Dh, page), F32)
    out = pl.pallas_call(
        functools.partial(_attn_sample_kernel, n_sel=n_sel, n_pages=n_pages, page=page, past_len=past_len),
        grid_spec=pltpu.PrefetchScalarGridSpec(
            num_scalar_prefetch=2,
            grid=(S,),
            in_specs=[col, col, col, pl.BlockSpec((H, page), lambda s, pt, tp: (0, 0)), hbm, hbm],
            out_specs=col,
            scratch_shapes=[tiles, tiles, pltpu.SemaphoreType.DMA((2, 2))],
        ),
        out_shape=jax.ShapeDtypeStruct((S, H, Dh, 1), F32),
        compiler_params=_cparams(1),
        name="attn_sample",
    )(page_table.reshape(-1), top.reshape(-1), cols(q), cols(k_new), cols(v_new), slopes_page, ck, cv)
    return out.reshape(S, H * Dh)


def _ssm_sample_kernel(u_ref, sre_ref, sim_ref, are_ref, aim_ref, bbre_ref, bbim_ref, cre_ref, cim_ref, d_ref,
                       y_ref, nre_ref, nim_ref):
    u = u_ref[...]
    ub = u.astype(BF16)
    ar, ai = are_ref[...], aim_ref[...]
    s0r, s0i = sre_ref[...], sim_ref[...]
    nr = ar * s0r - ai * s0i + _dot(ub, bbre_ref[...])
    ni = ar * s0i + ai * s0r + _dot(ub, bbim_ref[...])
    nre_ref[...] = nr
    nim_ref[...] = ni
    y_ref[...] = (_dot(nr.astype(BF16), cre_ref[...]) - _dot(ni.astype(BF16), cim_ref[...])
                  + d_ref[...] * u)


def _block_diag(x):
    G, r, c = x.shape
    eye = jnp.eye(G, dtype=x.dtype)
    return (x[:, :, None, :] * eye[:, None, :, None]).reshape(G * r, G * c)


def _ssm_sample(u, s_re, s_im, tb, c_re, c_im, d):
    S = u.shape[0]
    GN = N_SSM_GROUPS * SSM_STATE
    bb_re = _block_diag(jnp.transpose(tb["bb_re"], (0, 2, 1))).astype(BF16)
    bb_im = _block_diag(jnp.transpose(tb["bb_im"], (0, 2, 1))).astype(BF16)
    cc_re = _block_diag(jnp.transpose(c_re, (0, 2, 1))).astype(BF16)
    cc_im = _block_diag(jnp.transpose(c_im, (0, 2, 1))).astype(BF16)
    st = jax.ShapeDtypeStruct((S, GN), F32)
    return pl.pallas_call(
        _ssm_sample_kernel,
        out_shape=[jax.ShapeDtypeStruct((S, SSM_WIDTH), F32), st, st],
        compiler_params=pltpu.CompilerParams(vmem_limit_bytes=V7X_VMEM_LIMIT),
        name="ssm_sample",
    )(u, s_re.reshape(S, GN), s_im.reshape(S, GN), tb["abar_re"].reshape(1, GN), tb["abar_im"].reshape(1, GN),
      bb_re, bb_im, cc_re, cc_im, d.reshape(1, SSM_WIDTH))


def _layer(x, xs, cache_k, cache_v, page_table, s_re, s_im, conv_buf, lw):
    S = xs.shape[0]
    proj = _inproj_sample(xs, lw["g_mix_pre"], lw["w_in"])
    q = proj[:, :ATTN_WIDTH]
    k = proj[:, ATTN_WIDTH:2 * ATTN_WIDTH]
    v = proj[:, 2 * ATTN_WIDTH:3 * ATTN_WIDTH]
    u = proj[:, 3 * ATTN_WIDTH:]
    ck = jnp.transpose(cache_k, (0, 2, 3, 1))
    cv = jnp.transpose(cache_v, (0, 2, 3, 1))

    kt, vt, u4, qa, ka, va, sel, qn2, kn2 = _inproj_prompt(x, lw["g_mix_pre"], lw["w_in"], lw["k_aug"])
    keep = _alibi_keep_blocks(lw["slopes"], qn2[:, 0], kn2[:, 0], ka.shape[1])
    attn2, gates = _attn_prompt(keep, qa, ka, va, sel, lw["slopes_q"], q, ck, page_table)
    y4, f_re, f_im = _ssm_prompt(u4, *lw["ssm_chunk"])
    out, conv = _ffn_prompt(x, attn2, y4, lw["ffn"])
    prompt = (out, kt, vt, f_re.reshape(N_SSM_GROUPS, SSM_STATE), f_im.reshape(N_SSM_GROUPS, SSM_STATE),
              conv[8 - (CONV_W - 1):])

    attn = _attn_sample(q, k, v, ck, cv, page_table, _top_sample(gates), lw["slopes_page"])
    y, n_re, n_im = _ssm_sample(u, s_re, s_im, lw["ssm_tb"], lw["c_re"], lw["c_im"], lw["d"])
    outs, g = _ffn_sample(xs, attn.astype(BF16), y, conv_buf[:, 0], conv_buf[:, 1], lw["ffn"])
    conv_new = jnp.stack([conv_buf[:, 1], g], axis=1)
    sample = (outs, k, v, n_re.reshape(S, N_SSM_GROUPS, SSM_STATE), n_im.reshape(S, N_SSM_GROUPS, SSM_STATE),
              conv_new)
    return prompt, sample


def _alibi_key_table(slopes):
    off = jnp.arange(MOBA_BLOCK, dtype=F32)[None, :] * (slopes * LOG2E)[:, None]
    to_bf16 = lambda a: lax.reduce_precision(a, exponent_bits=8, mantissa_bits=7)
    t0 = to_bf16(off)
    t1 = to_bf16(off - t0)
    t2 = to_bf16(off - t0 - t1)
    terms = jnp.stack([t0, t1, t2], axis=-1).astype(BF16)
    half = jnp.pad(terms, ((0, 0), (0, 0), (0, HEAD_DIM - 3)))
    zero = jnp.zeros_like(half)
    odd = (jnp.arange(N_HEADS) % 2 == 1)[:, None, None]
    return jnp.where(odd, jnp.concatenate([half, zero], axis=-1), jnp.concatenate([zero, half], axis=-1))


def kernel(x_prompt, x_sample, cache_k, cache_v, page_table, state_ssm_re, state_ssm_im, state_conv,
           norm_mix_pre, norm_mix_post, w_in, ssm_a_re, ssm_a_im, ssm_log_step, ssm_b_re, ssm_b_im,
           ssm_c_re, ssm_c_im, ssm_d, w_glu, b_glu, w_out, norm_ffn_pre, norm_ffn_post,
           w_gate, w_up, conv_w, conv_b, w_down):
    depth = w_in.shape[0]
    bp, lp_len = x_prompt.shape[:2]
    bs, ls_len = x_sample.shape[:2]
    page = cache_k.shape[2]
    assert bp == 1 and ls_len == 1 and lp_len % ATTN_TQ == 0 and page == LANES
    slopes = jnp.exp2(-8.0 * jnp.arange(1, N_HEADS + 1, dtype=F32) / N_HEADS)
    hp = x_prompt[0]
    hs = x_sample[:, 0]
    outs = [[] for _ in range(10)]
    for l in range(depth):
        tb = _ssm_tables(ssm_a_re[l], ssm_a_im[l], ssm_log_step[l], ssm_b_re[l], ssm_b_im[l])
        row = lambda a: a[l].reshape(1, -1)
        lw = dict(
            g_mix_pre=row(norm_mix_pre), w_in=w_in[l].astype(BF16),
            k_aug=_alibi_key_table(slopes),
            slopes=slopes,
            slopes_q=jnp.broadcast_to((slopes * LOG2E)[:, None, None], (N_HEADS, 1, MOBA_BLOCK)),
            slopes_page=jnp.broadcast_to(slopes[:, None], (N_HEADS, page)),
            ssm_tb=tb, c_re=ssm_c_re[l], c_im=ssm_c_im[l], d=ssm_d[l],
            ssm_chunk=_ssm_chunk_tables(tb, ssm_c_re[l], ssm_c_im[l], ssm_d[l]),
            ffn=[w_glu[l].astype(BF16), row(b_glu), w_out[l, :ATTN_WIDTH].astype(BF16),
                 w_out[l, ATTN_WIDTH:].astype(BF16), row(norm_mix_post), row(norm_ffn_pre),
                 w_gate[l].astype(BF16), w_up[l].astype(BF16), conv_w[l], row(conv_b),
                 w_down[l].astype(BF16), row(norm_ffn_post)],
        )
        (hp, ktp, vtp, sr, si, cp), (hs, ks, vs, srs, sis, cs) = _layer(
            hp, hs, cache_k[l], cache_v[l], page_table, state_ssm_re[l], state_ssm_im[l], state_conv[l], lw)
        outs[0].append(jnp.transpose(ktp, (2, 0, 1))[None])
        outs[1].append(jnp.transpose(vtp, (2, 0, 1))[None])
        outs[4].append(sr[None])
        outs[5].append(si[None])
        outs[8].append(cp[None])
        sr, si = srs, sis
        outs[2].append(ks.reshape(bs, ls_len, N_HEADS, HEAD_DIM))
        outs[3].append(vs.reshape(bs, ls_len, N_HEADS, HEAD_DIM))
        outs[6].append(sr)
        outs[7].append(si)
        outs[9].append(cs)
    return (hp[None], hs[:, None], *[jnp.stack(o) for o in outs])
```

```python
import functools
import math

import jax
import jax.numpy as jnp
from jax import lax
from jax.experimental import pallas as pl
from jax.experimental.pallas import tpu as pltpu

F32 = jnp.float32
BF16 = jnp.bfloat16

D_MODEL = 1024
N_HEADS = 8
HEAD_DIM = 64
ATTN_WIDTH = N_HEADS * HEAD_DIM
SSM_WIDTH = D_MODEL - ATTN_WIDTH
MOBA_BLOCK = 256
MOBA_TOPK = 3
SSM_GROUP = 16
N_SSM_GROUPS = SSM_WIDTH // SSM_GROUP
SSM_STATE = 64
SSM_CHUNK = 16
CONV_W = 3
RMS_EPS = 1e-6
NEG = -1e30
LOG2E = 1.4426950408889634
LANES = 128
V7X_VMEM_LIMIT = 56 * 1024 * 1024
HI = lax.Precision.HIGHEST

ROW_TILE = 512
V_ROWS = 80
SLABS = SSM_WIDTH // LANES
GROUPS_PER_SLAB = LANES // SSM_GROUP


def _cparams(n_axes, vmem=None):
    return pltpu.CompilerParams(dimension_semantics=("arbitrary",) * n_axes, vmem_limit_bytes=vmem)


def _rms(x, g):
    return x * lax.rsqrt(jnp.mean(x * x, axis=-1, keepdims=True) + RMS_EPS) * g


def _gelu_tanh(x):
    return 0.5 * x * (1.0 + jnp.tanh(math.sqrt(2.0 / math.pi) * (x + 0.044715 * (x * x * x))))


def _sigmoid(x):
    return 1.0 / (1.0 + jnp.exp(-x))


def _split_bf16(a):
    hi = a.astype(BF16)
    lo = (a - hi.astype(F32)).astype(BF16)
    return hi, lo


def _round_bf16(a):
    return a.astype(BF16).astype(F32)


def _dot(a, b):
    return jnp.dot(a, b, preferred_element_type=F32)


def _dot_nt(a, b):
    return lax.dot_general(a, b, (((1,), (1,)), ((), ())), preferred_element_type=F32)


def _top_blocks(cur, blk, n, axis=0):
    picks = []
    for _ in range(MOBA_TOPK):
        mx = jnp.max(cur, axis=axis, keepdims=True)
        first = jnp.min(jnp.where(cur == mx, blk, float(n)), axis=axis, keepdims=True)
        picks.append((first, mx))
        cur = jnp.where(blk == first, NEG, cur)
    return picks


def _inproj_prompt_kernel(x_ref, g_ref, w_ref, aug_ref, kt_ref, vt_ref, u_ref, qa_ref, ka_ref, va_ref, sel_ref,
                          qn_ref, kn_ref, kmean_s, *, tl, nb):
    i = pl.program_id(0)
    bpt = tl // MOBA_BLOCK
    H, Dh, B = N_HEADS, HEAD_DIM, MOBA_BLOCK

    @pl.when(i == 0)
    def _():
        kmean_s[...] = jnp.zeros_like(kmean_s)
        qn_ref[...] = jnp.zeros_like(qn_ref)
        kn_ref[...] = jnp.zeros_like(kn_ref)

    h = _rms(x_ref[...], g_ref[...]).astype(BF16)
    proj = _dot(h, w_ref[...])
    q = proj[:, :ATTN_WIDTH]
    k = proj[:, ATTN_WIDTH:2 * ATTN_WIDTH]
    v = proj[:, 2 * ATTN_WIDTH:3 * ATTN_WIDTH]
    for s in range(SLABS):
        u_ref[s] = proj[:, 3 * ATTN_WIDTH + s * LANES:3 * ATTN_WIDTH + (s + 1) * LANES]

    kt = k.T
    vt = v.T
    qt = (q * (Dh ** -0.5 * LOG2E)).T.astype(BF16)
    kt_ref[...] = kt.reshape(H, Dh, tl)
    vt_ref[...] = vt.reshape(H, Dh, tl)

    def head_norm2(t):
        t = t.astype(F32)
        n2 = jnp.sum((t * t).reshape(H, Dh, tl), axis=1)
        return jnp.broadcast_to(jnp.max(n2, axis=1, keepdims=True), (H, LANES))

    qn_ref[...] = jnp.maximum(qn_ref[...], head_norm2(qt))
    kn_ref[...] = jnp.maximum(kn_ref[...], head_norm2(kt))

    sub = lax.broadcasted_iota(jnp.int32, (Dh, tl), 0)
    ones3 = jnp.where(sub < 3, 1.0, 0.0).astype(BF16)
    kb = k.astype(BF16)
    lane_hi = lax.broadcasted_iota(jnp.int32, (B, LANES), 1) >= Dh
    vtb = vt.astype(BF16)
    ones_rows = jnp.ones((V_ROWS - Dh, B), BF16)
    for hd in range(H):
        qh = qt[hd * Dh:(hd + 1) * Dh, :]
        odd = hd % 2 == 1
        qa_ref[hd] = jnp.concatenate([ones3, qh] if odd else [qh, ones3], axis=0)
        for b in range(bpt):
            slab = kb[b * B:(b + 1) * B, (hd // 2) * LANES:(hd // 2 + 1) * LANES]
            ka_ref[hd, b] = jnp.where(lane_hi == odd, slab, aug_ref[hd])
            va_ref[hd, b] = jnp.concatenate([vtb[hd * Dh:(hd + 1) * Dh, b * B:(b + 1) * B], ones_rows], axis=0)

    row = lax.broadcasted_iota(jnp.int32, kmean_s.shape, 0)
    km = kmean_s[...]
    for b in range(bpt):
        kmb = jnp.mean(k[b * B:(b + 1) * B, :], axis=0, keepdims=True)
        km = jnp.where(row == i * bpt + b, kmb, km)
    kmean_s[...] = km

    blk = lax.broadcasted_iota(jnp.int32, (nb, tl), 0).astype(F32)
    own = ((i * tl + lax.broadcasted_iota(jnp.int32, (1, tl), 1)) // B).astype(F32)
    kmb16 = km.astype(BF16)
    qb16 = q.astype(BF16)
    for hd in range(H):
        sl = slice(hd * Dh, (hd + 1) * Dh)
        gate = _dot_nt(kmb16[:, sl], qb16[:, sl])
        chosen = jnp.zeros((nb, tl), F32)
        for first, mx in _top_blocks(jnp.where(blk < own, gate, NEG), blk, nb):
            chosen = jnp.where((blk == first) & (mx > 0.5 * NEG), 1.0, chosen)
        sel_ref[hd] = jnp.where(chosen > 0.5, 0.0, NEG)


def _inproj_prompt(x, g, w_bf, aug):
    L = x.shape[0]
    tl = ROW_TILE
    nb = L // MOBA_BLOCK
    bpt = tl // MOBA_BLOCK
    H, Dh, B = N_HEADS, HEAD_DIM, MOBA_BLOCK
    tcol = pl.BlockSpec((H, Dh, tl), lambda i: (0, 0, i))
    return pl.pallas_call(
        functools.partial(_inproj_prompt_kernel, tl=tl, nb=nb),
        grid=(L // tl,),
        in_specs=[pl.BlockSpec((tl, D_MODEL), lambda i: (i, 0)),
                  pl.BlockSpec((1, D_MODEL), lambda i: (0, 0)),
                  pl.BlockSpec((D_MODEL, 4 * ATTN_WIDTH), lambda i: (0, 0)),
                  pl.BlockSpec((H, B, LANES), lambda i: (0, 0, 0))],
        out_specs=[tcol, tcol,
                   pl.BlockSpec((SLABS, tl, LANES), lambda i: (0, i, 0)),
                   pl.BlockSpec((H, 2 * Dh, tl), lambda i: (0, 0, i)),
                   pl.BlockSpec((H, bpt, B, LANES), lambda i: (0, i, 0, 0)),
                   pl.BlockSpec((H, bpt, V_ROWS, B), lambda i: (0, i, 0, 0)),
                   pl.BlockSpec((H, nb, tl), lambda i: (0, 0, i)),
                   pl.BlockSpec((H, LANES), lambda i: (0, 0)),
                   pl.BlockSpec((H, LANES), lambda i: (0, 0))],
        out_shape=[jax.ShapeDtypeStruct((H, Dh, L), F32), jax.ShapeDtypeStruct((H, Dh, L), F32),
                   jax.ShapeDtypeStruct((SLABS, L, LANES), F32),
                   jax.ShapeDtypeStruct((H, 2 * Dh, L), BF16),
                   jax.ShapeDtypeStruct((H, nb, B, LANES), BF16),
                   jax.ShapeDtypeStruct((H, nb, V_ROWS, B), BF16),
                   jax.ShapeDtypeStruct((H, nb, L), F32),
                   jax.ShapeDtypeStruct((H, LANES), F32), jax.ShapeDtypeStruct((H, LANES), F32)],
        scratch_shapes=[pltpu.VMEM((nb, ATTN_WIDTH), F32)],
        compiler_params=_cparams(1, V7X_VMEM_LIMIT),
        name="inproj_prompt",
    )(x, g, w_bf, aug)


HEADS_PER_STEP = 2
ATTN_TQ = 1024
UNDERFLOW_LOG2 = 160.0


def _block_gates(page_of, n_pages, ppb, qb, ind):
    lane = lax.broadcasted_iota(jnp.int32, (ATTN_WIDTH, LANES), 1)
    prods = jnp.zeros((ATTN_WIDTH, LANES), F32)
    for b in range(n_pages // ppb):
        tot = page_of(b * ppb)
        for r in range(1, ppb):
            tot = tot + page_of(b * ppb + r)
        kmean = jnp.sum(tot, axis=1, keepdims=True) * (1.0 / MOBA_BLOCK)
        prods = jnp.where(lane == b, _round_bf16(kmean) * qb, prods)
    p_hi, p_lo = _split_bf16(prods)
    return _dot(ind, p_hi) + _dot(ind, p_lo)


def _attn_prompt_kernel(w_ref, pt_ref, qa_ref, ka_ref, va_ref, sel_ref, slope_ref, qs_ref, ind_ref, ck_ref,
                        o_ref, gate_ref, s_scr, d_scr, acc_scr, pbuf, psem, *, tq, pps, ppb):
    hp = pl.program_id(0)
    qi = pl.program_id(1)
    step_id = hp * pl.num_programs(1) + qi
    n_steps = pl.num_programs(0) * pl.num_programs(1)
    slot = step_id % 2

    def page_copies(step, slot):
        return [pltpu.make_async_copy(ck_ref.at[pt_ref[step * pps + r]], pbuf.at[slot, r], psem.at[slot])
                for r in range(pps)]

    @pl.when(step_id == 0)
    def _():
        for c in page_copies(step_id, slot):
            c.start()

    @pl.when(step_id + 1 < n_steps)
    def _():
        for c in page_copies(step_id + 1, 1 - slot):
            c.start()

    for c in page_copies(step_id, slot):
        c.wait()

    B, Dh = MOBA_BLOCK, HEAD_DIM
    bpq = tq // B
    units = [(e, cb) for e in range(HEADS_PER_STEP) for cb in range(bpq)]
    n_off = qi * bpq + (bpq - 1)
    keep = w_ref[hp * HEADS_PER_STEP]
    for e in range(1, HEADS_PER_STEP):
        keep = jnp.maximum(keep, w_ref[hp * HEADS_PER_STEP + e])
    j_start = jnp.maximum(qi * bpq - keep, 0)
    lane = lax.broadcasted_iota(jnp.int32, (1, B), 1).astype(F32)
    causal = lax.broadcasted_iota(jnp.int32, (B, B), 0) <= lax.broadcasted_iota(jnp.int32, (B, B), 1)

    def q_of(e, cb):
        return qa_ref[e, :, cb * B:(cb + 1) * B]

    def slope_of(e):
        return slope_ref[e]

    for u, (e, cb) in enumerate(units):
        d_scr[u] = _dot(ka_ref[e, qi * bpq + cb], q_of(e, cb))
        s_scr[u] = _dot(ka_ref[e, j_start], q_of(e, cb))

    gate_ref[0] = _block_gates(lambda r: pbuf[slot, r].reshape(ATTN_WIDTH, -1), pps, ppb,
                               _round_bf16(qs_ref[0]), ind_ref[...])

    ms = []
    for u, (e, cb) in enumerate(units):
        s = jnp.where(causal, d_scr[u] - slope_of(e) * lane, NEG)
        m = jnp.max(s, axis=0, keepdims=True)
        acc_scr[u] = _dot(va_ref[e, qi * bpq + cb], jnp.exp2(s - m).astype(BF16))
        ms.append(m)

    def step(j, ms):
        nxt = jnp.minimum(j + 1, n_off)
        out = []
        for u, (e, cb) in enumerate(units):
            s = s_scr[u]
            dist = lane + ((qi * bpq + cb - j) * B).astype(F32)
            col = sel_ref[e, pl.ds(j, 1), cb * B:(cb + 1) * B] - slope_of(e) * dist
            m_new = jnp.maximum(ms[u], jnp.max(s, axis=0, keepdims=True) + col)
            p = jnp.exp2(s - (m_new - col)).astype(BF16)
            acc_scr[u] = jnp.exp2(ms[u] - m_new) * acc_scr[u] + _dot(va_ref[e, j], p)
            s_scr[u] = _dot(ka_ref[e, nxt], q_of(e, cb))
            out.append(m_new)
        return tuple(out)

    def body(t, ms):
        j = j_start + 2 * t
        return step(j + 1, step(j, ms))

    lax.fori_loop(0, (n_off - j_start + 1) // 2, body, tuple(ms))
    for cb in range(bpq):
        outs = []
        for e in range(HEADS_PER_STEP):
            acc = acc_scr[e * bpq + cb]
            outs.append((acc[:Dh, :] / acc[Dh:Dh + 1, :]).T)
        o_ref[0, cb * B:(cb + 1) * B, :] = jnp.concatenate(outs, axis=1).astype(o_ref.dtype)


def _alibi_keep_blocks(slopes, qn2, kn2, nb):
    qk = jnp.sqrt(qn2 * kn2) * 1.05
    need = (2.0 * qk + UNDERFLOW_LOG2) / (slopes * LOG2E)
    w = jnp.ceil((need - 1.0) / MOBA_BLOCK)
    return jnp.clip(w, 1.0, float(nb)).astype(jnp.int32)


def _attn_prompt(keep, qa, ka, va, sel, slopes, q_sample, ck, page_table):
    H, nb, B, _ = ka.shape
    L = qa.shape[2]
    tq = ATTN_TQ
    hp = HEADS_PER_STEP
    nq = L // tq
    n_steps = (H // hp) * nq
    S, n_pages = page_table.shape
    page = ck.shape[3]
    ppb = MOBA_BLOCK // page
    pps = (S * n_pages) // n_steps
    assert pps * n_steps == S * n_pages and n_pages % pps == 0 and pps % ppb == 0 and pps // ppb <= LANES
    W = H * HEAD_DIM
    ind = (jnp.arange(H)[:, None] == jnp.arange(W)[None, :] // HEAD_DIM).astype(BF16)
    q_rep = jnp.broadcast_to(q_sample.reshape(S, W, 1), (S, W, LANES))
    once = pl.Buffered(1)
    attn, gates = pl.pallas_call(
        functools.partial(_attn_prompt_kernel, tq=tq, pps=pps, ppb=ppb),
        grid_spec=pltpu.PrefetchScalarGridSpec(
            num_scalar_prefetch=2,
            grid=(H // hp, nq),
            in_specs=[pl.BlockSpec((hp, 2 * HEAD_DIM, tq), lambda h, i, w, pt: (h, 0, i)),
                      pl.BlockSpec((hp, nb, B, LANES), lambda h, i, w, pt: (h, 0, 0, 0), pipeline_mode=once),
                      pl.BlockSpec((hp, nb, V_ROWS, B), lambda h, i, w, pt: (h, 0, 0, 0), pipeline_mode=once),
                      pl.BlockSpec((hp, nb, tq), lambda h, i, w, pt: (h, 0, i)),
                      pl.BlockSpec((hp, 1, B), lambda h, i, w, pt: (h, 0, 0)),
                      pl.BlockSpec((1, W, LANES), lambda h, i, w, pt: (((h * nq + i) * pps) // n_pages, 0, 0)),
                      pl.BlockSpec((H, W), lambda h, i, w, pt: (0, 0)),
                      pl.BlockSpec(memory_space=pl.ANY)],
            out_specs=[pl.BlockSpec((1, tq, hp * HEAD_DIM), lambda h, i, w, pt: (h, i, 0)),
                       pl.BlockSpec((1, H, LANES), lambda h, i, w, pt: (h * nq + i, 0, 0))],
            scratch_shapes=[pltpu.VMEM((hp * tq // B, B, B), F32), pltpu.VMEM((hp * tq // B, B, B), F32),
                            pltpu.VMEM((hp * tq // B, V_ROWS, B), F32),
                            pltpu.VMEM((2, pps) + ck.shape[1:], F32), pltpu.SemaphoreType.DMA((2,))],
        ),
        out_shape=[jax.ShapeDtypeStruct((H // hp, L, hp * HEAD_DIM), BF16),
                   jax.ShapeDtypeStruct((n_steps, H, LANES), F32)],
        compiler_params=_cparams(2, V7X_VMEM_LIMIT),
        name="attn_prompt",
    )(keep, page_table.reshape(-1), qa, ka, va, sel, slopes, q_rep, ind, ck)
    bps = pps // ppb
    gates = jnp.transpose(gates[:, :, :bps], (0, 2, 1)).reshape(S, n_pages // ppb, H)
    return attn, gates


def _ssm_tables(a_re, a_im, log_step, b_re, b_im):
    T = SSM_CHUNK
    dt = jnp.exp(log_step)[:, None]
    j = jnp.arange(T + 1, dtype=F32)[:, None, None]
    mag = jnp.exp(a_re * dt * j)
    pw_re = mag * jnp.cos(a_im * dt * j)
    pw_im = mag * jnp.sin(a_im * dt * j)
    abar_re, abar_im = pw_re[1], pw_im[1]
    den = a_re * a_re + a_im * a_im
    nr = abar_re - 1.0
    ni = abar_im
    coef_re = (nr * a_re + ni * a_im) / den
    coef_im = (ni * a_re - nr * a_im) / den
    bb_re = coef_re[..., None] * b_re - coef_im[..., None] * b_im
    bb_im = coef_re[..., None] * b_im + coef_im[..., None] * b_re
    return dict(pw_re=pw_re, pw_im=pw_im, bb_re=bb_re, bb_im=bb_im, abar_re=abar_re, abar_im=abar_im)


def _spread_groups(compact, rows_per_group, cols_per_group, col_outer):
    gs = GROUPS_PER_SLAB
    rows = compact.shape[-2]
    src = jnp.arange(col_outer * cols_per_group)
    dst = jnp.arange(col_outer * gs * cols_per_group)
    same_outer = src[:, None] // cols_per_group == dst[None, :] // (gs * cols_per_group)
    same_c = src[:, None] % cols_per_group == dst[None, :] % cols_per_group
    rep = (same_outer & same_c).astype(BF16)
    row_group = (jnp.arange(rows) // rows_per_group) % gs
    col_group = (dst // cols_per_group) % gs
    wide = jnp.dot(compact, rep, preferred_element_type=F32)
    return jnp.where(row_group[:, None] == col_group[None, :], wide, 0.0).astype(BF16)


def _ssm_chunk_tables(tb, c_re, c_im, d):
    pw_re, pw_im, bb_re, bb_im = tb["pw_re"], tb["pw_im"], tb["bb_re"], tb["bb_im"]
    T = SSM_CHUNK
    G, N, P = bb_re.shape
    x_re = pw_re[:T, :, :, None] * bb_re[None] - pw_im[:T, :, :, None] * bb_im[None]
    x_im = pw_re[:T, :, :, None] * bb_im[None] + pw_im[:T, :, :, None] * bb_re[None]
    kj = (jnp.einsum("gpn,jgnq->jgqp", c_re, x_re, precision=HI)
          - jnp.einsum("gpn,jgnq->jgqp", c_im, x_im, precision=HI))
    kj = kj.at[0].add(jnp.eye(P, dtype=F32)[None] * d[:, :, None])
    gs = GROUPS_PER_SLAB
    kpad = jnp.concatenate([jnp.zeros_like(kj[:1]), kj], axis=0)
    kc = jnp.stack([kpad[:T], kpad[1:]], axis=3).reshape(T, SLABS, gs, P, 2, P)
    kc = jnp.transpose(kc, (1, 0, 2, 3, 4, 5)).reshape(SLABS, T * LANES, 2 * P)
    lag = _spread_groups(kc.astype(BF16), P, P, 2)
    xc = jnp.stack([x_re[::-1], x_im[::-1]], axis=2).reshape(T, SLABS, gs, 2, N, P)
    xc = jnp.transpose(xc, (1, 0, 2, 5, 3, 4)).reshape(SLABS, T * LANES, 2 * N)
    f = _spread_groups(xc.astype(BF16), P, N, 2)
    cr = jnp.transpose(c_re, (0, 2, 1))[None]
    ci = jnp.transpose(c_im, (0, 2, 1))[None]
    ar = pw_re[1:T + 1, :, :, None]
    ai = pw_im[1:T + 1, :, :, None]
    ec = jnp.stack([cr * ar - ci * ai, -(cr * ai + ci * ar)], axis=1)
    ec = ec.reshape(T // 2, 2, 2, SLABS, gs, N, P)
    ec = jnp.transpose(ec, (3, 0, 2, 4, 5, 1, 6)).reshape(SLABS, T // 2, 2 * gs * N, 2 * P)
    e = _spread_groups(ec.astype(BF16), N, P, 2)
    a16_re = pw_re[T].reshape(1, G * N)
    a16_im = pw_im[T].reshape(1, G * N)
    return lag, f, e, a16_re, a16_im


def _chunk_steps(u_ref, rows):
    return [u_ref[0, pl.ds(s, rows, stride=SSM_CHUNK), :].astype(BF16) for s in range(SSM_CHUNK)]


def _ssm_chunk_in_kernel(u_ref, f_ref, bre_ref, bim_ref, *, rows):
    b = _dot(jnp.concatenate(_chunk_steps(u_ref, rows), axis=1), f_ref[0])
    half = b.shape[1] // 2
    bre_ref[...] = b[:, :half]
    bim_ref[...] = b[:, half:]


def _ssm_scan_kernel(bre_ref, bim_ref, are_ref, aim_ref, sre_ref, sim_ref, fre_ref, fim_ref):
    nc = bre_ref.shape[0]
    ar = are_ref[...]
    ai = aim_ref[...]

    def body(c8, carry):
        sr, si = carry
        r0 = pl.multiple_of(c8 * 8, 8)
        br = bre_ref[pl.ds(r0, 8), :]
        bi = bim_ref[pl.ds(r0, 8), :]
        rows_r, rows_i = [], []
        for r in range(8):
            rows_r.append(sr)
            rows_i.append(si)
            sr, si = (ar * sr - ai * si + br[r:r + 1, :], ar * si + ai * sr + bi[r:r + 1, :])
        sre_ref[pl.ds(r0, 8), :] = jnp.concatenate(rows_r, axis=0)
        sim_ref[pl.ds(r0, 8), :] = jnp.concatenate(rows_i, axis=0)
        return sr, si

    z = jnp.zeros(are_ref.shape, F32)
    sr, si = lax.fori_loop(0, nc // 8, body, (z, z))
    fre_ref[...] = sr
    fim_ref[...] = si


def _ssm_chunk_out_kernel(u_ref, lag_ref, e_ref, sre_ref, sim_ref, y_ref, *, rows):
    us = _chunk_steps(u_ref, rows)
    s = jnp.concatenate([sre_ref[...], sim_ref[...]], axis=1).astype(BF16)
    for pair in range(SSM_CHUNK // 2):
        tau = 2 * pair
        lhs = jnp.concatenate(us[tau + 1::-1], axis=1)
        y2 = _dot(lhs, lag_ref[0, :LANES * (tau + 2), :]) + _dot(s, e_ref[0, pair])
        y_ref[0, pl.ds(tau, rows, stride=SSM_CHUNK), :] = y2[:, :LANES]
        y_ref[0, pl.ds(tau + 1, rows, stride=SSM_CHUNK), :] = y2[:, LANES:]


SSM_ROWS = 512


def _ssm_prompt(u4, lag, f, e, a16_re, a16_im):
    L = u4.shape[1]
    nc = L // SSM_CHUNK
    rows = min(SSM_ROWS, nc)
    GN = N_SSM_GROUPS * SSM_STATE
    SW = GROUPS_PER_SLAB * SSM_STATE
    st = jax.ShapeDtypeStruct((nc, GN), F32)
    slab_rows = pl.BlockSpec((1, rows * SSM_CHUNK, LANES), lambda s, r: (s, r, 0))
    state_cols = pl.BlockSpec((rows, SW), lambda s, r: (r, s))
    b_re, b_im = pl.pallas_call(
        functools.partial(_ssm_chunk_in_kernel, rows=rows),
        grid=(SLABS, nc // rows),
        in_specs=[slab_rows, pl.BlockSpec((1,) + f.shape[1:], lambda s, r: (s, 0, 0))],
        out_specs=[state_cols, state_cols],
        out_shape=[st, st],
        compiler_params=_cparams(2, V7X_VMEM_LIMIT),
        name="ssm_chunk_in",
    )(u4, f)
    fin = jax.ShapeDtypeStruct((1, GN), F32)
    s_re, s_im, f_re, f_im = pl.pallas_call(
        _ssm_scan_kernel,
        out_shape=[st, st, fin, fin],
        compiler_params=pltpu.CompilerParams(vmem_limit_bytes=V7X_VMEM_LIMIT),
        name="ssm_scan",
    )(b_re, b_im, a16_re, a16_im)
    y4 = pl.pallas_call(
        functools.partial(_ssm_chunk_out_kernel, rows=rows),
        grid=(SLABS, nc // rows),
        in_specs=[slab_rows,
                  pl.BlockSpec((1,) + lag.shape[1:], lambda s, r: (s, 0, 0)),
                  pl.BlockSpec((1,) + e.shape[1:], lambda s, r: (s, 0, 0, 0)),
                  state_cols, state_cols],
        out_specs=slab_rows,
        out_shape=jax.ShapeDtypeStruct(u4.shape, F32),
        compiler_params=_cparams(2, V7X_VMEM_LIMIT),
        name="ssm_chunk_out",
    )(u4, lag, e, s_re, s_im)
    return y4, f_re, f_im


FF_CHUNK = 256


def _mix_and_prenorm(x, attn_bf, y, wglu_ref, bglu_ref, wouta_ref, wouts_ref, gpost_ref, gpre_ref):
    z = _gelu_tanh(y)
    ssm = z * _sigmoid(_dot(z.astype(BF16), wglu_ref[...]) + bglu_ref[...])
    mix = _dot(attn_bf, wouta_ref[...]) + _dot(ssm.astype(BF16), wouts_ref[...])
    x1 = x + _rms(mix, gpost_ref[...])
    h2 = _rms(x1, gpre_ref[...]).astype(BF16)
    return x1, h2


def _ffn_prompt_kernel(x_ref, attn_ref, y_ref, wglu_ref, bglu_ref, wouta_ref, wouts_ref, gpost_ref, gpre_ref,
                       wgate_ref, wup_ref, cw_ref, cb_ref, wdown_ref, gfpost_ref, out_ref, conv_ref, tail_s, act_s,
                       *, tl, dff):
    i = pl.program_id(0)

    @pl.when(i == 0)
    def _():
        tail_s[...] = jnp.zeros_like(tail_s)

    attn = jnp.concatenate([attn_ref[s] for s in range(attn_ref.shape[0])], axis=1)
    y = jnp.concatenate([y_ref[s] for s in range(SLABS)], axis=1)
    x1, h2 = _mix_and_prenorm(x_ref[...], attn, y, wglu_ref, bglu_ref, wouta_ref, wouts_ref, gpost_ref, gpre_ref)
    row = lax.broadcasted_iota(jnp.int32, (tl, FF_CHUNK), 0)
    for c in range(dff // FF_CHUNK):
        cs = slice(c * FF_CHUNK, (c + 1) * FF_CHUNK)
        g = _dot(h2, wgate_ref[:, cs])
        up = _dot(h2, wup_ref[:, cs])
        tail = tail_s[c]
        p1 = tail[7:8, :]
        p2 = tail[6:7, :]
        g1 = jnp.where(row == 0, p1, pltpu.roll(g, 1, 0))
        g2 = jnp.where(row == 0, p2, jnp.where(row == 1, p1, pltpu.roll(g, 2, 0)))
        gc = cw_ref[0:1, cs] * g2 + cw_ref[1:2, cs] * g1 + cw_ref[2:3, cs] * g + cb_ref[:, cs]
        act_s[:, cs] = (_gelu_tanh(gc) * up).astype(BF16)
        tail_s[c] = g[tl - 8:, :]
        conv_ref[:, cs] = g[tl - 8:, :]
    f = _dot(act_s[...], wdown_ref[...])
    out_ref[...] = x1 + _rms(f, gfpost_ref[...])


def _ffn_sample_kernel(x_ref, attn_ref, y_ref, b0_ref, b1_ref, wglu_ref, bglu_ref, wouta_ref, wouts_ref,
                       gpost_ref, gpre_ref, wgate_ref, wup_ref, cw_ref, cb_ref, wdown_ref, gfpost_ref,
                       out_ref, g_ref, *, dff):
    x1, h2 = _mix_and_prenorm(x_ref[...], attn_ref[...], y_ref[...], wglu_ref, bglu_ref, wouta_ref,
                              wouts_ref, gpost_ref, gpre_ref)
    f = jnp.zeros(x1.shape, F32)
    for c in range(dff // FF_CHUNK):
        cs = slice(c * FF_CHUNK, (c + 1) * FF_CHUNK)
        g = _dot(h2, wgate_ref[:, cs])
        up = _dot(h2, wup_ref[:, cs])
        gc = (cw_ref[0:1, cs] * b0_ref[:, cs] + cw_ref[1:2, cs] * b1_ref[:, cs] + cw_ref[2:3, cs] * g
              + cb_ref[:, cs])
        act = (_gelu_tanh(gc) * up).astype(BF16)
        f = f + _dot(act, wdown_ref[cs, :])
        g_ref[:, cs] = g
    out_ref[...] = x1 + _rms(f, gfpost_ref[...])


def _weight_specs(dff):
    c2 = lambda *_: (0, 0)
    full = lambda r, c: pl.BlockSpec((r, c), c2, pipeline_mode=pl.Buffered(1))
    return [full(SSM_WIDTH, SSM_WIDTH), full(1, SSM_WIDTH), full(ATTN_WIDTH, D_MODEL), full(SSM_WIDTH, D_MODEL),
            full(1, D_MODEL), full(1, D_MODEL), full(D_MODEL, dff), full(D_MODEL, dff), full(CONV_W, dff),
            full(1, dff), full(dff, D_MODEL), full(1, D_MODEL)]


def _ffn_prompt(x, attn2, y4, weights):
    L = x.shape[0]
    tl = ROW_TILE
    dff = weights[6].shape[1]
    rows = lambda w: pl.BlockSpec((tl, w), lambda i: (i, 0))
    slabs = lambda a: pl.BlockSpec((a.shape[0], tl, LANES), lambda i: (0, i, 0))
    return pl.pallas_call(
        functools.partial(_ffn_prompt_kernel, tl=tl, dff=dff),
        grid=(L // tl,),
        in_specs=[rows(D_MODEL), slabs(attn2), slabs(y4)] + _weight_specs(dff),
        out_specs=[rows(D_MODEL), pl.BlockSpec((8, dff), lambda i: (0, 0))],
        out_shape=[jax.ShapeDtypeStruct((L, D_MODEL), F32), jax.ShapeDtypeStruct((8, dff), F32)],
        scratch_shapes=[pltpu.VMEM((dff // FF_CHUNK, 8, FF_CHUNK), F32), pltpu.VMEM((tl, dff), BF16)],
        compiler_params=_cparams(1, V7X_VMEM_LIMIT),
        name="ffn_prompt",
    )(x, attn2, y4, *weights)


def _ffn_sample(x, attn_bf, y, buf0, buf1, weights):
    nb = x.shape[0]
    dff = weights[6].shape[1]
    rows = lambda w: pl.BlockSpec((nb, w), lambda i: (0, 0))
    return pl.pallas_call(
        functools.partial(_ffn_sample_kernel, dff=dff),
        grid=(1,),
        in_specs=[rows(D_MODEL), rows(ATTN_WIDTH), rows(SSM_WIDTH), rows(dff), rows(dff)] + _weight_specs(dff),
        out_specs=[rows(D_MODEL), rows(dff)],
        out_shape=[jax.ShapeDtypeStruct((nb, D_MODEL), F32), jax.ShapeDtypeStruct((nb, dff), F32)],
        compiler_params=_cparams(1, V7X_VMEM_LIMIT),
        name="ffn_sample",
    )(x, attn_bf, y, buf0, buf1, *weights)


def _inproj_sample_kernel(x_ref, g_ref, w_ref, o_ref):
    o_ref[...] = _dot(_rms(x_ref[...], g_ref[...]).astype(BF16), w_ref[...])


def _inproj_sample(x, g, w_bf):
    nb = x.shape[0]
    return pl.pallas_call(
        _inproj_sample_kernel,
        out_shape=jax.ShapeDtypeStruct((nb, w_bf.shape[1]), F32),
        compiler_params=pltpu.CompilerParams(vmem_limit_bytes=V7X_VMEM_LIMIT),
        name="inproj_sample",
    )(x, g, w_bf)


def _top_sample_kernel(g_ref, top_ref):
    gate = g_ref[...]
    nb = gate.shape[1]
    blk = lax.broadcasted_iota(jnp.int32, gate.shape, 1).astype(F32)
    picks = [first for first, _ in _top_blocks(gate, blk, nb, axis=1)]
    top_ref[...] = jnp.concatenate(picks, axis=1).astype(jnp.int32)


def _top_sample(gates):
    S, nb, H = gates.shape
    return pl.pallas_call(
        _top_sample_kernel,
        out_shape=jax.ShapeDtypeStruct((S, MOBA_TOPK, H), jnp.int32),
        name="top_sample",
    )(gates)


def _attn_sample_kernel(pt_ref, top_ref, q_ref, kn_ref, vn_ref, slope_ref, ck_ref, cv_ref, o_ref,
                        kbuf, vbuf, sems, *, n_sel, n_pages, page, past_len):
    H = N_HEADS
    s_i = pl.program_id(0)
    n_seq = pl.num_programs(0)
    scale = HEAD_DIM ** -0.5
    ppb = MOBA_BLOCK // page
    off = lax.broadcasted_iota(jnp.int32, (1, page), 1)

    def block_of(seq, h, r):
        return top_ref[(seq * MOBA_TOPK + r // ppb) * H + h]

    def copies(seq, slot):
        out = []
        for h in range(H):
            for r in range(n_sel):
                pg = pt_ref[seq * n_pages + block_of(seq, h, r) * ppb + r % ppb]
                out.append(pltpu.make_async_copy(ck_ref.at[pg, h], kbuf.at[slot, h * n_sel + r], sems.at[0, slot]))
                out.append(pltpu.make_async_copy(cv_ref.at[pg, h], vbuf.at[slot, h * n_sel + r], sems.at[1, slot]))
        return out

    slot = s_i % 2

    @pl.when(s_i == 0)
    def _():
        for c in copies(s_i, slot):
            c.start()

    @pl.when(s_i + 1 < n_seq)
    def _():
        for c in copies(s_i + 1, 1 - slot):
            c.start()

    for c in copies(s_i, slot):
        c.wait()

    qbs = [_round_bf16(q_ref[0, h]) for h in range(H)]
    scores = []
    for r in range(n_sel):
        qk_rows, dist_rows = [], []
        for h in range(H):
            blk = block_of(s_i, h, r)
            dist_rows.append((past_len - (blk * MOBA_BLOCK + (r % ppb) * page + off)).astype(F32))
            qk_rows.append(jnp.sum(_round_bf16(kbuf[slot, h * n_sel + r]) * qbs[h], axis=0, keepdims=True))
        scores.append(jnp.concatenate(qk_rows, axis=0) * scale
                      - slope_ref[...] * jnp.concatenate(dist_rows, axis=0))
    s_own = jnp.concatenate([jnp.sum(qbs[h] * _round_bf16(kn_ref[0, h]), axis=0, keepdims=True)
                             for h in range(H)], axis=0) * scale
    m = s_own
    for s in scores:
        m = jnp.maximum(m, jnp.max(s, axis=1, keepdims=True))
    p_own = jnp.exp(s_own - m)
    ps = [jnp.exp(s - m) for s in scores]
    l = p_own
    for p in ps:
        l = l + jnp.sum(p, axis=1, keepdims=True)
    inv = 1.0 / l
    pn = [_round_bf16(p * inv) for p in ps]
    pn_own = _round_bf16(p_own * inv)
    for h in range(H):
        acc = pn[0][h:h + 1, :] * _round_bf16(vbuf[slot, h * n_sel])
        for r in range(1, n_sel):
            acc = acc + pn[r][h:h + 1, :] * _round_bf16(vbuf[slot, h * n_sel + r])
        o_ref[0, h] = jnp.sum(acc, axis=1, keepdims=True) + pn_own[h:h + 1, :] * _round_bf16(vn_ref[0, h])


def _attn_sample(q, k_new, v_new, ck, cv, page_table, top, slopes_page):
    n_pool, H, Dh, page = ck.shape
    S, n_pages = page_table.shape
    ppb = MOBA_BLOCK // page
    n_sel = MOBA_TOPK * ppb
    past_len = n_pages * page

    col = pl.BlockSpec((1, H, Dh, 1), lambda s, pt, tp: (s, 0, 0, 0))
    cols = lambda a: a.reshape(S, H, Dh, 1)
    hbm = pl.BlockSpec(memory_space=pl.ANY)
    tiles = pltpu.VMEM((2, H * n_sel, Dh, page), F32)
    out = pl.pallas_call(
        functools.partial(_attn_sample_kernel, n_sel=n_sel, n_pages=n_pages, page=page, past_len=past_len),
        grid_spec=pltpu.PrefetchScalarGridSpec(
            num_scalar_prefetch=2,
            grid=(S,),
            in_specs=[col, col, col, pl.BlockSpec((H, page), lambda s, pt, tp: (0, 0)), hbm, hbm],
            out_specs=col,
            scratch_shapes=[tiles, tiles, pltpu.SemaphoreType.DMA((2, 2))],
        ),
        out_shape=jax.ShapeDtypeStruct((S, H, Dh, 1), F32),
        compiler_params=_cparams(1),
        name="attn_sample",
    )(page_table.reshape(-1), top.reshape(-1), cols(q), cols(k_new), cols(v_new), slopes_page, ck, cv)
    return out.reshape(S, H * Dh)


def _ssm_sample_kernel(u_ref, sre_ref, sim_ref, are_ref, aim_ref, f_ref, e_ref, lag_ref, y_ref, nre_ref, nim_ref):
    SW = GROUPS_PER_SLAB * SSM_STATE
    for s in range(SLABS):
        cs = slice(s * SW, (s + 1) * SW)
        ls = slice(s * LANES, (s + 1) * LANES)
        ub = u_ref[:, ls].astype(BF16)
        s0r, s0i = sre_ref[:, cs], sim_ref[:, cs]
        ar, ai = are_ref[:, cs], aim_ref[:, cs]
        b = _dot(ub, f_ref[s])
        nre_ref[:, cs] = ar * s0r - ai * s0i + b[:, :SW]
        nim_ref[:, cs] = ar * s0i + ai * s0r + b[:, SW:]
        s0 = jnp.concatenate([s0r, s0i], axis=1).astype(BF16)
        y_ref[:, ls] = _dot(s0, e_ref[s, 0, :, :LANES]) + _dot(ub, lag_ref[s, :, :LANES])


def _ssm_sample(u, s_re, s_im, tb, lag, f, e):
    S = u.shape[0]
    GN = N_SSM_GROUPS * SSM_STATE
    st = jax.ShapeDtypeStruct((S, GN), F32)
    whole = lambda a: pl.BlockSpec(a.shape, lambda i: (0,) * a.ndim)
    args = (u, s_re.reshape(S, GN), s_im.reshape(S, GN), tb["abar_re"].reshape(1, GN), tb["abar_im"].reshape(1, GN))
    return pl.pallas_call(
        _ssm_sample_kernel,
        grid=(1,),
        in_specs=[whole(a) for a in args]
        + [pl.BlockSpec((SLABS, LANES, f.shape[2]), lambda i: (0, SSM_CHUNK - 1, 0)),
           pl.BlockSpec((SLABS, 1) + e.shape[2:], lambda i: (0, 0, 0, 0)),
           pl.BlockSpec((SLABS, LANES, lag.shape[2]), lambda i: (0, 1, 0))],
        out_specs=[pl.BlockSpec((S, SSM_WIDTH), lambda i: (0, 0)), pl.BlockSpec((S, GN), lambda i: (0, 0)),
                   pl.BlockSpec((S, GN), lambda i: (0, 0))],
        out_shape=[jax.ShapeDtypeStruct((S, SSM_WIDTH), F32), st, st],
        compiler_params=_cparams(1, V7X_VMEM_LIMIT),
        name="ssm_sample",
    )(*args, f, e, lag)


def _layer(x, xs, cache_k, cache_v, page_table, s_re, s_im, conv_buf, lw):
    S = xs.shape[0]
    proj = _inproj_sample(xs, lw["g_mix_pre"], lw["w_in"])
    q = proj[:, :ATTN_WIDTH]
    k = proj[:, ATTN_WIDTH:2 * ATTN_WIDTH]
    v = proj[:, 2 * ATTN_WIDTH:3 * ATTN_WIDTH]
    u = proj[:, 3 * ATTN_WIDTH:]
    ck = jnp.transpose(cache_k, (0, 2, 3, 1))
    cv = jnp.transpose(cache_v, (0, 2, 3, 1))

    kt, vt, u4, qa, ka, va, sel, qn2, kn2 = _inproj_prompt(x, lw["g_mix_pre"], lw["w_in"], lw["k_aug"])
    keep = _alibi_keep_blocks(lw["slopes"], qn2[:, 0], kn2[:, 0], ka.shape[1])
    attn2, gates = _attn_prompt(keep, qa, ka, va, sel, lw["slopes_q"], q, ck, page_table)
    y4, f_re, f_im = _ssm_prompt(u4, *lw["ssm_chunk"])
    out, conv = _ffn_prompt(x, attn2, y4, lw["ffn"])
    prompt = (out, kt, vt, f_re.reshape(N_SSM_GROUPS, SSM_STATE), f_im.reshape(N_SSM_GROUPS, SSM_STATE),
              conv[8 - (CONV_W - 1):])

    attn = _attn_sample(q, k, v, ck, cv, page_table, _top_sample(gates), lw["slopes_page"])
    y, n_re, n_im = _ssm_sample(u, s_re, s_im, lw["ssm_tb"], *lw["ssm_chunk"][:3])
    outs, g = _ffn_sample(xs, attn.astype(BF16), y, conv_buf[:, 0], conv_buf[:, 1], lw["ffn"])
    conv_new = jnp.stack([conv_buf[:, 1], g], axis=1)
    sample = (outs, k, v, n_re.reshape(S, N_SSM_GROUPS, SSM_STATE), n_im.reshape(S, N_SSM_GROUPS, SSM_STATE),
              conv_new)
    return prompt, sample


def _alibi_key_table(slopes):
    off = jnp.arange(MOBA_BLOCK, dtype=F32)[None, :] * (slopes * LOG2E)[:, None]
    to_bf16 = lambda a: lax.reduce_precision(a, exponent_bits=8, mantissa_bits=7)
    t0 = to_bf16(off)
    t1 = to_bf16(off - t0)
    t2 = to_bf16(off - t0 - t1)
    terms = jnp.stack([t0, t1, t2], axis=-1).astype(BF16)
    half = jnp.pad(terms, ((0, 0), (0, 0), (0, HEAD_DIM - 3)))
    zero = jnp.zeros_like(half)
    odd = (jnp.arange(N_HEADS) % 2 == 1)[:, None, None]
    return jnp.where(odd, jnp.concatenate([half, zero], axis=-1), jnp.concatenate([zero, half], axis=-1))


def kernel(x_prompt, x_sample, cache_k, cache_v, page_table, state_ssm_re, state_ssm_im, state_conv,
           norm_mix_pre, norm_mix_post, w_in, ssm_a_re, ssm_a_im, ssm_log_step, ssm_b_re, ssm_b_im,
           ssm_c_re, ssm_c_im, ssm_d, w_glu, b_glu, w_out, norm_ffn_pre, norm_ffn_post,
           w_gate, w_up, conv_w, conv_b, w_down):
    depth = w_in.shape[0]
    bp, lp_len = x_prompt.shape[:2]
    bs, ls_len = x_sample.shape[:2]
    page = cache_k.shape[2]
    assert bp == 1 and ls_len == 1 and lp_len % ATTN_TQ == 0 and page == LANES
    slopes = jnp.exp2(-8.0 * jnp.arange(1, N_HEADS + 1, dtype=F32) / N_HEADS)
    hp = x_prompt[0]
    hs = x_sample[:, 0]
    outs = [[] for _ in range(10)]
    for l in range(depth):
        tb = _ssm_tables(ssm_a_re[l], ssm_a_im[l], ssm_log_step[l], ssm_b_re[l], ssm_b_im[l])
        row = lambda a: a[l].reshape(1, -1)
        lw = dict(
            g_mix_pre=row(norm_mix_pre), w_in=w_in[l].astype(BF16),
            k_aug=_alibi_key_table(slopes),
            slopes=slopes,
            slopes_q=jnp.broadcast_to((slopes * LOG2E)[:, None, None], (N_HEADS, 1, MOBA_BLOCK)),
            slopes_page=jnp.broadcast_to(slopes[:, None], (N_HEADS, page)),
            ssm_tb=tb, c_re=ssm_c_re[l], c_im=ssm_c_im[l], d=ssm_d[l],
            ssm_chunk=_ssm_chunk_tables(tb, ssm_c_re[l], ssm_c_im[l], ssm_d[l]),
            ffn=[w_glu[l].astype(BF16), row(b_glu), w_out[l, :ATTN_WIDTH].astype(BF16),
                 w_out[l, ATTN_WIDTH:].astype(BF16), row(norm_mix_post), row(norm_ffn_pre),
                 w_gate[l].astype(BF16), w_up[l].astype(BF16), conv_w[l], row(conv_b),
                 w_down[l].astype(BF16), row(norm_ffn_post)],
        )
        (hp, ktp, vtp, sr, si, cp), (hs, ks, vs, srs, sis, cs) = _layer(
            hp, hs, cache_k[l], cache_v[l], page_table, state_ssm_re[l], state_ssm_im[l], state_conv[l], lw)
        outs[0].append(jnp.transpose(ktp, (2, 0, 1))[None])
        outs[1].append(jnp.transpose(vtp, (2, 0, 1))[None])
        outs[4].append(sr[None])
        outs[5].append(si[None])
        outs[8].append(cp[None])
        sr, si = srs, sis
        outs[2].append(ks.reshape(bs, ls_len, N_HEADS, HEAD_DIM))
        outs[3].append(vs.reshape(bs, ls_len, N_HEADS, HEAD_DIM))
        outs[6].append(sr)
        outs[7].append(si)
        outs[9].append(cs)
    return (hp[None], hs[:, None], *[jnp.stack(o) for o in outs])
```

```python
import functools
import math

import jax
import jax.numpy as jnp
from jax import lax
from jax.experimental import pallas as pl
from jax.experimental.pallas import tpu as pltpu

F32 = jnp.float32
BF16 = jnp.bfloat16

D_MODEL = 1024
N_HEADS = 8
HEAD_DIM = 64
ATTN_WIDTH = N_HEADS * HEAD_DIM
SSM_WIDTH = D_MODEL - ATTN_WIDTH
MOBA_BLOCK = 256
MOBA_TOPK = 3
SSM_GROUP = 16
N_SSM_GROUPS = SSM_WIDTH // SSM_GROUP
SSM_STATE = 64
SSM_CHUNK = 16
CONV_W = 3
RMS_EPS = 1e-6
NEG = -1e30
LOG2E = 1.4426950408889634
LANES = 128
V7X_VMEM_LIMIT = 56 * 1024 * 1024
HI = lax.Precision.HIGHEST

ROW_TILE = 512
V_ROWS = 80
SLABS = SSM_WIDTH // LANES
GROUPS_PER_SLAB = LANES // SSM_GROUP


def _cparams(n_axes, vmem=None):
    return pltpu.CompilerParams(dimension_semantics=("arbitrary",) * n_axes, vmem_limit_bytes=vmem)


def _rms(x, g):
    return x * lax.rsqrt(jnp.mean(x * x, axis=-1, keepdims=True) + RMS_EPS) * g


def _gelu_tanh(x):
    return 0.5 * x * (1.0 + jnp.tanh(math.sqrt(2.0 / math.pi) * (x + 0.044715 * (x * x * x))))


def _sigmoid(x):
    return 1.0 / (1.0 + jnp.exp(-x))


def _split_bf16(a):
    hi = a.astype(BF16)
    lo = (a - hi.astype(F32)).astype(BF16)
    return hi, lo


def _round_bf16(a):
    return a.astype(BF16).astype(F32)


def _dot(a, b):
    return jnp.dot(a, b, preferred_element_type=F32)


def _dot_nt(a, b):
    return lax.dot_general(a, b, (((1,), (1,)), ((), ())), preferred_element_type=F32)


def _top_blocks(cur, blk, n, axis=0):
    picks = []
    for _ in range(MOBA_TOPK):
        mx = jnp.max(cur, axis=axis, keepdims=True)
        first = jnp.min(jnp.where(cur == mx, blk, float(n)), axis=axis, keepdims=True)
        picks.append((first, mx))
        cur = jnp.where(blk == first, NEG, cur)
    return picks


def _inproj_prompt_kernel(x_ref, g_ref, w_ref, aug_ref, kt_ref, vt_ref, u_ref, qa_ref, ka_ref, va_ref, sel_ref,
                          qn_ref, kn_ref, kmean_s, *, tl, nb):
    i = pl.program_id(0)
    bpt = tl // MOBA_BLOCK
    H, Dh, B = N_HEADS, HEAD_DIM, MOBA_BLOCK

    @pl.when(i == 0)
    def _():
        kmean_s[...] = jnp.zeros_like(kmean_s)
        qn_ref[...] = jnp.zeros_like(qn_ref)
        kn_ref[...] = jnp.zeros_like(kn_ref)

    h = _rms(x_ref[...], g_ref[...]).astype(BF16)
    proj = _dot(h, w_ref[...])
    q = proj[:, :ATTN_WIDTH]
    k = proj[:, ATTN_WIDTH:2 * ATTN_WIDTH]
    v = proj[:, 2 * ATTN_WIDTH:3 * ATTN_WIDTH]
    for s in range(SLABS):
        u_ref[s] = proj[:, 3 * ATTN_WIDTH + s * LANES:3 * ATTN_WIDTH + (s + 1) * LANES]

    kt = k.T
    vt = v.T
    qt = (q * (Dh ** -0.5 * LOG2E)).T.astype(BF16)
    kt_ref[...] = kt.reshape(H, Dh, tl)
    vt_ref[...] = vt.reshape(H, Dh, tl)

    def head_norm2(t):
        t = t.astype(F32)
        n2 = jnp.sum((t * t).reshape(H, Dh, tl), axis=1)
        return jnp.broadcast_to(jnp.max(n2, axis=1, keepdims=True), (H, LANES))

    qn_ref[...] = jnp.maximum(qn_ref[...], head_norm2(qt))
    kn_ref[...] = jnp.maximum(kn_ref[...], head_norm2(kt))

    sub = lax.broadcasted_iota(jnp.int32, (Dh, tl), 0)
    ones3 = jnp.where(sub < 3, 1.0, 0.0).astype(BF16)
    kb = k.astype(BF16)
    lane_hi = lax.broadcasted_iota(jnp.int32, (B, LANES), 1) >= Dh
    vtb = vt.astype(BF16)
    ones_rows = jnp.ones((V_ROWS - Dh, B), BF16)
    for hd in range(H):
        qh = qt[hd * Dh:(hd + 1) * Dh, :]
        odd = hd % 2 == 1
        qa_ref[hd] = jnp.concatenate([ones3, qh] if odd else [qh, ones3], axis=0)
        for b in range(bpt):
            slab = kb[b * B:(b + 1) * B, (hd // 2) * LANES:(hd // 2 + 1) * LANES]
            ka_ref[hd, b] = jnp.where(lane_hi == odd, slab, aug_ref[hd])
            va_ref[hd, b] = jnp.concatenate([vtb[hd * Dh:(hd + 1) * Dh, b * B:(b + 1) * B], ones_rows], axis=0)

    row = lax.broadcasted_iota(jnp.int32, kmean_s.shape, 0)
    km = kmean_s[...]
    for b in range(bpt):
        kmb = jnp.mean(k[b * B:(b + 1) * B, :], axis=0, keepdims=True)
        km = jnp.where(row == i * bpt + b, kmb, km)
    kmean_s[...] = km

    blk = lax.broadcasted_iota(jnp.int32, (nb, tl), 0).astype(F32)
    own = ((i * tl + lax.broadcasted_iota(jnp.int32, (1, tl), 1)) // B).astype(F32)
    kmb16 = km.astype(BF16)
    qb16 = q.astype(BF16)
    for hd in range(H):
        sl = slice(hd * Dh, (hd + 1) * Dh)
        gate = _dot_nt(kmb16[:, sl], qb16[:, sl])
        chosen = jnp.zeros((nb, tl), F32)
        for first, mx in _top_blocks(jnp.where(blk < own, gate, NEG), blk, nb):
            chosen = jnp.where((blk == first) & (mx > 0.5 * NEG), 1.0, chosen)
        sel_ref[hd] = jnp.where(chosen > 0.5, 0.0, NEG)


def _inproj_prompt(x, g, w_bf, aug):
    L = x.shape[0]
    tl = ROW_TILE
    nb = L // MOBA_BLOCK
    bpt = tl // MOBA_BLOCK
    H, Dh, B = N_HEADS, HEAD_DIM, MOBA_BLOCK
    tcol = pl.BlockSpec((H, Dh, tl), lambda i: (0, 0, i))
    return pl.pallas_call(
        functools.partial(_inproj_prompt_kernel, tl=tl, nb=nb),
        grid=(L // tl,),
        in_specs=[pl.BlockSpec((tl, D_MODEL), lambda i: (i, 0)),
                  pl.BlockSpec((1, D_MODEL), lambda i: (0, 0)),
                  pl.BlockSpec((D_MODEL, 4 * ATTN_WIDTH), lambda i: (0, 0)),
                  pl.BlockSpec((H, B, LANES), lambda i: (0, 0, 0))],
        out_specs=[tcol, tcol,
                   pl.BlockSpec((SLABS, tl, LANES), lambda i: (0, i, 0)),
                   pl.BlockSpec((H, 2 * Dh, tl), lambda i: (0, 0, i)),
                   pl.BlockSpec((H, bpt, B, LANES), lambda i: (0, i, 0, 0)),
                   pl.BlockSpec((H, bpt, V_ROWS, B), lambda i: (0, i, 0, 0)),
                   pl.BlockSpec((H, nb, tl), lambda i: (0, 0, i)),
                   pl.BlockSpec((H, LANES), lambda i: (0, 0)),
                   pl.BlockSpec((H, LANES), lambda i: (0, 0))],
        out_shape=[jax.ShapeDtypeStruct((H, Dh, L), F32), jax.ShapeDtypeStruct((H, Dh, L), F32),
                   jax.ShapeDtypeStruct((SLABS, L, LANES), F32),
                   jax.ShapeDtypeStruct((H, 2 * Dh, L), BF16),
                   jax.ShapeDtypeStruct((H, nb, B, LANES), BF16),
                   jax.ShapeDtypeStruct((H, nb, V_ROWS, B), BF16),
                   jax.ShapeDtypeStruct((H, nb, L), F32),
                   jax.ShapeDtypeStruct((H, LANES), F32), jax.ShapeDtypeStruct((H, LANES), F32)],
        scratch_shapes=[pltpu.VMEM((nb, ATTN_WIDTH), F32)],
        compiler_params=_cparams(1, V7X_VMEM_LIMIT),
        name="inproj_prompt",
    )(x, g, w_bf, aug)


HEADS_PER_STEP = 2
ATTN_TQ = 1024
UNDERFLOW_LOG2 = 154.0


def _block_gates(page_of, n_pages, ppb, qb, ind):
    lane = lax.broadcasted_iota(jnp.int32, (ATTN_WIDTH, LANES), 1)
    prods = jnp.zeros((ATTN_WIDTH, LANES), F32)
    for b in range(n_pages // ppb):
        tot = page_of(b * ppb)
        for r in range(1, ppb):
            tot = tot + page_of(b * ppb + r)
        kmean = jnp.sum(tot, axis=1, keepdims=True) * (1.0 / MOBA_BLOCK)
        prods = jnp.where(lane == b, _round_bf16(kmean) * qb, prods)
    p_hi, p_lo = _split_bf16(prods)
    return _dot(ind, p_hi) + _dot(ind, p_lo)


def _attn_prompt_kernel(w_ref, pt_ref, qa_ref, ka_ref, va_ref, sel_ref, slope_ref, qs_ref, ind_ref, ck_ref,
                        o_ref, gate_ref, s_scr, d_scr, acc_scr, pbuf, psem, *, tq, pps, ppb):
    hp = pl.program_id(0)
    qi = pl.program_id(1)
    step_id = hp * pl.num_programs(1) + qi
    n_steps = pl.num_programs(0) * pl.num_programs(1)
    slot = step_id % 2

    def page_copies(step, slot):
        return [pltpu.make_async_copy(ck_ref.at[pt_ref[step * pps + r]], pbuf.at[slot, r], psem.at[slot])
                for r in range(pps)]

    @pl.when(step_id == 0)
    def _():
        for c in page_copies(step_id, slot):
            c.start()

    @pl.when(step_id + 1 < n_steps)
    def _():
        for c in page_copies(step_id + 1, 1 - slot):
            c.start()

    for c in page_copies(step_id, slot):
        c.wait()

    B, Dh = MOBA_BLOCK, HEAD_DIM
    bpq = tq // B
    units = [(e, cb) for e in range(HEADS_PER_STEP) for cb in range(bpq)]
    n_off = qi * bpq + (bpq - 1)
    keep = w_ref[hp * HEADS_PER_STEP]
    for e in range(1, HEADS_PER_STEP):
        keep = jnp.maximum(keep, w_ref[hp * HEADS_PER_STEP + e])
    j_start = jnp.maximum(qi * bpq - keep, 0)
    lane = lax.broadcasted_iota(jnp.int32, (1, B), 1).astype(F32)
    causal = lax.broadcasted_iota(jnp.int32, (B, B), 0) <= lax.broadcasted_iota(jnp.int32, (B, B), 1)

    def q_of(e, cb):
        return qa_ref[e, :, cb * B:(cb + 1) * B]

    def slope_of(e):
        return slope_ref[e]

    for u, (e, cb) in enumerate(units):
        d_scr[u] = _dot(ka_ref[e, qi * bpq + cb], q_of(e, cb))
        s_scr[u] = _dot(ka_ref[e, j_start], q_of(e, cb))

    gate_ref[0] = _block_gates(lambda r: pbuf[slot, r].reshape(ATTN_WIDTH, -1), pps, ppb,
                               _round_bf16(qs_ref[0]), ind_ref[...])

    ms = []
    for u, (e, cb) in enumerate(units):
        s = jnp.where(causal, d_scr[u] - slope_of(e) * lane, NEG)
        m = jnp.max(s, axis=0, keepdims=True)
        acc_scr[u] = _dot(va_ref[e, qi * bpq + cb], jnp.exp2(s - m).astype(BF16))
        ms.append(m)

    def step(j, ms):
        nxt = jnp.minimum(j + 1, n_off)
        out = []
        for u, (e, cb) in enumerate(units):
            s = s_scr[u]
            dist = lane + ((qi * bpq + cb - j) * B).astype(F32)
            col = sel_ref[e, pl.ds(j, 1), cb * B:(cb + 1) * B] - slope_of(e) * dist
            m_new = jnp.maximum(ms[u], jnp.max(s, axis=0, keepdims=True) + col)
            p = jnp.exp2(s - (m_new - col)).astype(BF16)
            acc_scr[u] = jnp.exp2(ms[u] - m_new) * acc_scr[u] + _dot(va_ref[e, j], p)
            s_scr[u] = _dot(ka_ref[e, nxt], q_of(e, cb))
            out.append(m_new)
        return tuple(out)

    def body(t, ms):
        j = j_start + 2 * t
        return step(j + 1, step(j, ms))

    lax.fori_loop(0, (n_off - j_start + 1) // 2, body, tuple(ms))
    for cb in range(bpq):
        outs = []
        for e in range(HEADS_PER_STEP):
            acc = acc_scr[e * bpq + cb]
            outs.append((acc[:Dh, :] / acc[Dh:Dh + 1, :]).T)
        o_ref[0, cb * B:(cb + 1) * B, :] = jnp.concatenate(outs, axis=1).astype(o_ref.dtype)


def _alibi_keep_blocks(slopes, qn2, kn2, nb):
    qk = jnp.sqrt(qn2 * kn2) * 1.02
    need = (2.0 * qk + UNDERFLOW_LOG2) / (slopes * LOG2E)
    w = jnp.ceil((need - 1.0) / MOBA_BLOCK)
    return jnp.clip(w, 1.0, float(nb)).astype(jnp.int32)


def _attn_prompt(keep, qa, ka, va, sel, slopes, q_sample, ck, page_table):
    H, nb, B, _ = ka.shape
    L = qa.shape[2]
    tq = ATTN_TQ
    hp = HEADS_PER_STEP
    nq = L // tq
    n_steps = (H // hp) * nq
    S, n_pages = page_table.shape
    page = ck.shape[3]
    ppb = MOBA_BLOCK // page
    pps = (S * n_pages) // n_steps
    assert pps * n_steps == S * n_pages and n_pages % pps == 0 and pps % ppb == 0 and pps // ppb <= LANES
    W = H * HEAD_DIM
    ind = (jnp.arange(H)[:, None] == jnp.arange(W)[None, :] // HEAD_DIM).astype(BF16)
    q_rep = jnp.broadcast_to(q_sample.reshape(S, W, 1), (S, W, LANES))
    once = pl.Buffered(1)
    attn, gates = pl.pallas_call(
        functools.partial(_attn_prompt_kernel, tq=tq, pps=pps, ppb=ppb),
        grid_spec=pltpu.PrefetchScalarGridSpec(
            num_scalar_prefetch=2,
            grid=(H // hp, nq),
            in_specs=[pl.BlockSpec((hp, 2 * HEAD_DIM, tq), lambda h, i, w, pt: (h, 0, i)),
                      pl.BlockSpec((hp, nb, B, LANES), lambda h, i, w, pt: (h, 0, 0, 0), pipeline_mode=once),
                      pl.BlockSpec((hp, nb, V_ROWS, B), lambda h, i, w, pt: (h, 0, 0, 0), pipeline_mode=once),
                      pl.BlockSpec((hp, nb, tq), lambda h, i, w, pt: (h, 0, i)),
                      pl.BlockSpec((hp, 1, B), lambda h, i, w, pt: (h, 0, 0)),
                      pl.BlockSpec((1, W, LANES), lambda h, i, w, pt: (((h * nq + i) * pps) // n_pages, 0, 0)),
                      pl.BlockSpec((H, W), lambda h, i, w, pt: (0, 0)),
                      pl.BlockSpec(memory_space=pl.ANY)],
            out_specs=[pl.BlockSpec((1, tq, hp * HEAD_DIM), lambda h, i, w, pt: (h, i, 0)),
                       pl.BlockSpec((1, H, LANES), lambda h, i, w, pt: (h * nq + i, 0, 0))],
            scratch_shapes=[pltpu.VMEM((hp * tq // B, B, B), F32), pltpu.VMEM((hp * tq // B, B, B), F32),
                            pltpu.VMEM((hp * tq // B, V_ROWS, B), F32),
                            pltpu.VMEM((2, pps) + ck.shape[1:], F32), pltpu.SemaphoreType.DMA((2,))],
        ),
        out_shape=[jax.ShapeDtypeStruct((H // hp, L, hp * HEAD_DIM), BF16),
                   jax.ShapeDtypeStruct((n_steps, H, LANES), F32)],
        compiler_params=_cparams(2, V7X_VMEM_LIMIT),
        name="attn_prompt",
    )(keep, page_table.reshape(-1), qa, ka, va, sel, slopes, q_rep, ind, ck)
    bps = pps // ppb
    gates = jnp.transpose(gates[:, :, :bps], (0, 2, 1)).reshape(S, n_pages // ppb, H)
    return attn, gates


def _ssm_tables(a_re, a_im, log_step, b_re, b_im):
    T = SSM_CHUNK
    dt = jnp.exp(log_step)[:, None]
    j = jnp.arange(T + 1, dtype=F32)[:, None, None]
    mag = jnp.exp(a_re * dt * j)
    pw_re = mag * jnp.cos(a_im * dt * j)
    pw_im = mag * jnp.sin(a_im * dt * j)
    abar_re, abar_im = pw_re[1], pw_im[1]
    den = a_re * a_re + a_im * a_im
    nr = abar_re - 1.0
    ni = abar_im
    coef_re = (nr * a_re + ni * a_im) / den
    coef_im = (ni * a_re - nr * a_im) / den
    bb_re = coef_re[..., None] * b_re - coef_im[..., None] * b_im
    bb_im = coef_re[..., None] * b_im + coef_im[..., None] * b_re
    return dict(pw_re=pw_re, pw_im=pw_im, bb_re=bb_re, bb_im=bb_im, abar_re=abar_re, abar_im=abar_im)


def _spread_groups(compact, rows_per_group, cols_per_group, col_outer):
    gs = GROUPS_PER_SLAB
    rows = compact.shape[-2]
    src = jnp.arange(col_outer * cols_per_group)
    dst = jnp.arange(col_outer * gs * cols_per_group)
    same_outer = src[:, None] // cols_per_group == dst[None, :] // (gs * cols_per_group)
    same_c = src[:, None] % cols_per_group == dst[None, :] % cols_per_group
    rep = (same_outer & same_c).astype(BF16)
    row_group = (jnp.arange(rows) // rows_per_group) % gs
    col_group = (dst // cols_per_group) % gs
    wide = jnp.dot(compact, rep, preferred_element_type=F32)
    return jnp.where(row_group[:, None] == col_group[None, :], wide, 0.0).astype(BF16)


def _ssm_chunk_tables(tb, c_re, c_im, d):
    pw_re, pw_im, bb_re, bb_im = tb["pw_re"], tb["pw_im"], tb["bb_re"], tb["bb_im"]
    T = SSM_CHUNK
    G, N, P = bb_re.shape
    x_re = pw_re[:T, :, :, None] * bb_re[None] - pw_im[:T, :, :, None] * bb_im[None]
    x_im = pw_re[:T, :, :, None] * bb_im[None] + pw_im[:T, :, :, None] * bb_re[None]
    kj = (jnp.einsum("gpn,jgnq->jgqp", c_re, x_re, precision=HI)
          - jnp.einsum("gpn,jgnq->jgqp", c_im, x_im, precision=HI))
    kj = kj.at[0].add(jnp.eye(P, dtype=F32)[None] * d[:, :, None])
    gs = GROUPS_PER_SLAB
    kpad = jnp.concatenate([jnp.zeros_like(kj[:1]), kj], axis=0)
    kc = jnp.stack([kpad[:T], kpad[1:]], axis=3).reshape(T, SLABS, gs, P, 2, P)
    kc = jnp.transpose(kc, (1, 0, 2, 3, 4, 5)).reshape(SLABS, T * LANES, 2 * P)
    lag = _spread_groups(kc.astype(BF16), P, P, 2)
    xc = jnp.stack([x_re[::-1], x_im[::-1]], axis=2).reshape(T, SLABS, gs, 2, N, P)
    xc = jnp.transpose(xc, (1, 0, 2, 5, 3, 4)).reshape(SLABS, T * LANES, 2 * N)
    f = _spread_groups(xc.astype(BF16), P, N, 2)
    cr = jnp.transpose(c_re, (0, 2, 1))[None]
    ci = jnp.transpose(c_im, (0, 2, 1))[None]
    ar = pw_re[1:T + 1, :, :, None]
    ai = pw_im[1:T + 1, :, :, None]
    ec = jnp.stack([cr * ar - ci * ai, -(cr * ai + ci * ar)], axis=1)
    ec = ec.reshape(T // 2, 2, 2, SLABS, gs, N, P)
    ec = jnp.transpose(ec, (3, 0, 2, 4, 5, 1, 6)).reshape(SLABS, T // 2, 2 * gs * N, 2 * P)
    e = _spread_groups(ec.astype(BF16), N, P, 2)
    a16_re = pw_re[T].reshape(1, G * N)
    a16_im = pw_im[T].reshape(1, G * N)
    return lag, f, e, a16_re, a16_im


def _chunk_steps(u_ref, rows):
    return [u_ref[0, pl.ds(s, rows, stride=SSM_CHUNK), :].astype(BF16) for s in range(SSM_CHUNK)]


def _ssm_chunk_in_kernel(u_ref, f_ref, bre_ref, bim_ref, *, rows):
    b = _dot(jnp.concatenate(_chunk_steps(u_ref, rows), axis=1), f_ref[0])
    half = b.shape[1] // 2
    bre_ref[...] = b[:, :half]
    bim_ref[...] = b[:, half:]


def _ssm_scan_kernel(bre_ref, bim_ref, are_ref, aim_ref, sre_ref, sim_ref, fre_ref, fim_ref):
    nc = bre_ref.shape[0]
    ar = are_ref[...]
    ai = aim_ref[...]

    def body(c8, carry):
        sr, si = carry
        r0 = pl.multiple_of(c8 * 8, 8)
        br = bre_ref[pl.ds(r0, 8), :]
        bi = bim_ref[pl.ds(r0, 8), :]
        rows_r, rows_i = [], []
        for r in range(8):
            rows_r.append(sr)
            rows_i.append(si)
            sr, si = (ar * sr - ai * si + br[r:r + 1, :], ar * si + ai * sr + bi[r:r + 1, :])
        sre_ref[pl.ds(r0, 8), :] = jnp.concatenate(rows_r, axis=0)
        sim_ref[pl.ds(r0, 8), :] = jnp.concatenate(rows_i, axis=0)
        return sr, si

    z = jnp.zeros(are_ref.shape, F32)
    sr, si = lax.fori_loop(0, nc // 8, body, (z, z))
    fre_ref[...] = sr
    fim_ref[...] = si


def _ssm_chunk_out_kernel(u_ref, lag_ref, e_ref, sre_ref, sim_ref, y_ref, *, rows):
    us = _chunk_steps(u_ref, rows)
    s = jnp.concatenate([sre_ref[...], sim_ref[...]], axis=1).astype(BF16)
    for pair in range(SSM_CHUNK // 2):
        tau = 2 * pair
        lhs = jnp.concatenate(us[tau + 1::-1], axis=1)
        y2 = _dot(lhs, lag_ref[0, :LANES * (tau + 2), :]) + _dot(s, e_ref[0, pair])
        y_ref[0, pl.ds(tau, rows, stride=SSM_CHUNK), :] = y2[:, :LANES]
        y_ref[0, pl.ds(tau + 1, rows, stride=SSM_CHUNK), :] = y2[:, LANES:]


SSM_ROWS = 512


def _ssm_prompt(u4, lag, f, e, a16_re, a16_im):
    L = u4.shape[1]
    nc = L // SSM_CHUNK
    rows = min(SSM_ROWS, nc)
    GN = N_SSM_GROUPS * SSM_STATE
    SW = GROUPS_PER_SLAB * SSM_STATE
    st = jax.ShapeDtypeStruct((nc, GN), F32)
    slab_rows = pl.BlockSpec((1, rows * SSM_CHUNK, LANES), lambda s, r: (s, r, 0))
    state_cols = pl.BlockSpec((rows, SW), lambda s, r: (r, s))
    b_re, b_im = pl.pallas_call(
        functools.partial(_ssm_chunk_in_kernel, rows=rows),
        grid=(SLABS, nc // rows),
        in_specs=[slab_rows, pl.BlockSpec((1,) + f.shape[1:], lambda s, r: (s, 0, 0))],
        out_specs=[state_cols, state_cols],
        out_shape=[st, st],
        compiler_params=_cparams(2, V7X_VMEM_LIMIT),
        name="ssm_chunk_in",
    )(u4, f)
    fin = jax.ShapeDtypeStruct((1, GN), F32)
    s_re, s_im, f_re, f_im = pl.pallas_call(
        _ssm_scan_kernel,
        out_shape=[st, st, fin, fin],
        compiler_params=pltpu.CompilerParams(vmem_limit_bytes=V7X_VMEM_LIMIT),
        name="ssm_scan",
    )(b_re, b_im, a16_re, a16_im)
    y4 = pl.pallas_call(
        functools.partial(_ssm_chunk_out_kernel, rows=rows),
        grid=(SLABS, nc // rows),
        in_specs=[slab_rows,
                  pl.BlockSpec((1,) + lag.shape[1:], lambda s, r: (s, 0, 0)),
                  pl.BlockSpec((1,) + e.shape[1:], lambda s, r: (s, 0, 0, 0)),
                  state_cols, state_cols],
        out_specs=slab_rows,
        out_shape=jax.ShapeDtypeStruct(u4.shape, F32),
        compiler_params=_cparams(2, V7X_VMEM_LIMIT),
        name="ssm_chunk_out",
    )(u4, lag, e, s_re, s_im)
    return y4, f_re, f_im


FF_CHUNK = 256


def _mix_and_prenorm(x, attn_bf, y, wglu_ref, bglu_ref, wouta_ref, wouts_ref, gpost_ref, gpre_ref):
    z = _gelu_tanh(y)
    ssm = z * _sigmoid(_dot(z.astype(BF16), wglu_ref[...]) + bglu_ref[...])
    mix = _dot(attn_bf, wouta_ref[...]) + _dot(ssm.astype(BF16), wouts_ref[...])
    x1 = x + _rms(mix, gpost_ref[...])
    h2 = _rms(x1, gpre_ref[...]).astype(BF16)
    return x1, h2


def _ffn_prompt_kernel(x_ref, attn_ref, y_ref, wglu_ref, bglu_ref, wouta_ref, wouts_ref, gpost_ref, gpre_ref,
                       wgate_ref, wup_ref, cw_ref, cb_ref, wdown_ref, gfpost_ref, out_ref, conv_ref, tail_s, act_s,
                       *, tl, dff):
    i = pl.program_id(0)

    @pl.when(i == 0)
    def _():
        tail_s[...] = jnp.zeros_like(tail_s)

    attn = jnp.concatenate([attn_ref[s] for s in range(attn_ref.shape[0])], axis=1)
    y = jnp.concatenate([y_ref[s] for s in range(SLABS)], axis=1)
    x1, h2 = _mix_and_prenorm(x_ref[...], attn, y, wglu_ref, bglu_ref, wouta_ref, wouts_ref, gpost_ref, gpre_ref)
    row = lax.broadcasted_iota(jnp.int32, (tl, FF_CHUNK), 0)
    for c in range(dff // FF_CHUNK):
        cs = slice(c * FF_CHUNK, (c + 1) * FF_CHUNK)
        g = _dot(h2, wgate_ref[:, cs])
        up = _dot(h2, wup_ref[:, cs])
        tail = tail_s[c]
        p1 = tail[7:8, :]
        p2 = tail[6:7, :]
        g1 = jnp.where(row == 0, p1, pltpu.roll(g, 1, 0))
        g2 = jnp.where(row == 0, p2, jnp.where(row == 1, p1, pltpu.roll(g, 2, 0)))
        gc = cw_ref[0:1, cs] * g2 + cw_ref[1:2, cs] * g1 + cw_ref[2:3, cs] * g + cb_ref[:, cs]
        act_s[:, cs] = (_gelu_tanh(gc) * up).astype(BF16)
        tail_s[c] = g[tl - 8:, :]
        conv_ref[:, cs] = g[tl - 8:, :]
    f = _dot(act_s[...], wdown_ref[...])
    out_ref[...] = x1 + _rms(f, gfpost_ref[...])


def _ffn_sample_kernel(x_ref, attn_ref, y_ref, b0_ref, b1_ref, wglu_ref, bglu_ref, wouta_ref, wouts_ref,
                       gpost_ref, gpre_ref, wgate_ref, wup_ref, cw_ref, cb_ref, wdown_ref, gfpost_ref,
                       out_ref, g_ref, *, dff):
    x1, h2 = _mix_and_prenorm(x_ref[...], attn_ref[...], y_ref[...], wglu_ref, bglu_ref, wouta_ref,
                              wouts_ref, gpost_ref, gpre_ref)
    f = jnp.zeros(x1.shape, F32)
    for c in range(dff // FF_CHUNK):
        cs = slice(c * FF_CHUNK, (c + 1) * FF_CHUNK)
        g = _dot(h2, wgate_ref[:, cs])
        up = _dot(h2, wup_ref[:, cs])
        gc = (cw_ref[0:1, cs] * b0_ref[:, cs] + cw_ref[1:2, cs] * b1_ref[:, cs] + cw_ref[2:3, cs] * g
              + cb_ref[:, cs])
        act = (_gelu_tanh(gc) * up).astype(BF16)
        f = f + _dot(act, wdown_ref[cs, :])
        g_ref[:, cs] = g
    out_ref[...] = x1 + _rms(f, gfpost_ref[...])


def _weight_specs(dff):
    c2 = lambda *_: (0, 0)
    full = lambda r, c: pl.BlockSpec((r, c), c2, pipeline_mode=pl.Buffered(1))
    return [full(SSM_WIDTH, SSM_WIDTH), full(1, SSM_WIDTH), full(ATTN_WIDTH, D_MODEL), full(SSM_WIDTH, D_MODEL),
            full(1, D_MODEL), full(1, D_MODEL), full(D_MODEL, dff), full(D_MODEL, dff), full(CONV_W, dff),
            full(1, dff), full(dff, D_MODEL), full(1, D_MODEL)]


def _ffn_prompt(x, attn2, y4, weights):
    L = x.shape[0]
    tl = ROW_TILE
    dff = weights[6].shape[1]
    rows = lambda w: pl.BlockSpec((tl, w), lambda i: (i, 0))
    slabs = lambda a: pl.BlockSpec((a.shape[0], tl, LANES), lambda i: (0, i, 0))
    return pl.pallas_call(
        functools.partial(_ffn_prompt_kernel, tl=tl, dff=dff),
        grid=(L // tl,),
        in_specs=[rows(D_MODEL), slabs(attn2), slabs(y4)] + _weight_specs(dff),
        out_specs=[rows(D_MODEL), pl.BlockSpec((8, dff), lambda i: (0, 0))],
        out_shape=[jax.ShapeDtypeStruct((L, D_MODEL), F32), jax.ShapeDtypeStruct((8, dff), F32)],
        scratch_shapes=[pltpu.VMEM((dff // FF_CHUNK, 8, FF_CHUNK), F32), pltpu.VMEM((tl, dff), BF16)],
        compiler_params=_cparams(1, V7X_VMEM_LIMIT),
        name="ffn_prompt",
    )(x, attn2, y4, *weights)


def _ffn_sample(x, attn_bf, y, buf0, buf1, weights):
    nb = x.shape[0]
    dff = weights[6].shape[1]
    rows = lambda w: pl.BlockSpec((nb, w), lambda i: (0, 0))
    return pl.pallas_call(
        functools.partial(_ffn_sample_kernel, dff=dff),
        grid=(1,),
        in_specs=[rows(D_MODEL), rows(ATTN_WIDTH), rows(SSM_WIDTH), rows(dff), rows(dff)] + _weight_specs(dff),
        out_specs=[rows(D_MODEL), rows(dff)],
        out_shape=[jax.ShapeDtypeStruct((nb, D_MODEL), F32), jax.ShapeDtypeStruct((nb, dff), F32)],
        compiler_params=_cparams(1, V7X_VMEM_LIMIT),
        name="ffn_sample",
    )(x, attn_bf, y, buf0, buf1, *weights)


def _inproj_sample_kernel(x_ref, g_ref, w_ref, o_ref):
    o_ref[...] = _dot(_rms(x_ref[...], g_ref[...]).astype(BF16), w_ref[...])


def _inproj_sample(x, g, w_bf):
    nb = x.shape[0]
    return pl.pallas_call(
        _inproj_sample_kernel,
        out_shape=jax.ShapeDtypeStruct((nb, w_bf.shape[1]), F32),
        compiler_params=pltpu.CompilerParams(vmem_limit_bytes=V7X_VMEM_LIMIT),
        name="inproj_sample",
    )(x, g, w_bf)


def _top_sample_kernel(g_ref, top_ref):
    gate = g_ref[...]
    nb = gate.shape[1]
    blk = lax.broadcasted_iota(jnp.int32, gate.shape, 1).astype(F32)
    picks = [first for first, _ in _top_blocks(gate, blk, nb, axis=1)]
    top_ref[...] = jnp.concatenate(picks, axis=1).astype(jnp.int32)


def _top_sample(gates):
    S, nb, H = gates.shape
    return pl.pallas_call(
        _top_sample_kernel,
        out_shape=jax.ShapeDtypeStruct((S, MOBA_TOPK, H), jnp.int32),
        name="top_sample",
    )(gates)


def _attn_sample_kernel(pt_ref, top_ref, q_ref, kn_ref, vn_ref, slope_ref, ck_ref, cv_ref, o_ref,
                        kbuf, vbuf, sems, *, n_sel, n_pages, page, past_len):
    H = N_HEADS
    s_i = pl.program_id(0)
    n_seq = pl.num_programs(0)
    scale = HEAD_DIM ** -0.5
    ppb = MOBA_BLOCK // page
    off = lax.broadcasted_iota(jnp.int32, (1, page), 1)

    def block_of(seq, h, r):
        return top_ref[(seq * MOBA_TOPK + r // ppb) * H + h]

    def copies(seq, slot):
        out = []
        for h in range(H):
            for r in range(n_sel):
                pg = pt_ref[seq * n_pages + block_of(seq, h, r) * ppb + r % ppb]
                out.append(pltpu.make_async_copy(ck_ref.at[pg, h], kbuf.at[slot, h * n_sel + r], sems.at[0, slot]))
                out.append(pltpu.make_async_copy(cv_ref.at[pg, h], vbuf.at[slot, h * n_sel + r], sems.at[1, slot]))
        return out

    slot = s_i % 2

    @pl.when(s_i == 0)
    def _():
        for c in copies(s_i, slot):
            c.start()

    @pl.when(s_i + 1 < n_seq)
    def _():
        for c in copies(s_i + 1, 1 - slot):
            c.start()

    for c in copies(s_i, slot):
        c.wait()

    q = q_ref[0]
    qk = []
    for h in range(H):
        q8 = jnp.broadcast_to(q[h:h + 1, :], (8, HEAD_DIM)).astype(BF16)
        kt = jnp.concatenate([kbuf[slot, h * n_sel + r] for r in range(n_sel)], axis=1).astype(BF16)
        qk.append(_dot(q8, kt)[0:1, :])
    scores = []
    for r in range(n_sel):
        dist_rows = [(past_len - (block_of(s_i, h, r) * MOBA_BLOCK + (r % ppb) * page + off)).astype(F32)
                     for h in range(H)]
        qk_r = jnp.concatenate([qk[h][:, r * page:(r + 1) * page] for h in range(H)], axis=0)
        scores.append(qk_r * scale - slope_ref[...] * jnp.concatenate(dist_rows, axis=0))
    s_own = jnp.sum(_round_bf16(q) * _round_bf16(kn_ref[0]), axis=1, keepdims=True) * scale
    m = s_own
    for s in scores:
        m = jnp.maximum(m, jnp.max(s, axis=1, keepdims=True))
    p_own = jnp.exp(s_own - m)
    ps = [jnp.exp(s - m) for s in scores]
    l = p_own
    for p in ps:
        l = l + jnp.sum(p, axis=1, keepdims=True)
    inv = 1.0 / l
    pn = [_round_bf16(p * inv) for p in ps]
    pn_own = _round_bf16(p_own * inv)
    vn = _round_bf16(vn_ref[0])
    for h in range(H):
        p8 = jnp.broadcast_to(jnp.concatenate([p[h:h + 1, :] for p in pn], axis=1), (8, n_sel * page)).astype(BF16)
        vt = jnp.concatenate([vbuf[slot, h * n_sel + r] for r in range(n_sel)], axis=1).astype(BF16)
        o_ref[0, h:h + 1, :] = _dot_nt(p8, vt)[0:1, :] + pn_own[h:h + 1, :] * vn[h:h + 1, :]


def _attn_sample(q, k_new, v_new, ck, cv, page_table, top, slopes_page):
    n_pool, H, Dh, page = ck.shape
    S, n_pages = page_table.shape
    ppb = MOBA_BLOCK // page
    n_sel = MOBA_TOPK * ppb
    past_len = n_pages * page

    col = pl.BlockSpec((1, H, Dh), lambda s, pt, tp: (s, 0, 0))
    cols = lambda a: a.reshape(S, H, Dh)
    hbm = pl.BlockSpec(memory_space=pl.ANY)
    tiles = pltpu.VMEM((2, H * n_sel, Dh, page), F32)
    out = pl.pallas_call(
        functools.partial(_attn_sample_kernel, n_sel=n_sel, n_pages=n_pages, page=page, past_len=past_len),
        grid_spec=pltpu.PrefetchScalarGridSpec(
            num_scalar_prefetch=2,
            grid=(S,),
            in_specs=[col, col, col, pl.BlockSpec((H, page), lambda s, pt, tp: (0, 0)), hbm, hbm],
            out_specs=col,
            scratch_shapes=[tiles, tiles, pltpu.SemaphoreType.DMA((2, 2))],
        ),
        out_shape=jax.ShapeDtypeStruct((S, H, Dh), F32),
        compiler_params=_cparams(1),
        name="attn_sample",
    )(page_table.reshape(-1), top.reshape(-1), cols(q), cols(k_new), cols(v_new), slopes_page, ck, cv)
    return out.reshape(S, H * Dh)


def _ssm_sample_kernel(u_ref, sre_ref, sim_ref, are_ref, aim_ref, f_ref, e_ref, lag_ref, y_ref, nre_ref, nim_ref):
    SW = GROUPS_PER_SLAB * SSM_STATE
    for s in range(SLABS):
        cs = slice(s * SW, (s + 1) * SW)
        ls = slice(s * LANES, (s + 1) * LANES)
        ub = u_ref[:, ls].astype(BF16)
        s0r, s0i = sre_ref[:, cs], sim_ref[:, cs]
        ar, ai = are_ref[:, cs], aim_ref[:, cs]
        b = _dot(ub, f_ref[s])
        nre_ref[:, cs] = ar * s0r - ai * s0i + b[:, :SW]
        nim_ref[:, cs] = ar * s0i + ai * s0r + b[:, SW:]
        s0 = jnp.concatenate([s0r, s0i], axis=1).astype(BF16)
        y_ref[:, ls] = _dot(s0, e_ref[s, 0, :, :LANES]) + _dot(ub, lag_ref[s, :, :LANES])


def _ssm_sample(u, s_re, s_im, tb, lag, f, e):
    S = u.shape[0]
    GN = N_SSM_GROUPS * SSM_STATE
    st = jax.ShapeDtypeStruct((S, GN), F32)
    whole = lambda a: pl.BlockSpec(a.shape, lambda i: (0,) * a.ndim)
    args = (u, s_re.reshape(S, GN), s_im.reshape(S, GN), tb["abar_re"].reshape(1, GN), tb["abar_im"].reshape(1, GN))
    return pl.pallas_call(
        _ssm_sample_kernel,
        grid=(1,),
        in_specs=[whole(a) for a in args]
        + [pl.BlockSpec((SLABS, LANES, f.shape[2]), lambda i: (0, SSM_CHUNK - 1, 0)),
           pl.BlockSpec((SLABS, 1) + e.shape[2:], lambda i: (0, 0, 0, 0)),
           pl.BlockSpec((SLABS, LANES, lag.shape[2]), lambda i: (0, 1, 0))],
        out_specs=[pl.BlockSpec((S, SSM_WIDTH), lambda i: (0, 0)), pl.BlockSpec((S, GN), lambda i: (0, 0)),
                   pl.BlockSpec((S, GN), lambda i: (0, 0))],
        out_shape=[jax.ShapeDtypeStruct((S, SSM_WIDTH), F32), st, st],
        compiler_params=_cparams(1, V7X_VMEM_LIMIT),
        name="ssm_sample",
    )(*args, f, e, lag)


def _layer(x, xs, cache_k, cache_v, page_table, s_re, s_im, conv_buf, lw):
    S = xs.shape[0]
    proj = _inproj_sample(xs, lw["g_mix_pre"], lw["w_in"])
    q = proj[:, :ATTN_WIDTH]
    k = proj[:, ATTN_WIDTH:2 * ATTN_WIDTH]
    v = proj[:, 2 * ATTN_WIDTH:3 * ATTN_WIDTH]
    u = proj[:, 3 * ATTN_WIDTH:]
    ck = jnp.transpose(cache_k, (0, 2, 3, 1))
    cv = jnp.transpose(cache_v, (0, 2, 3, 1))

    kt, vt, u4, qa, ka, va, sel, qn2, kn2 = _inproj_prompt(x, lw["g_mix_pre"], lw["w_in"], lw["k_aug"])
    keep = _alibi_keep_blocks(lw["slopes"], qn2[:, 0], kn2[:, 0], ka.shape[1])
    attn2, gates = _attn_prompt(keep, qa, ka, va, sel, lw["slopes_q"], q, ck, page_table)
    y4, f_re, f_im = _ssm_prompt(u4, *lw["ssm_chunk"])
    out, conv = _ffn_prompt(x, attn2, y4, lw["ffn"])
    prompt = (out, kt, vt, f_re.reshape(N_SSM_GROUPS, SSM_STATE), f_im.reshape(N_SSM_GROUPS, SSM_STATE),
              conv[8 - (CONV_W - 1):])

    attn = _attn_sample(q, k, v, ck, cv, page_table, _top_sample(gates), lw["slopes_page"])
    y, n_re, n_im = _ssm_sample(u, s_re, s_im, lw["ssm_tb"], *lw["ssm_chunk"][:3])
    outs, g = _ffn_sample(xs, attn.astype(BF16), y, conv_buf[:, 0], conv_buf[:, 1], lw["ffn"])
    conv_new = jnp.stack([conv_buf[:, 1], g], axis=1)
    sample = (outs, k, v, n_re.reshape(S, N_SSM_GROUPS, SSM_STATE), n_im.reshape(S, N_SSM_GROUPS, SSM_STATE),
              conv_new)
    return prompt, sample


def _alibi_key_table(slopes):
    off = jnp.arange(MOBA_BLOCK, dtype=F32)[None, :] * (slopes * LOG2E)[:, None]
    to_bf16 = lambda a: lax.reduce_precision(a, exponent_bits=8, mantissa_bits=7)
    t0 = to_bf16(off)
    t1 = to_bf16(off - t0)
    t2 = to_bf16(off - t0 - t1)
    terms = jnp.stack([t0, t1, t2], axis=-1).astype(BF16)
    half = jnp.pad(terms, ((0, 0), (0, 0), (0, HEAD_DIM - 3)))
    zero = jnp.zeros_like(half)
    odd = (jnp.arange(N_HEADS) % 2 == 1)[:, None, None]
    return jnp.where(odd, jnp.concatenate([half, zero], axis=-1), jnp.concatenate([zero, half], axis=-1))


def kernel(x_prompt, x_sample, cache_k, cache_v, page_table, state_ssm_re, state_ssm_im, state_conv,
           norm_mix_pre, norm_mix_post, w_in, ssm_a_re, ssm_a_im, ssm_log_step, ssm_b_re, ssm_b_im,
           ssm_c_re, ssm_c_im, ssm_d, w_glu, b_glu, w_out, norm_ffn_pre, norm_ffn_post,
           w_gate, w_up, conv_w, conv_b, w_down):
    depth = w_in.shape[0]
    bp, lp_len = x_prompt.shape[:2]
    bs, ls_len = x_sample.shape[:2]
    page = cache_k.shape[2]
    assert bp == 1 and ls_len == 1 and lp_len % ATTN_TQ == 0 and page == LANES
    slopes = jnp.exp2(-8.0 * jnp.arange(1, N_HEADS + 1, dtype=F32) / N_HEADS)
    hp = x_prompt[0]
    hs = x_sample[:, 0]
    outs = [[] for _ in range(10)]
    for l in range(depth):
        tb = _ssm_tables(ssm_a_re[l], ssm_a_im[l], ssm_log_step[l], ssm_b_re[l], ssm_b_im[l])
        row = lambda a: a[l].reshape(1, -1)
        lw = dict(
            g_mix_pre=row(norm_mix_pre), w_in=w_in[l].astype(BF16),
            k_aug=_alibi_key_table(slopes),
            slopes=slopes,
            slopes_q=jnp.broadcast_to((slopes * LOG2E)[:, None, None], (N_HEADS, 1, MOBA_BLOCK)),
            slopes_page=jnp.broadcast_to(slopes[:, None], (N_HEADS, page)),
            ssm_tb=tb,
            ssm_chunk=_ssm_chunk_tables(tb, ssm_c_re[l], ssm_c_im[l], ssm_d[l]),
            ffn=[w_glu[l].astype(BF16), row(b_glu), w_out[l, :ATTN_WIDTH].astype(BF16),
                 w_out[l, ATTN_WIDTH:].astype(BF16), row(norm_mix_post), row(norm_ffn_pre),
                 w_gate[l].astype(BF16), w_up[l].astype(BF16), conv_w[l], row(conv_b),
                 w_down[l].astype(BF16), row(norm_ffn_post)],
        )
        (hp, ktp, vtp, sr, si, cp), (hs, ks, vs, srs, sis, cs) = _layer(
            hp, hs, cache_k[l], cache_v[l], page_table, state_ssm_re[l], state_ssm_im[l], state_conv[l], lw)
        outs[0].append(jnp.transpose(ktp, (2, 0, 1))[None])
        outs[1].append(jnp.transpose(vtp, (2, 0, 1))[None])
        outs[4].append(sr[None])
        outs[5].append(si[None])
        outs[8].append(cp[None])
        sr, si = srs, sis
        outs[2].append(ks.reshape(bs, ls_len, N_HEADS, HEAD_DIM))
        outs[3].append(vs.reshape(bs, ls_len, N_HEADS, HEAD_DIM))
        outs[6].append(sr)
        outs[7].append(si)
        outs[9].append(cs)
    return (hp[None], hs[:, None], *[jnp.stack(o) for o in outs])
```

```python
import functools
import math

import jax
import jax.numpy as jnp
from jax import lax
from jax.experimental import pallas as pl
from jax.experimental.pallas import tpu as pltpu

F32 = jnp.float32
BF16 = jnp.bfloat16

D_MODEL = 1024
N_HEADS = 8
HEAD_DIM = 64
ATTN_WIDTH = N_HEADS * HEAD_DIM
SSM_WIDTH = D_MODEL - ATTN_WIDTH
MOBA_BLOCK = 256
MOBA_TOPK = 3
SSM_GROUP = 16
N_SSM_GROUPS = SSM_WIDTH // SSM_GROUP
SSM_STATE = 64
SSM_CHUNK = 16
CONV_W = 3
RMS_EPS = 1e-6
NEG = -1e30
LOG2E = 1.4426950408889634
LANES = 128
V7X_VMEM_LIMIT = 56 * 1024 * 1024
HI = lax.Precision.HIGHEST

ROW_TILE = 512
SELECT_GROUPS = 4
V_ROWS = 80
SLABS = SSM_WIDTH // LANES
GROUPS_PER_SLAB = LANES // SSM_GROUP


def _cparams(n_axes, vmem=None):
    return pltpu.CompilerParams(dimension_semantics=("arbitrary",) * n_axes, vmem_limit_bytes=vmem)


def _rms(x, g):
    return x * lax.rsqrt(jnp.mean(x * x, axis=-1, keepdims=True) + RMS_EPS) * g


def _gelu_tanh(x):
    return 0.5 * x * (1.0 + jnp.tanh(math.sqrt(2.0 / math.pi) * (x + 0.044715 * (x * x * x))))


def _sigmoid(x):
    return 1.0 / (1.0 + jnp.exp(-x))


def _split_bf16(a):
    hi = a.astype(BF16)
    lo = (a - hi.astype(F32)).astype(BF16)
    return hi, lo


def _round_bf16(a):
    return a.astype(BF16).astype(F32)


def _dot(a, b):
    return jnp.dot(a, b, preferred_element_type=F32)


def _dot_nt(a, b):
    return lax.dot_general(a, b, (((1,), (1,)), ((), ())), preferred_element_type=F32)


def _top_blocks(cur, blk, n, axis=0):
    picks = []
    for _ in range(MOBA_TOPK):
        mx = jnp.max(cur, axis=axis, keepdims=True)
        first = jnp.min(jnp.where(cur == mx, blk, float(n)), axis=axis, keepdims=True)
        picks.append((first, mx))
        cur = jnp.where(blk == first, NEG, cur)
    return picks


def _inproj_prompt_kernel(x_ref, g_ref, w_ref, aug_ref, kt_ref, vt_ref, u_ref, qa_ref, ka_ref, va_ref, sel_ref,
                          qn_ref, kn_ref, kmean_s, *, tl, nb):
    i = pl.program_id(0)
    bpt = tl // MOBA_BLOCK
    H, Dh, B = N_HEADS, HEAD_DIM, MOBA_BLOCK

    @pl.when(i == 0)
    def _():
        kmean_s[...] = jnp.zeros_like(kmean_s)
        qn_ref[...] = jnp.zeros_like(qn_ref)
        kn_ref[...] = jnp.zeros_like(kn_ref)

    h = _rms(x_ref[...], g_ref[...]).astype(BF16)
    proj = _dot(h, w_ref[...])
    q = proj[:, :ATTN_WIDTH]
    k = proj[:, ATTN_WIDTH:2 * ATTN_WIDTH]
    v = proj[:, 2 * ATTN_WIDTH:3 * ATTN_WIDTH]
    for s in range(SLABS):
        u_ref[s] = proj[:, 3 * ATTN_WIDTH + s * LANES:3 * ATTN_WIDTH + (s + 1) * LANES]

    kt = k.T
    vt = v.T
    qt = (q * (Dh ** -0.5 * LOG2E)).T.astype(BF16)
    kt_ref[...] = kt.reshape(H, Dh, tl)
    vt_ref[...] = vt.reshape(H, Dh, tl)

    def head_norm2(t):
        t = t.astype(F32)
        n2 = jnp.sum((t * t).reshape(H, Dh, tl), axis=1)
        return jnp.broadcast_to(jnp.max(n2, axis=1, keepdims=True), (H, LANES))

    qn_ref[...] = jnp.maximum(qn_ref[...], head_norm2(qt))
    kn_ref[...] = jnp.maximum(kn_ref[...], head_norm2(kt))

    sub = lax.broadcasted_iota(jnp.int32, (Dh, tl), 0)
    ones3 = jnp.where(sub < 3, 1.0, 0.0).astype(BF16)
    kb = k.astype(BF16)
    lane_hi = lax.broadcasted_iota(jnp.int32, (B, LANES), 1) >= Dh
    vtb = vt.astype(BF16)
    ones_rows = jnp.ones((V_ROWS - Dh, B), BF16)
    for hd in range(H):
        qh = qt[hd * Dh:(hd + 1) * Dh, :]
        odd = hd % 2 == 1
        qa_ref[hd] = jnp.concatenate([ones3, qh] if odd else [qh, ones3], axis=0)
        for b in range(bpt):
            slab = kb[b * B:(b + 1) * B, (hd // 2) * LANES:(hd // 2 + 1) * LANES]
            ka_ref[hd, b] = jnp.where(lane_hi == odd, slab, aug_ref[hd])
            va_ref[hd, b] = jnp.concatenate([vtb[hd * Dh:(hd + 1) * Dh, b * B:(b + 1) * B], ones_rows], axis=0)

    row = lax.broadcasted_iota(jnp.int32, kmean_s.shape, 0)
    km = kmean_s[...]
    for b in range(bpt):
        kmb = jnp.mean(k[b * B:(b + 1) * B, :], axis=0, keepdims=True)
        km = jnp.where(row == i * bpt + b, kmb, km)
    kmean_s[...] = km

    own = ((i * tl + lax.broadcasted_iota(jnp.int32, (1, tl), 1)) // B).astype(F32)
    kmb16 = km.astype(BF16)
    qb16 = q.astype(BF16)
    group = max(8, nb // SELECT_GROUPS)

    def select(rows):
        blk = lax.broadcasted_iota(jnp.int32, (rows, tl), 0).astype(F32)
        for hd in range(H):
            sl = slice(hd * Dh, (hd + 1) * Dh)
            gate = _dot_nt(kmb16[:rows, sl], qb16[:, sl])
            chosen = jnp.zeros((rows, tl), F32)
            for first, mx in _top_blocks(jnp.where(blk < own, gate, NEG), blk, rows):
                chosen = jnp.where((blk == first) & (mx > 0.5 * NEG), 1.0, chosen)
            sel_ref[hd, :rows, :] = jnp.where(chosen > 0.5, 0.0, NEG)
            if rows < nb:
                sel_ref[hd, rows:, :] = jnp.full((nb - rows, tl), NEG, F32)

    last_candidate = i * bpt + (bpt - 2)
    for gi in range(nb // group):
        pl.when(jnp.minimum(last_candidate // group, nb // group - 1) == gi)(
            functools.partial(select, group * (gi + 1)))


def _inproj_prompt(x, g, w_bf, aug):
    L = x.shape[0]
    tl = ROW_TILE
    nb = L // MOBA_BLOCK
    bpt = tl // MOBA_BLOCK
    H, Dh, B = N_HEADS, HEAD_DIM, MOBA_BLOCK
    tcol = pl.BlockSpec((H, Dh, tl), lambda i: (0, 0, i))
    return pl.pallas_call(
        functools.partial(_inproj_prompt_kernel, tl=tl, nb=nb),
        grid=(L // tl,),
        in_specs=[pl.BlockSpec((tl, D_MODEL), lambda i: (i, 0)),
                  pl.BlockSpec((1, D_MODEL), lambda i: (0, 0)),
                  pl.BlockSpec((D_MODEL, 4 * ATTN_WIDTH), lambda i: (0, 0)),
                  pl.BlockSpec((H, B, LANES), lambda i: (0, 0, 0))],
        out_specs=[tcol, tcol,
                   pl.BlockSpec((SLABS, tl, LANES), lambda i: (0, i, 0)),
                   pl.BlockSpec((H, 2 * Dh, tl), lambda i: (0, 0, i)),
                   pl.BlockSpec((H, bpt, B, LANES), lambda i: (0, i, 0, 0)),
                   pl.BlockSpec((H, bpt, V_ROWS, B), lambda i: (0, i, 0, 0)),
                   pl.BlockSpec((H, nb, tl), lambda i: (0, 0, i)),
                   pl.BlockSpec((H, LANES), lambda i: (0, 0)),
                   pl.BlockSpec((H, LANES), lambda i: (0, 0))],
        out_shape=[jax.ShapeDtypeStruct((H, Dh, L), F32), jax.ShapeDtypeStruct((H, Dh, L), F32),
                   jax.ShapeDtypeStruct((SLABS, L, LANES), F32),
                   jax.ShapeDtypeStruct((H, 2 * Dh, L), BF16),
                   jax.ShapeDtypeStruct((H, nb, B, LANES), BF16),
                   jax.ShapeDtypeStruct((H, nb, V_ROWS, B), BF16),
                   jax.ShapeDtypeStruct((H, nb, L), F32),
                   jax.ShapeDtypeStruct((H, LANES), F32), jax.ShapeDtypeStruct((H, LANES), F32)],
        scratch_shapes=[pltpu.VMEM((nb, ATTN_WIDTH), F32)],
        compiler_params=_cparams(1, V7X_VMEM_LIMIT),
        name="inproj_prompt",
    )(x, g, w_bf, aug)


HEADS_PER_STEP = 2
ATTN_TQ = 1024
UNDERFLOW_LOG2 = 154.0


def _block_gates(page_of, n_pages, ppb, qb, ind):
    lane = lax.broadcasted_iota(jnp.int32, (ATTN_WIDTH, LANES), 1)
    prods = jnp.zeros((ATTN_WIDTH, LANES), F32)
    for b in range(n_pages // ppb):
        tot = page_of(b * ppb)
        for r in range(1, ppb):
            tot = tot + page_of(b * ppb + r)
        ksum = jnp.sum(tot, axis=1, keepdims=True)
        prods = jnp.where(lane == b, _round_bf16(ksum) * qb, prods)
    p_hi, p_lo = _split_bf16(prods)
    return (_dot(ind, p_hi) + _dot(ind, p_lo)) * (1.0 / MOBA_BLOCK)


def _attn_prompt_kernel(w_ref, pt_ref, qa_ref, ka_ref, va_ref, sel_ref, slope_ref, qs_ref, ind_ref, ck_ref,
                        o_ref, gate_ref, s_scr, d_scr, acc_scr, pbuf, psem, *, tq, pps, ppb):
    hp = pl.program_id(0)
    qi = pl.program_id(1)
    step_id = hp * pl.num_programs(1) + qi
    n_steps = pl.num_programs(0) * pl.num_programs(1)
    slot = step_id % 2

    def page_copies(step, slot):
        return [pltpu.make_async_copy(ck_ref.at[pt_ref[step * pps + r]], pbuf.at[slot, r], psem.at[slot])
                for r in range(pps)]

    @pl.when(step_id == 0)
    def _():
        for c in page_copies(step_id, slot):
            c.start()

    @pl.when(step_id + 1 < n_steps)
    def _():
        for c in page_copies(step_id + 1, 1 - slot):
            c.start()

    for c in page_copies(step_id, slot):
        c.wait()

    B, Dh = MOBA_BLOCK, HEAD_DIM
    bpq = tq // B
    units = [(e, cb) for e in range(HEADS_PER_STEP) for cb in range(bpq)]
    n_off = qi * bpq + (bpq - 1)
    keep = w_ref[hp * HEADS_PER_STEP]
    for e in range(1, HEADS_PER_STEP):
        keep = jnp.maximum(keep, w_ref[hp * HEADS_PER_STEP + e])
    j_start = jnp.maximum(qi * bpq - keep, 0)
    lane = lax.broadcasted_iota(jnp.int32, (1, B), 1).astype(F32)
    causal = lax.broadcasted_iota(jnp.int32, (B, B), 0) <= lax.broadcasted_iota(jnp.int32, (B, B), 1)

    def q_of(e, cb):
        return qa_ref[e, :, cb * B:(cb + 1) * B]

    def slope_of(e):
        return slope_ref[e]

    for u, (e, cb) in enumerate(units):
        d_scr[u] = _dot(ka_ref[e, qi * bpq + cb], q_of(e, cb))
        s_scr[u] = _dot(ka_ref[e, j_start], q_of(e, cb))

    gate_ref[0] = _block_gates(lambda r: pbuf[slot, r].reshape(ATTN_WIDTH, -1), pps, ppb,
                               _round_bf16(qs_ref[0]), ind_ref[...])

    ms = []
    for u, (e, cb) in enumerate(units):
        s = jnp.where(causal, d_scr[u] - slope_of(e) * lane, NEG)
        m = jnp.max(s, axis=0, keepdims=True)
        acc_scr[u] = _dot(va_ref[e, qi * bpq + cb], jnp.exp2(s - m).astype(BF16))
        ms.append(m)

    def step(j, ms):
        nxt = jnp.minimum(j + 1, n_off)
        out = []
        for u, (e, cb) in enumerate(units):
            s = s_scr[u]
            dist = lane + ((qi * bpq + cb - j) * B).astype(F32)
            col = sel_ref[e, pl.ds(j, 1), cb * B:(cb + 1) * B] - slope_of(e) * dist
            m_new = jnp.maximum(ms[u], jnp.max(s, axis=0, keepdims=True) + col)
            p = jnp.exp2(s - (m_new - col)).astype(BF16)
            acc_scr[u] = jnp.exp2(ms[u] - m_new) * acc_scr[u] + _dot(va_ref[e, j], p)
            s_scr[u] = _dot(ka_ref[e, nxt], q_of(e, cb))
            out.append(m_new)
        return tuple(out)

    def body(t, ms):
        j = j_start + 2 * t
        return step(j + 1, step(j, ms))

    lax.fori_loop(0, (n_off - j_start + 1) // 2, body, tuple(ms))
    for cb in range(bpq):
        outs = []
        for e in range(HEADS_PER_STEP):
            acc = acc_scr[e * bpq + cb]
            outs.append((acc[:Dh, :] / acc[Dh:Dh + 1, :]).T)
        o_ref[0, cb * B:(cb + 1) * B, :] = jnp.concatenate(outs, axis=1).astype(o_ref.dtype)


def _alibi_keep_blocks(slopes, qn2, kn2, nb):
    qk = jnp.sqrt(qn2 * kn2) * 1.02
    need = (2.0 * qk + UNDERFLOW_LOG2) / (slopes * LOG2E)
    w = jnp.ceil((need - 1.0) / MOBA_BLOCK)
    return jnp.clip(w, 1.0, float(nb)).astype(jnp.int32)


def _attn_prompt(keep, qa, ka, va, sel, slopes, q_sample, ck, page_table):
    H, nb, B, _ = ka.shape
    L = qa.shape[2]
    tq = ATTN_TQ
    hp = HEADS_PER_STEP
    nq = L // tq
    n_steps = (H // hp) * nq
    S, n_pages = page_table.shape
    page = ck.shape[3]
    ppb = MOBA_BLOCK // page
    pps = (S * n_pages) // n_steps
    assert pps * n_steps == S * n_pages and n_pages % pps == 0 and pps % ppb == 0 and pps // ppb <= LANES
    W = H * HEAD_DIM
    ind = (jnp.arange(H)[:, None] == jnp.arange(W)[None, :] // HEAD_DIM).astype(BF16)
    q_rep = jnp.broadcast_to(q_sample.reshape(S, W, 1), (S, W, LANES))
    once = pl.Buffered(1)
    attn, gates = pl.pallas_call(
        functools.partial(_attn_prompt_kernel, tq=tq, pps=pps, ppb=ppb),
        grid_spec=pltpu.PrefetchScalarGridSpec(
            num_scalar_prefetch=2,
            grid=(H // hp, nq),
            in_specs=[pl.BlockSpec((hp, 2 * HEAD_DIM, tq), lambda h, i, w, pt: (h, 0, i)),
                      pl.BlockSpec((hp, nb, B, LANES), lambda h, i, w, pt: (h, 0, 0, 0), pipeline_mode=once),
                      pl.BlockSpec((hp, nb, V_ROWS, B), lambda h, i, w, pt: (h, 0, 0, 0), pipeline_mode=once),
                      pl.BlockSpec((hp, nb, tq), lambda h, i, w, pt: (h, 0, i)),
                      pl.BlockSpec((hp, 1, B), lambda h, i, w, pt: (h, 0, 0)),
                      pl.BlockSpec((1, W, LANES), lambda h, i, w, pt: (((h * nq + i) * pps) // n_pages, 0, 0)),
                      pl.BlockSpec((H, W), lambda h, i, w, pt: (0, 0)),
                      pl.BlockSpec(memory_space=pl.ANY)],
            out_specs=[pl.BlockSpec((1, tq, hp * HEAD_DIM), lambda h, i, w, pt: (h, i, 0)),
                       pl.BlockSpec((1, H, LANES), lambda h, i, w, pt: (h * nq + i, 0, 0))],
            scratch_shapes=[pltpu.VMEM((hp * tq // B, B, B), F32), pltpu.VMEM((hp * tq // B, B, B), F32),
                            pltpu.VMEM((hp * tq // B, V_ROWS, B), F32),
                            pltpu.VMEM((2, pps) + ck.shape[1:], F32), pltpu.SemaphoreType.DMA((2,))],
        ),
        out_shape=[jax.ShapeDtypeStruct((H // hp, L, hp * HEAD_DIM), BF16),
                   jax.ShapeDtypeStruct((n_steps, H, LANES), F32)],
        compiler_params=_cparams(2, V7X_VMEM_LIMIT),
        name="attn_prompt",
    )(keep, page_table.reshape(-1), qa, ka, va, sel, slopes, q_rep, ind, ck)
    bps = pps // ppb
    gates = jnp.transpose(gates[:, :, :bps], (0, 2, 1)).reshape(S, n_pages // ppb, H)
    return attn, gates


def _ssm_tables(a_re, a_im, log_step, b_re, b_im):
    T = SSM_CHUNK
    dt = jnp.exp(log_step)[:, None]
    j = jnp.arange(T + 1, dtype=F32)[:, None, None]
    mag = jnp.exp(a_re * dt * j)
    pw_re = mag * jnp.cos(a_im * dt * j)
    pw_im = mag * jnp.sin(a_im * dt * j)
    abar_re, abar_im = pw_re[1], pw_im[1]
    den = a_re * a_re + a_im * a_im
    nr = abar_re - 1.0
    ni = abar_im
    coef_re = (nr * a_re + ni * a_im) / den
    coef_im = (ni * a_re - nr * a_im) / den
    bb_re = coef_re[..., None] * b_re - coef_im[..., None] * b_im
    bb_im = coef_re[..., None] * b_im + coef_im[..., None] * b_re
    return dict(pw_re=pw_re, pw_im=pw_im, bb_re=bb_re, bb_im=bb_im, abar_re=abar_re, abar_im=abar_im)


def _spread_groups(compact, rows_per_group, cols_per_group, col_outer):
    gs = GROUPS_PER_SLAB
    rows = compact.shape[-2]
    src = jnp.arange(col_outer * cols_per_group)
    dst = jnp.arange(col_outer * gs * cols_per_group)
    same_outer = src[:, None] // cols_per_group == dst[None, :] // (gs * cols_per_group)
    same_c = src[:, None] % cols_per_group == dst[None, :] % cols_per_group
    rep = (same_outer & same_c).astype(BF16)
    row_group = (jnp.arange(rows) // rows_per_group) % gs
    col_group = (dst // cols_per_group) % gs
    wide = jnp.dot(compact, rep, preferred_element_type=F32)
    return jnp.where(row_group[:, None] == col_group[None, :], wide, 0.0).astype(BF16)


def _ssm_chunk_tables(tb, c_re, c_im, d):
    pw_re, pw_im, bb_re, bb_im = tb["pw_re"], tb["pw_im"], tb["bb_re"], tb["bb_im"]
    T = SSM_CHUNK
    G, N, P = bb_re.shape
    x_re = pw_re[:T, :, :, None] * bb_re[None] - pw_im[:T, :, :, None] * bb_im[None]
    x_im = pw_re[:T, :, :, None] * bb_im[None] + pw_im[:T, :, :, None] * bb_re[None]
    kj = (jnp.einsum("gpn,jgnq->jgqp", c_re, x_re, precision=HI)
          - jnp.einsum("gpn,jgnq->jgqp", c_im, x_im, precision=HI))
    kj = kj.at[0].add(jnp.eye(P, dtype=F32)[None] * d[:, :, None])
    gs = GROUPS_PER_SLAB
    kpad = jnp.concatenate([jnp.zeros_like(kj[:1]), kj], axis=0)
    kc = jnp.stack([kpad[:T], kpad[1:]], axis=3).reshape(T, SLABS, gs, P, 2, P)
    kc = jnp.transpose(kc, (1, 0, 2, 3, 4, 5)).reshape(SLABS, T * LANES, 2 * P)
    lag = _spread_groups(kc.astype(BF16), P, P, 2)
    xc = jnp.stack([x_re[::-1], x_im[::-1]], axis=2).reshape(T, SLABS, gs, 2, N, P)
    xc = jnp.transpose(xc, (1, 0, 2, 5, 3, 4)).reshape(SLABS, T * LANES, 2 * N)
    f = _spread_groups(xc.astype(BF16), P, N, 2)
    cr = jnp.transpose(c_re, (0, 2, 1))[None]
    ci = jnp.transpose(c_im, (0, 2, 1))[None]
    ar = pw_re[1:T + 1, :, :, None]
    ai = pw_im[1:T + 1, :, :, None]
    ec = jnp.stack([cr * ar - ci * ai, -(cr * ai + ci * ar)], axis=1)
    ec = ec.reshape(T // 2, 2, 2, SLABS, gs, N, P)
    ec = jnp.transpose(ec, (3, 0, 2, 4, 5, 1, 6)).reshape(SLABS, T // 2, 2 * gs * N, 2 * P)
    e = _spread_groups(ec.astype(BF16), N, P, 2)
    a16_re = pw_re[T].reshape(1, G * N)
    a16_im = pw_im[T].reshape(1, G * N)
    return lag, f, e, a16_re, a16_im


def _chunk_steps(u_ref, rows):
    return [u_ref[0, pl.ds(s, rows, stride=SSM_CHUNK), :].astype(BF16) for s in range(SSM_CHUNK)]


def _ssm_chunk_in_kernel(u_ref, f_ref, bre_ref, bim_ref, *, rows):
    b = _dot(jnp.concatenate(_chunk_steps(u_ref, rows), axis=1), f_ref[0])
    half = b.shape[1] // 2
    bre_ref[...] = b[:, :half]
    bim_ref[...] = b[:, half:]


def _ssm_scan_kernel(bre_ref, bim_ref, are_ref, aim_ref, sre_ref, sim_ref, fre_ref, fim_ref):
    nc = bre_ref.shape[0]
    ar = are_ref[...]
    ai = aim_ref[...]

    def body(c8, carry):
        sr, si = carry
        r0 = pl.multiple_of(c8 * 8, 8)
        br = bre_ref[pl.ds(r0, 8), :]
        bi = bim_ref[pl.ds(r0, 8), :]
        rows_r, rows_i = [], []
        for r in range(8):
            rows_r.append(sr)
            rows_i.append(si)
            sr, si = (ar * sr - ai * si + br[r:r + 1, :], ar * si + ai * sr + bi[r:r + 1, :])
        sre_ref[pl.ds(r0, 8), :] = jnp.concatenate(rows_r, axis=0)
        sim_ref[pl.ds(r0, 8), :] = jnp.concatenate(rows_i, axis=0)
        return sr, si

    z = jnp.zeros(are_ref.shape, F32)
    sr, si = lax.fori_loop(0, nc // 8, body, (z, z))
    fre_ref[...] = sr
    fim_ref[...] = si


def _ssm_chunk_out_kernel(u_ref, lag_ref, e_ref, sre_ref, sim_ref, y_ref, *, rows):
    us = _chunk_steps(u_ref, rows)
    s = jnp.concatenate([sre_ref[...], sim_ref[...]], axis=1).astype(BF16)
    for pair in range(SSM_CHUNK // 2):
        tau = 2 * pair
        lhs = jnp.concatenate(us[tau + 1::-1], axis=1)
        y2 = _dot(lhs, lag_ref[0, :LANES * (tau + 2), :]) + _dot(s, e_ref[0, pair])
        y_ref[0, pl.ds(tau, rows, stride=SSM_CHUNK), :] = y2[:, :LANES]
        y_ref[0, pl.ds(tau + 1, rows, stride=SSM_CHUNK), :] = y2[:, LANES:]


SSM_ROWS = 512


def _ssm_prompt(u4, lag, f, e, a16_re, a16_im):
    L = u4.shape[1]
    nc = L // SSM_CHUNK
    rows = min(SSM_ROWS, nc)
    GN = N_SSM_GROUPS * SSM_STATE
    SW = GROUPS_PER_SLAB * SSM_STATE
    st = jax.ShapeDtypeStruct((nc, GN), F32)
    slab_rows = pl.BlockSpec((1, rows * SSM_CHUNK, LANES), lambda s, r: (s, r, 0))
    state_cols = pl.BlockSpec((rows, SW), lambda s, r: (r, s))
    b_re, b_im = pl.pallas_call(
        functools.partial(_ssm_chunk_in_kernel, rows=rows),
        grid=(SLABS, nc // rows),
        in_specs=[slab_rows, pl.BlockSpec((1,) + f.shape[1:], lambda s, r: (s, 0, 0))],
        out_specs=[state_cols, state_cols],
        out_shape=[st, st],
        compiler_params=_cparams(2, V7X_VMEM_LIMIT),
        name="ssm_chunk_in",
    )(u4, f)
    fin = jax.ShapeDtypeStruct((1, GN), F32)
    s_re, s_im, f_re, f_im = pl.pallas_call(
        _ssm_scan_kernel,
        out_shape=[st, st, fin, fin],
        compiler_params=pltpu.CompilerParams(vmem_limit_bytes=V7X_VMEM_LIMIT),
        name="ssm_scan",
    )(b_re, b_im, a16_re, a16_im)
    y4 = pl.pallas_call(
        functools.partial(_ssm_chunk_out_kernel, rows=rows),
        grid=(SLABS, nc // rows),
        in_specs=[slab_rows,
                  pl.BlockSpec((1,) + lag.shape[1:], lambda s, r: (s, 0, 0)),
                  pl.BlockSpec((1,) + e.shape[1:], lambda s, r: (s, 0, 0, 0)),
                  state_cols, state_cols],
        out_specs=slab_rows,
        out_shape=jax.ShapeDtypeStruct(u4.shape, F32),
        compiler_params=_cparams(2, V7X_VMEM_LIMIT),
        name="ssm_chunk_out",
    )(u4, lag, e, s_re, s_im)
    return y4, f_re, f_im


FF_CHUNK = 256


def _mix_and_prenorm(x, attn_bf, y, wglu_ref, bglu_ref, wouta_ref, wouts_ref, gpost_ref, gpre_ref):
    z = _gelu_tanh(y)
    ssm = z * _sigmoid(_dot(z.astype(BF16), wglu_ref[...]) + bglu_ref[...])
    mix = _dot(attn_bf, wouta_ref[...]) + _dot(ssm.astype(BF16), wouts_ref[...])
    x1 = x + _rms(mix, gpost_ref[...])
    h2 = _rms(x1, gpre_ref[...]).astype(BF16)
    return x1, h2


def _ffn_prompt_kernel(x_ref, attn_ref, y_ref, wglu_ref, bglu_ref, wouta_ref, wouts_ref, gpost_ref, gpre_ref,
                       wgate_ref, wup_ref, cw_ref, cb_ref, wdown_ref, gfpost_ref, out_ref, conv_ref, tail_s, act_s,
                       *, tl, dff):
    i = pl.program_id(0)

    @pl.when(i == 0)
    def _():
        tail_s[...] = jnp.zeros_like(tail_s)

    attn = jnp.concatenate([attn_ref[s] for s in range(attn_ref.shape[0])], axis=1)
    y = jnp.concatenate([y_ref[s] for s in range(SLABS)], axis=1)
    x1, h2 = _mix_and_prenorm(x_ref[...], attn, y, wglu_ref, bglu_ref, wouta_ref, wouts_ref, gpost_ref, gpre_ref)
    row = lax.broadcasted_iota(jnp.int32, (tl, FF_CHUNK), 0)
    for c in range(dff // FF_CHUNK):
        cs = slice(c * FF_CHUNK, (c + 1) * FF_CHUNK)
        g = _dot(h2, wgate_ref[:, cs])
        up = _dot(h2, wup_ref[:, cs])
        tail = tail_s[c]
        p1 = tail[7:8, :]
        p2 = tail[6:7, :]
        g1 = jnp.where(row == 0, p1, pltpu.roll(g, 1, 0))
        g2 = jnp.where(row == 0, p2, jnp.where(row == 1, p1, pltpu.roll(g, 2, 0)))
        gc = cw_ref[0:1, cs] * g2 + cw_ref[1:2, cs] * g1 + cw_ref[2:3, cs] * g + cb_ref[:, cs]
        act_s[:, cs] = (_gelu_tanh(gc) * up).astype(BF16)
        tail_s[c] = g[tl - 8:, :]
        conv_ref[:, cs] = g[tl - 8:, :]
    f = _dot(act_s[...], wdown_ref[...])
    out_ref[...] = x1 + _rms(f, gfpost_ref[...])


def _ffn_sample_kernel(x_ref, attn_ref, y_ref, b0_ref, b1_ref, wglu_ref, bglu_ref, wouta_ref, wouts_ref,
                       gpost_ref, gpre_ref, wgate_ref, wup_ref, cw_ref, cb_ref, wdown_ref, gfpost_ref,
                       out_ref, g_ref, *, dff):
    x1, h2 = _mix_and_prenorm(x_ref[...], attn_ref[...], y_ref[...], wglu_ref, bglu_ref, wouta_ref,
                              wouts_ref, gpost_ref, gpre_ref)
    f = jnp.zeros(x1.shape, F32)
    for c in range(dff // FF_CHUNK):
        cs = slice(c * FF_CHUNK, (c + 1) * FF_CHUNK)
        g = _dot(h2, wgate_ref[:, cs])
        up = _dot(h2, wup_ref[:, cs])
        gc = (cw_ref[0:1, cs] * b0_ref[:, cs] + cw_ref[1:2, cs] * b1_ref[:, cs] + cw_ref[2:3, cs] * g
              + cb_ref[:, cs])
        act = (_gelu_tanh(gc) * up).astype(BF16)
        f = f + _dot(act, wdown_ref[cs, :])
        g_ref[:, cs] = g
    out_ref[...] = x1 + _rms(f, gfpost_ref[...])


def _weight_specs(dff):
    c2 = lambda *_: (0, 0)
    full = lambda r, c: pl.BlockSpec((r, c), c2, pipeline_mode=pl.Buffered(1))
    return [full(SSM_WIDTH, SSM_WIDTH), full(1, SSM_WIDTH), full(ATTN_WIDTH, D_MODEL), full(SSM_WIDTH, D_MODEL),
            full(1, D_MODEL), full(1, D_MODEL), full(D_MODEL, dff), full(D_MODEL, dff), full(CONV_W, dff),
            full(1, dff), full(dff, D_MODEL), full(1, D_MODEL)]


def _ffn_prompt(x, attn2, y4, weights):
    L = x.shape[0]
    tl = ROW_TILE
    dff = weights[6].shape[1]
    rows = lambda w: pl.BlockSpec((tl, w), lambda i: (i, 0))
    slabs = lambda a: pl.BlockSpec((a.shape[0], tl, LANES), lambda i: (0, i, 0))
    return pl.pallas_call(
        functools.partial(_ffn_prompt_kernel, tl=tl, dff=dff),
        grid=(L // tl,),
        in_specs=[rows(D_MODEL), slabs(attn2), slabs(y4)] + _weight_specs(dff),
        out_specs=[rows(D_MODEL), pl.BlockSpec((8, dff), lambda i: (0, 0))],
        out_shape=[jax.ShapeDtypeStruct((L, D_MODEL), F32), jax.ShapeDtypeStruct((8, dff), F32)],
        scratch_shapes=[pltpu.VMEM((dff // FF_CHUNK, 8, FF_CHUNK), F32), pltpu.VMEM((tl, dff), BF16)],
        compiler_params=_cparams(1, V7X_VMEM_LIMIT),
        name="ffn_prompt",
    )(x, attn2, y4, *weights)


def _ffn_sample(x, attn_bf, y, buf0, buf1, weights):
    nb = x.shape[0]
    dff = weights[6].shape[1]
    rows = lambda w: pl.BlockSpec((nb, w), lambda i: (0, 0))
    return pl.pallas_call(
        functools.partial(_ffn_sample_kernel, dff=dff),
        grid=(1,),
        in_specs=[rows(D_MODEL), rows(ATTN_WIDTH), rows(SSM_WIDTH), rows(dff), rows(dff)] + _weight_specs(dff),
        out_specs=[rows(D_MODEL), rows(dff)],
        out_shape=[jax.ShapeDtypeStruct((nb, D_MODEL), F32), jax.ShapeDtypeStruct((nb, dff), F32)],
        compiler_params=_cparams(1, V7X_VMEM_LIMIT),
        name="ffn_sample",
    )(x, attn_bf, y, buf0, buf1, *weights)


def _inproj_sample_kernel(x_ref, g_ref, w_ref, o_ref):
    o_ref[...] = _dot(_rms(x_ref[...], g_ref[...]).astype(BF16), w_ref[...])


def _inproj_sample(x, g, w_bf):
    nb = x.shape[0]
    return pl.pallas_call(
        _inproj_sample_kernel,
        out_shape=jax.ShapeDtypeStruct((nb, w_bf.shape[1]), F32),
        compiler_params=pltpu.CompilerParams(vmem_limit_bytes=V7X_VMEM_LIMIT),
        name="inproj_sample",
    )(x, g, w_bf)


def _top_sample_kernel(g_ref, top_ref):
    gate = g_ref[...]
    nb = gate.shape[1]
    blk = lax.broadcasted_iota(jnp.int32, gate.shape, 1).astype(F32)
    picks = [first for first, _ in _top_blocks(gate, blk, nb, axis=1)]
    top_ref[...] = jnp.concatenate(picks, axis=1).astype(jnp.int32)


def _top_sample(gates):
    S, nb, H = gates.shape
    return pl.pallas_call(
        _top_sample_kernel,
        out_shape=jax.ShapeDtypeStruct((S, MOBA_TOPK, H), jnp.int32),
        name="top_sample",
    )(gates)


def _attn_sample_kernel(pt_ref, top_ref, q_ref, kn_ref, vn_ref, slope_ref, ck_ref, cv_ref, o_ref,
                        kbuf, vbuf, sems, *, n_sel, n_pages, page, past_len):
    H = N_HEADS
    s_i = pl.program_id(0)
    n_seq = pl.num_programs(0)
    scale = HEAD_DIM ** -0.5
    ppb = MOBA_BLOCK // page
    off = lax.broadcasted_iota(jnp.int32, (1, page), 1)

    def block_of(seq, h, r):
        return top_ref[(seq * MOBA_TOPK + r // ppb) * H + h]

    def copies(seq, slot):
        out = []
        for h in range(H):
            for r in range(n_sel):
                pg = pt_ref[seq * n_pages + block_of(seq, h, r) * ppb + r % ppb]
                out.append(pltpu.make_async_copy(ck_ref.at[pg, h], kbuf.at[slot, h * n_sel + r], sems.at[0, slot]))
                out.append(pltpu.make_async_copy(cv_ref.at[pg, h], vbuf.at[slot, h * n_sel + r], sems.at[1, slot]))
        return out

    slot = s_i % 2

    @pl.when(s_i == 0)
    def _():
        for c in copies(s_i, slot):
            c.start()

    @pl.when(s_i + 1 < n_seq)
    def _():
        for c in copies(s_i + 1, 1 - slot):
            c.start()

    for c in copies(s_i, slot):
        c.wait()

    q = q_ref[0]
    qk = []
    for h in range(H):
        q8 = jnp.broadcast_to(q[h:h + 1, :], (8, HEAD_DIM)).astype(BF16)
        kt = jnp.concatenate([kbuf[slot, h * n_sel + r] for r in range(n_sel)], axis=1).astype(BF16)
        qk.append(_dot(q8, kt)[0:1, :])
    scores = []
    for r in range(n_sel):
        dist_rows = [(past_len - (block_of(s_i, h, r) * MOBA_BLOCK + (r % ppb) * page + off)).astype(F32)
                     for h in range(H)]
        qk_r = jnp.concatenate([qk[h][:, r * page:(r + 1) * page] for h in range(H)], axis=0)
        scores.append(qk_r * scale - slope_ref[...] * jnp.concatenate(dist_rows, axis=0))
    s_own = jnp.sum(_round_bf16(q) * _round_bf16(kn_ref[0]), axis=1, keepdims=True) * scale
    m = s_own
    for s in scores:
        m = jnp.maximum(m, jnp.max(s, axis=1, keepdims=True))
    p_own = jnp.exp(s_own - m)
    ps = [jnp.exp(s - m) for s in scores]
    l = p_own
    for p in ps:
        l = l + jnp.sum(p, axis=1, keepdims=True)
    inv = 1.0 / l
    pn = [_round_bf16(p * inv) for p in ps]
    pn_own = _round_bf16(p_own * inv)
    vn = _round_bf16(vn_ref[0])
    for h in range(H):
        p8 = jnp.broadcast_to(jnp.concatenate([p[h:h + 1, :] for p in pn], axis=1), (8, n_sel * page)).astype(BF16)
        vt = jnp.concatenate([vbuf[slot, h * n_sel + r] for r in range(n_sel)], axis=1).astype(BF16)
        o_ref[0, h:h + 1, :] = _dot_nt(p8, vt)[0:1, :] + pn_own[h:h + 1, :] * vn[h:h + 1, :]


def _attn_sample(q, k_new, v_new, ck, cv, page_table, top, slopes_page):
    n_pool, H, Dh, page = ck.shape
    S, n_pages = page_table.shape
    ppb = MOBA_BLOCK // page
    n_sel = MOBA_TOPK * ppb
    past_len = n_pages * page

    col = pl.BlockSpec((1, H, Dh), lambda s, pt, tp: (s, 0, 0))
    cols = lambda a: a.reshape(S, H, Dh)
    hbm = pl.BlockSpec(memory_space=pl.ANY)
    tiles = pltpu.VMEM((2, H * n_sel, Dh, page), F32)
    out = pl.pallas_call(
        functools.partial(_attn_sample_kernel, n_sel=n_sel, n_pages=n_pages, page=page, past_len=past_len),
        grid_spec=pltpu.PrefetchScalarGridSpec(
            num_scalar_prefetch=2,
            grid=(S,),
            in_specs=[col, col, col, pl.BlockSpec((H, page), lambda s, pt, tp: (0, 0)), hbm, hbm],
            out_specs=col,
            scratch_shapes=[tiles, tiles, pltpu.SemaphoreType.DMA((2, 2))],
        ),
        out_shape=jax.ShapeDtypeStruct((S, H, Dh), F32),
        compiler_params=_cparams(1),
        name="attn_sample",
    )(page_table.reshape(-1), top.reshape(-1), cols(q), cols(k_new), cols(v_new), slopes_page, ck, cv)
    return out.reshape(S, H * Dh)


def _ssm_sample_kernel(u_ref, sre_ref, sim_ref, are_ref, aim_ref, f_ref, e_ref, lag_ref, y_ref, nre_ref, nim_ref):
    SW = GROUPS_PER_SLAB * SSM_STATE
    for s in range(SLABS):
        cs = slice(s * SW, (s + 1) * SW)
        ls = slice(s * LANES, (s + 1) * LANES)
        ub = u_ref[:, ls].astype(BF16)
        s0r, s0i = sre_ref[:, cs], sim_ref[:, cs]
        ar, ai = are_ref[:, cs], aim_ref[:, cs]
        b = _dot(ub, f_ref[s])
        nre_ref[:, cs] = ar * s0r - ai * s0i + b[:, :SW]
        nim_ref[:, cs] = ar * s0i + ai * s0r + b[:, SW:]
        s0 = jnp.concatenate([s0r, s0i], axis=1).astype(BF16)
        y_ref[:, ls] = _dot(s0, e_ref[s, 0, :, :LANES]) + _dot(ub, lag_ref[s, :, :LANES])


def _ssm_sample(u, s_re, s_im, tb, lag, f, e):
    S = u.shape[0]
    GN = N_SSM_GROUPS * SSM_STATE
    st = jax.ShapeDtypeStruct((S, GN), F32)
    whole = lambda a: pl.BlockSpec(a.shape, lambda i: (0,) * a.ndim)
    args = (u, s_re.reshape(S, GN), s_im.reshape(S, GN), tb["abar_re"].reshape(1, GN), tb["abar_im"].reshape(1, GN))
    return pl.pallas_call(
        _ssm_sample_kernel,
        grid=(1,),
        in_specs=[whole(a) for a in args]
        + [pl.BlockSpec((SLABS, LANES, f.shape[2]), lambda i: (0, SSM_CHUNK - 1, 0)),
           pl.BlockSpec((SLABS, 1) + e.shape[2:], lambda i: (0, 0, 0, 0)),
           pl.BlockSpec((SLABS, LANES, lag.shape[2]), lambda i: (0, 1, 0))],
        out_specs=[pl.BlockSpec((S, SSM_WIDTH), lambda i: (0, 0)), pl.BlockSpec((S, GN), lambda i: (0, 0)),
                   pl.BlockSpec((S, GN), lambda i: (0, 0))],
        out_shape=[jax.ShapeDtypeStruct((S, SSM_WIDTH), F32), st, st],
        compiler_params=_cparams(1, V7X_VMEM_LIMIT),
        name="ssm_sample",
    )(*args, f, e, lag)


def _layer(x, xs, cache_k, cache_v, page_table, s_re, s_im, conv_buf, lw):
    S = xs.shape[0]
    proj = _inproj_sample(xs, lw["g_mix_pre"], lw["w_in"])
    q = proj[:, :ATTN_WIDTH]
    k = proj[:, ATTN_WIDTH:2 * ATTN_WIDTH]
    v = proj[:, 2 * ATTN_WIDTH:3 * ATTN_WIDTH]
    u = proj[:, 3 * ATTN_WIDTH:]
    ck = jnp.transpose(cache_k, (0, 2, 3, 1))
    cv = jnp.transpose(cache_v, (0, 2, 3, 1))

    kt, vt, u4, qa, ka, va, sel, qn2, kn2 = _inproj_prompt(x, lw["g_mix_pre"], lw["w_in"], lw["k_aug"])
    keep = _alibi_keep_blocks(lw["slopes"], qn2[:, 0], kn2[:, 0], ka.shape[1])
    attn2, gates = _attn_prompt(keep, qa, ka, va, sel, lw["slopes_q"], q, ck, page_table)
    y4, f_re, f_im = _ssm_prompt(u4, *lw["ssm_chunk"])
    out, conv = _ffn_prompt(x, attn2, y4, lw["ffn"])
    prompt = (out, kt, vt, f_re.reshape(N_SSM_GROUPS, SSM_STATE), f_im.reshape(N_SSM_GROUPS, SSM_STATE),
              conv[8 - (CONV_W - 1):])

    attn = _attn_sample(q, k, v, ck, cv, page_table, _top_sample(gates), lw["slopes_page"])
    y, n_re, n_im = _ssm_sample(u, s_re, s_im, lw["ssm_tb"], *lw["ssm_chunk"][:3])
    outs, g = _ffn_sample(xs, attn.astype(BF16), y, conv_buf[:, 0], conv_buf[:, 1], lw["ffn"])
    conv_new = jnp.stack([conv_buf[:, 1], g], axis=1)
    sample = (outs, k, v, n_re.reshape(S, N_SSM_GROUPS, SSM_STATE), n_im.reshape(S, N_SSM_GROUPS, SSM_STATE),
              conv_new)
    return prompt, sample


def _alibi_key_table(slopes):
    off = jnp.arange(MOBA_BLOCK, dtype=F32)[None, :] * (slopes * LOG2E)[:, None]
    to_bf16 = lambda a: lax.reduce_precision(a, exponent_bits=8, mantissa_bits=7)
    t0 = to_bf16(off)
    t1 = to_bf16(off - t0)
    t2 = to_bf16(off - t0 - t1)
    terms = jnp.stack([t0, t1, t2], axis=-1).astype(BF16)
    half = jnp.pad(terms, ((0, 0), (0, 0), (0, HEAD_DIM - 3)))
    zero = jnp.zeros_like(half)
    odd = (jnp.arange(N_HEADS) % 2 == 1)[:, None, None]
    return jnp.where(odd, jnp.concatenate([half, zero], axis=-1), jnp.concatenate([zero, half], axis=-1))


def kernel(x_prompt, x_sample, cache_k, cache_v, page_table, state_ssm_re, state_ssm_im, state_conv,
           norm_mix_pre, norm_mix_post, w_in, ssm_a_re, ssm_a_im, ssm_log_step, ssm_b_re, ssm_b_im,
           ssm_c_re, ssm_c_im, ssm_d, w_glu, b_glu, w_out, norm_ffn_pre, norm_ffn_post,
           w_gate, w_up, conv_w, conv_b, w_down):
    depth = w_in.shape[0]
    bp, lp_len = x_prompt.shape[:2]
    bs, ls_len = x_sample.shape[:2]
    page = cache_k.shape[2]
    assert bp == 1 and ls_len == 1 and lp_len % ATTN_TQ == 0 and page == LANES
    slopes = jnp.exp2(-8.0 * jnp.arange(1, N_HEADS + 1, dtype=F32) / N_HEADS)
    hp = x_prompt[0]
    hs = x_sample[:, 0]
    outs = [[] for _ in range(10)]
    for l in range(depth):
        tb = _ssm_tables(ssm_a_re[l], ssm_a_im[l], ssm_log_step[l], ssm_b_re[l], ssm_b_im[l])
        row = lambda a: a[l].reshape(1, -1)
        lw = dict(
            g_mix_pre=row(norm_mix_pre), w_in=w_in[l].astype(BF16),
            k_aug=_alibi_key_table(slopes),
            slopes=slopes,
            slopes_q=jnp.broadcast_to((slopes * LOG2E)[:, None, None], (N_HEADS, 1, MOBA_BLOCK)),
            slopes_page=jnp.broadcast_to(slopes[:, None], (N_HEADS, page)),
            ssm_tb=tb,
            ssm_chunk=_ssm_chunk_tables(tb, ssm_c_re[l], ssm_c_im[l], ssm_d[l]),
            ffn=[w_glu[l].astype(BF16), row(b_glu), w_out[l, :ATTN_WIDTH].astype(BF16),
                 w_out[l, ATTN_WIDTH:].astype(BF16), row(norm_mix_post), row(norm_ffn_pre),
                 w_gate[l].astype(BF16), w_up[l].astype(BF16), conv_w[l], row(conv_b),
                 w_down[l].astype(BF16), row(norm_ffn_post)],
        )
        (hp, ktp, vtp, sr, si, cp), (hs, ks, vs, srs, sis, cs) = _layer(
            hp, hs, cache_k[l], cache_v[l], page_table, state_ssm_re[l], state_ssm_im[l], state_conv[l], lw)
        outs[0].append(jnp.transpose(ktp, (2, 0, 1))[None])
        outs[1].append(jnp.transpose(vtp, (2, 0, 1))[None])
        outs[4].append(sr[None])
        outs[5].append(si[None])
        outs[8].append(cp[None])
        sr, si = srs, sis
        outs[2].append(ks.reshape(bs, ls_len, N_HEADS, HEAD_DIM))
        outs[3].append(vs.reshape(bs, ls_len, N_HEADS, HEAD_DIM))
        outs[6].append(sr)
        outs[7].append(si)
        outs[9].append(cs)
    return (hp[None], hs[:, None], *[jnp.stack(o) for o in outs])
```

```python
import functools
import math

import jax
import jax.numpy as jnp
from jax import lax
from jax.experimental import pallas as pl
from jax.experimental.pallas import tpu as pltpu

F32 = jnp.float32
BF16 = jnp.bfloat16

D_MODEL = 1024
N_HEADS = 8
HEAD_DIM = 64
ATTN_WIDTH = N_HEADS * HEAD_DIM
SSM_WIDTH = D_MODEL - ATTN_WIDTH
MOBA_BLOCK = 256
MOBA_TOPK = 3
SSM_GROUP = 16
N_SSM_GROUPS = SSM_WIDTH // SSM_GROUP
SSM_STATE = 64
SSM_CHUNK = 16
CONV_W = 3
RMS_EPS = 1e-6
NEG = -1e30
LOG2E = 1.4426950408889634
LANES = 128
V7X_VMEM_LIMIT = 56 * 1024 * 1024
HI = lax.Precision.HIGHEST

ROW_TILE = 512
SELECT_GROUPS = 4
V_ROWS = 80
SLABS = SSM_WIDTH // LANES
GROUPS_PER_SLAB = LANES // SSM_GROUP


def _cparams(n_axes, vmem=None):
    return pltpu.CompilerParams(dimension_semantics=("arbitrary",) * n_axes, vmem_limit_bytes=vmem)


def _rms(x, g):
    return x * lax.rsqrt(jnp.mean(x * x, axis=-1, keepdims=True) + RMS_EPS) * g


def _gelu_tanh(x):
    return 0.5 * x * (1.0 + jnp.tanh(math.sqrt(2.0 / math.pi) * (x + 0.044715 * (x * x * x))))


def _sigmoid(x):
    return 1.0 / (1.0 + jnp.exp(-x))


def _split_bf16(a):
    hi = a.astype(BF16)
    lo = (a - hi.astype(F32)).astype(BF16)
    return hi, lo


def _round_bf16(a):
    return a.astype(BF16).astype(F32)


def _dot(a, b):
    return jnp.dot(a, b, preferred_element_type=F32)


def _dot_nt(a, b):
    return lax.dot_general(a, b, (((1,), (1,)), ((), ())), preferred_element_type=F32)


def _top_blocks(cur, blk, n, axis=0):
    picks = []
    for _ in range(MOBA_TOPK):
        mx = jnp.max(cur, axis=axis, keepdims=True)
        first = jnp.min(jnp.where(cur == mx, blk, float(n)), axis=axis, keepdims=True)
        picks.append((first, mx))
        cur = jnp.where(blk == first, NEG, cur)
    return picks


def _inproj_prompt_kernel(x_ref, g_ref, w_ref, aug_ref, kt_ref, vt_ref, u_ref, qa_ref, ka_ref, va_ref, sel_ref,
                          qn_ref, kn_ref, kmean_s, *, tl, nb):
    i = pl.program_id(0)
    bpt = tl // MOBA_BLOCK
    H, Dh, B = N_HEADS, HEAD_DIM, MOBA_BLOCK

    @pl.when(i == 0)
    def _():
        kmean_s[...] = jnp.zeros_like(kmean_s)
        qn_ref[...] = jnp.zeros_like(qn_ref)
        kn_ref[...] = jnp.zeros_like(kn_ref)

    h = _rms(x_ref[...], g_ref[...]).astype(BF16)
    proj = _dot(h, w_ref[...])
    q = proj[:, :ATTN_WIDTH]
    k = proj[:, ATTN_WIDTH:2 * ATTN_WIDTH]
    v = proj[:, 2 * ATTN_WIDTH:3 * ATTN_WIDTH]
    for s in range(SLABS):
        u_ref[s] = proj[:, 3 * ATTN_WIDTH + s * LANES:3 * ATTN_WIDTH + (s + 1) * LANES]

    kt = k.T
    vt = v.T
    qt = (q * (Dh ** -0.5 * LOG2E)).T.astype(BF16)
    kt_ref[...] = kt.reshape(H, Dh, tl)
    vt_ref[...] = vt.reshape(H, Dh, tl)

    def head_norm2(t):
        t = t.astype(F32)
        n2 = jnp.sum((t * t).reshape(H, Dh, tl), axis=1)
        return jnp.broadcast_to(jnp.max(n2, axis=1, keepdims=True), (H, LANES))

    qn_ref[...] = jnp.maximum(qn_ref[...], head_norm2(qt))
    kn_ref[...] = jnp.maximum(kn_ref[...], head_norm2(kt))

    sub = lax.broadcasted_iota(jnp.int32, (Dh, tl), 0)
    ones3 = jnp.where(sub < 3, 1.0, 0.0).astype(BF16)
    kb = k.astype(BF16)
    lane_hi = lax.broadcasted_iota(jnp.int32, (B, LANES), 1) >= Dh
    vtb = vt.astype(BF16)
    ones_rows = jnp.ones((V_ROWS - Dh, B), BF16)
    for hd in range(H):
        qh = qt[hd * Dh:(hd + 1) * Dh, :]
        odd = hd % 2 == 1
        qa_ref[hd] = jnp.concatenate([ones3, qh] if odd else [qh, ones3], axis=0)
        for b in range(bpt):
            slab = kb[b * B:(b + 1) * B, (hd // 2) * LANES:(hd // 2 + 1) * LANES]
            ka_ref[hd, b] = jnp.where(lane_hi == odd, slab, aug_ref[hd])
            va_ref[hd, b] = jnp.concatenate([vtb[hd * Dh:(hd + 1) * Dh, b * B:(b + 1) * B], ones_rows], axis=0)

    row = lax.broadcasted_iota(jnp.int32, kmean_s.shape, 0)
    km = kmean_s[...]
    for b in range(bpt):
        kmb = jnp.mean(k[b * B:(b + 1) * B, :], axis=0, keepdims=True)
        km = jnp.where(row == i * bpt + b, kmb, km)
    kmean_s[...] = km

    own = ((i * tl + lax.broadcasted_iota(jnp.int32, (1, tl), 1)) // B).astype(F32)
    kmb16 = km.astype(BF16)
    qb16 = q.astype(BF16)
    group = max(8, nb // SELECT_GROUPS)

    def select(rows):
        blk = lax.broadcasted_iota(jnp.int32, (rows, tl), 0).astype(F32)
        for hd in range(H):
            sl = slice(hd * Dh, (hd + 1) * Dh)
            gate = _dot_nt(kmb16[:rows, sl], qb16[:, sl])
            chosen = jnp.zeros((rows, tl), F32)
            for first, mx in _top_blocks(jnp.where(blk < own, gate, NEG), blk, rows):
                chosen = jnp.where((blk == first) & (mx > 0.5 * NEG), 1.0, chosen)
            sel_ref[hd, :rows, :] = jnp.where(chosen > 0.5, 0.0, NEG)
            if rows < nb:
                sel_ref[hd, rows:, :] = jnp.full((nb - rows, tl), NEG, F32)

    last_candidate = i * bpt + (bpt - 2)
    for gi in range(nb // group):
        pl.when(jnp.minimum(last_candidate // group, nb // group - 1) == gi)(
            functools.partial(select, group * (gi + 1)))


def _inproj_prompt(x, g, w_bf, aug):
    L = x.shape[0]
    tl = ROW_TILE
    nb = L // MOBA_BLOCK
    bpt = tl // MOBA_BLOCK
    H, Dh, B = N_HEADS, HEAD_DIM, MOBA_BLOCK
    tcol = pl.BlockSpec((H, Dh, tl), lambda i: (0, 0, i))
    return pl.pallas_call(
        functools.partial(_inproj_prompt_kernel, tl=tl, nb=nb),
        grid=(L // tl,),
        in_specs=[pl.BlockSpec((tl, D_MODEL), lambda i: (i, 0)),
                  pl.BlockSpec((1, D_MODEL), lambda i: (0, 0)),
                  pl.BlockSpec((D_MODEL, 4 * ATTN_WIDTH), lambda i: (0, 0)),
                  pl.BlockSpec((H, B, LANES), lambda i: (0, 0, 0))],
        out_specs=[tcol, tcol,
                   pl.BlockSpec((SLABS, tl, LANES), lambda i: (0, i, 0)),
                   pl.BlockSpec((H, 2 * Dh, tl), lambda i: (0, 0, i)),
                   pl.BlockSpec((H, bpt, B, LANES), lambda i: (0, i, 0, 0)),
                   pl.BlockSpec((H, bpt, V_ROWS, B), lambda i: (0, i, 0, 0)),
                   pl.BlockSpec((H, nb, tl), lambda i: (0, 0, i)),
                   pl.BlockSpec((H, LANES), lambda i: (0, 0)),
                   pl.BlockSpec((H, LANES), lambda i: (0, 0))],
        out_shape=[jax.ShapeDtypeStruct((H, Dh, L), F32), jax.ShapeDtypeStruct((H, Dh, L), F32),
                   jax.ShapeDtypeStruct((SLABS, L, LANES), F32),
                   jax.ShapeDtypeStruct((H, 2 * Dh, L), BF16),
                   jax.ShapeDtypeStruct((H, nb, B, LANES), BF16),
                   jax.ShapeDtypeStruct((H, nb, V_ROWS, B), BF16),
                   jax.ShapeDtypeStruct((H, nb, L), F32),
                   jax.ShapeDtypeStruct((H, LANES), F32), jax.ShapeDtypeStruct((H, LANES), F32)],
        scratch_shapes=[pltpu.VMEM((nb, ATTN_WIDTH), F32)],
        compiler_params=_cparams(1, V7X_VMEM_LIMIT),
        name="inproj_prompt",
    )(x, g, w_bf, aug)


HEADS_PER_STEP = 2
ATTN_TQ = 1024
UNDERFLOW_LOG2 = 154.0


def _block_gates(page_of, n_pages, ppb, qb, ind):
    lane = lax.broadcasted_iota(jnp.int32, (ATTN_WIDTH, LANES), 1)
    prods = jnp.zeros((ATTN_WIDTH, LANES), F32)
    for b in range(n_pages // ppb):
        tot = page_of(b * ppb)
        for r in range(1, ppb):
            tot = tot + page_of(b * ppb + r)
        ksum = jnp.sum(tot, axis=1, keepdims=True)
        prods = jnp.where(lane == b, _round_bf16(ksum) * qb, prods)
    p_hi, p_lo = _split_bf16(prods)
    return (_dot(ind, p_hi) + _dot(ind, p_lo)) * (1.0 / MOBA_BLOCK)


def _attn_prompt_kernel(w_ref, pt_ref, qa_ref, ka_ref, va_ref, sel_ref, slope_ref, qs_ref, ind_ref, ck_ref,
                        o_ref, gate_ref, s_scr, d_scr, acc_scr, pbuf, psem, *, tq, pps, ppb):
    hp = pl.program_id(0)
    qi = pl.program_id(1)
    step_id = hp * pl.num_programs(1) + qi
    n_steps = pl.num_programs(0) * pl.num_programs(1)
    slot = step_id % 2

    def page_copies(step, slot):
        return [pltpu.make_async_copy(ck_ref.at[pt_ref[step * pps + r]], pbuf.at[slot, r], psem.at[slot])
                for r in range(pps)]

    @pl.when(step_id == 0)
    def _():
        for c in page_copies(step_id, slot):
            c.start()

    @pl.when(step_id + 1 < n_steps)
    def _():
        for c in page_copies(step_id + 1, 1 - slot):
            c.start()

    for c in page_copies(step_id, slot):
        c.wait()

    B, Dh = MOBA_BLOCK, HEAD_DIM
    bpq = tq // B
    units = [(e, cb) for e in range(HEADS_PER_STEP) for cb in range(bpq)]
    n_off = qi * bpq + (bpq - 1)
    keep = w_ref[hp * HEADS_PER_STEP]
    for e in range(1, HEADS_PER_STEP):
        keep = jnp.maximum(keep, w_ref[hp * HEADS_PER_STEP + e])
    j_start = jnp.maximum(qi * bpq - keep, 0)
    lane = lax.broadcasted_iota(jnp.int32, (1, B), 1).astype(F32)
    causal = lax.broadcasted_iota(jnp.int32, (B, B), 0) <= lax.broadcasted_iota(jnp.int32, (B, B), 1)

    def q_of(e, cb):
        return qa_ref[e, :, cb * B:(cb + 1) * B]

    def slope_of(e):
        return slope_ref[e]

    for u, (e, cb) in enumerate(units):
        d_scr[u] = _dot(ka_ref[e, qi * bpq + cb], q_of(e, cb))
        s_scr[u] = _dot(ka_ref[e, j_start], q_of(e, cb))

    gate_ref[0] = _block_gates(lambda r: pbuf[slot, r].reshape(ATTN_WIDTH, -1), pps, ppb,
                               _round_bf16(qs_ref[0]), ind_ref[...])

    ms = []
    for u, (e, cb) in enumerate(units):
        s = jnp.where(causal, d_scr[u] - slope_of(e) * lane, NEG)
        m = jnp.max(s, axis=0, keepdims=True)
        acc_scr[u] = _dot(va_ref[e, qi * bpq + cb], jnp.exp2(s - m).astype(BF16))
        ms.append(m)

    def step(j, ms):
        nxt = jnp.minimum(j + 1, n_off)
        out = []
        for u, (e, cb) in enumerate(units):
            s = s_scr[u]
            dist = lane + ((qi * bpq + cb - j) * B).astype(F32)
            col = sel_ref[e, pl.ds(j, 1), cb * B:(cb + 1) * B] - slope_of(e) * dist
            m_new = jnp.maximum(ms[u], jnp.max(s, axis=0, keepdims=True) + col)
            p = jnp.exp2(s - (m_new - col)).astype(BF16)
            acc_scr[u] = jnp.exp2(ms[u] - m_new) * acc_scr[u] + _dot(va_ref[e, j], p)
            s_scr[u] = _dot(ka_ref[e, nxt], q_of(e, cb))
            out.append(m_new)
        return tuple(out)

    def body(t, ms):
        j = j_start + 2 * t
        return step(j + 1, step(j, ms))

    lax.fori_loop(0, (n_off - j_start + 1) // 2, body, tuple(ms))
    for cb in range(bpq):
        outs = []
        for e in range(HEADS_PER_STEP):
            acc = acc_scr[e * bpq + cb]
            outs.append((acc[:Dh, :] / acc[Dh:Dh + 1, :]).T)
        o_ref[0, cb * B:(cb + 1) * B, :] = jnp.concatenate(outs, axis=1).astype(o_ref.dtype)


def _alibi_keep_blocks(slopes, qn2, kn2, nb):
    qk = jnp.sqrt(qn2 * kn2) * 1.02
    need = (2.0 * qk + UNDERFLOW_LOG2) / (slopes * LOG2E)
    w = jnp.ceil((need - 1.0) / MOBA_BLOCK)
    return jnp.clip(w, 1.0, float(nb)).astype(jnp.int32)


def _attn_prompt(keep, qa, ka, va, sel, slopes, q_sample, ck, page_table):
    H, nb, B, _ = ka.shape
    L = qa.shape[2]
    tq = ATTN_TQ
    hp = HEADS_PER_STEP
    nq = L // tq
    n_steps = (H // hp) * nq
    S, n_pages = page_table.shape
    page = ck.shape[3]
    ppb = MOBA_BLOCK // page
    pps = (S * n_pages) // n_steps
    assert pps * n_steps == S * n_pages and n_pages % pps == 0 and pps % ppb == 0 and pps // ppb <= LANES
    W = H * HEAD_DIM
    ind = (jnp.arange(H)[:, None] == jnp.arange(W)[None, :] // HEAD_DIM).astype(BF16)
    q_rep = jnp.broadcast_to(q_sample.reshape(S, W, 1), (S, W, LANES))
    once = pl.Buffered(1)
    attn, gates = pl.pallas_call(
        functools.partial(_attn_prompt_kernel, tq=tq, pps=pps, ppb=ppb),
        grid_spec=pltpu.PrefetchScalarGridSpec(
            num_scalar_prefetch=2,
            grid=(H // hp, nq),
            in_specs=[pl.BlockSpec((hp, 2 * HEAD_DIM, tq), lambda h, i, w, pt: (h, 0, i)),
                      pl.BlockSpec((hp, nb, B, LANES), lambda h, i, w, pt: (h, 0, 0, 0), pipeline_mode=once),
                      pl.BlockSpec((hp, nb, V_ROWS, B), lambda h, i, w, pt: (h, 0, 0, 0), pipeline_mode=once),
                      pl.BlockSpec((hp, nb, tq), lambda h, i, w, pt: (h, 0, i)),
                      pl.BlockSpec((hp, 1, B), lambda h, i, w, pt: (h, 0, 0)),
                      pl.BlockSpec((1, W, LANES), lambda h, i, w, pt: (((h * nq + i) * pps) // n_pages, 0, 0)),
                      pl.BlockSpec((H, W), lambda h, i, w, pt: (0, 0)),
                      pl.BlockSpec(memory_space=pl.ANY)],
            out_specs=[pl.BlockSpec((1, tq, hp * HEAD_DIM), lambda h, i, w, pt: (h, i, 0)),
                       pl.BlockSpec((1, H, LANES), lambda h, i, w, pt: (h * nq + i, 0, 0))],
            scratch_shapes=[pltpu.VMEM((hp * tq // B, B, B), F32), pltpu.VMEM((hp * tq // B, B, B), F32),
                            pltpu.VMEM((hp * tq // B, V_ROWS, B), F32),
                            pltpu.VMEM((2, pps) + ck.shape[1:], F32), pltpu.SemaphoreType.DMA((2,))],
        ),
        out_shape=[jax.ShapeDtypeStruct((H // hp, L, hp * HEAD_DIM), BF16),
                   jax.ShapeDtypeStruct((n_steps, H, LANES), F32)],
        compiler_params=_cparams(2, V7X_VMEM_LIMIT),
        name="attn_prompt",
    )(keep, page_table.reshape(-1), qa, ka, va, sel, slopes, q_rep, ind, ck)
    bps = pps // ppb
    gates = jnp.transpose(gates[:, :, :bps], (0, 2, 1)).reshape(S, n_pages // ppb, H)
    return attn, gates


def _ssm_tables(a_re, a_im, log_step, b_re, b_im):
    T = SSM_CHUNK
    dt = jnp.exp(log_step)[:, None]
    j = jnp.arange(T + 1, dtype=F32)[:, None, None]
    mag = jnp.exp(a_re * dt * j)
    pw_re = mag * jnp.cos(a_im * dt * j)
    pw_im = mag * jnp.sin(a_im * dt * j)
    abar_re, abar_im = pw_re[1], pw_im[1]
    den = a_re * a_re + a_im * a_im
    nr = abar_re - 1.0
    ni = abar_im
    coef_re = (nr * a_re + ni * a_im) / den
    coef_im = (ni * a_re - nr * a_im) / den
    bb_re = coef_re[..., None] * b_re - coef_im[..., None] * b_im
    bb_im = coef_re[..., None] * b_im + coef_im[..., None] * b_re
    return dict(pw_re=pw_re, pw_im=pw_im, bb_re=bb_re, bb_im=bb_im, abar_re=abar_re, abar_im=abar_im)


def _spread_groups(compact, rows_per_group, cols_per_group, col_outer):
    gs = GROUPS_PER_SLAB
    rows = compact.shape[-2]
    src = jnp.arange(col_outer * cols_per_group)
    dst = jnp.arange(col_outer * gs * cols_per_group)
    same_outer = src[:, None] // cols_per_group == dst[None, :] // (gs * cols_per_group)
    same_c = src[:, None] % cols_per_group == dst[None, :] % cols_per_group
    rep = (same_outer & same_c).astype(BF16)
    row_group = (jnp.arange(rows) // rows_per_group) % gs
    col_group = (dst // cols_per_group) % gs
    wide = jnp.dot(compact, rep, preferred_element_type=F32)
    return jnp.where(row_group[:, None] == col_group[None, :], wide, 0.0).astype(BF16)


def _ssm_chunk_tables(tb, c_re, c_im, d):
    pw_re, pw_im, bb_re, bb_im = tb["pw_re"], tb["pw_im"], tb["bb_re"], tb["bb_im"]
    T = SSM_CHUNK
    G, N, P = bb_re.shape
    x_re = pw_re[:T, :, :, None] * bb_re[None] - pw_im[:T, :, :, None] * bb_im[None]
    x_im = pw_re[:T, :, :, None] * bb_im[None] + pw_im[:T, :, :, None] * bb_re[None]
    kj = (jnp.einsum("gpn,jgnq->jgqp", c_re, x_re, precision=HI)
          - jnp.einsum("gpn,jgnq->jgqp", c_im, x_im, precision=HI))
    kj = kj.at[0].add(jnp.eye(P, dtype=F32)[None] * d[:, :, None])
    gs = GROUPS_PER_SLAB
    kpad = jnp.concatenate([jnp.zeros_like(kj[:1]), kj], axis=0)
    kc = jnp.stack([kpad[:T], kpad[1:]], axis=3).reshape(T, SLABS, gs, P, 2, P)
    kc = jnp.transpose(kc, (1, 0, 2, 3, 4, 5)).reshape(SLABS, T * LANES, 2 * P)
    lag = _spread_groups(kc.astype(BF16), P, P, 2)
    xc = jnp.stack([x_re[::-1], x_im[::-1]], axis=2).reshape(T, SLABS, gs, 2, N, P)
    xc = jnp.transpose(xc, (1, 0, 2, 5, 3, 4)).reshape(SLABS, T * LANES, 2 * N)
    f = _spread_groups(xc.astype(BF16), P, N, 2)
    cr = jnp.transpose(c_re, (0, 2, 1))[None]
    ci = jnp.transpose(c_im, (0, 2, 1))[None]
    ar = pw_re[1:T + 1, :, :, None]
    ai = pw_im[1:T + 1, :, :, None]
    ec = jnp.stack([cr * ar - ci * ai, -(cr * ai + ci * ar)], axis=1)
    ec = ec.reshape(T // 2, 2, 2, SLABS, gs, N, P)
    ec = jnp.transpose(ec, (3, 0, 2, 4, 5, 1, 6)).reshape(SLABS, T // 2, 2 * gs * N, 2 * P)
    e = _spread_groups(ec.astype(BF16), N, P, 2)
    a16_re = pw_re[T].reshape(1, G * N)
    a16_im = pw_im[T].reshape(1, G * N)
    return lag, f, e, a16_re, a16_im


def _chunk_steps(u_ref, rows):
    return [u_ref[0, pl.ds(s, rows, stride=SSM_CHUNK), :].astype(BF16) for s in range(SSM_CHUNK)]


def _ssm_chunk_in_kernel(u_ref, f_ref, bre_ref, bim_ref, *, rows):
    b = _dot(jnp.concatenate(_chunk_steps(u_ref, rows), axis=1), f_ref[0])
    half = b.shape[1] // 2
    bre_ref[...] = b[:, :half]
    bim_ref[...] = b[:, half:]


def _ssm_scan_kernel(bre_ref, bim_ref, are_ref, aim_ref, sre_ref, sim_ref, fre_ref, fim_ref):
    nc = bre_ref.shape[0]
    ar = are_ref[...]
    ai = aim_ref[...]

    def body(c8, carry):
        sr, si = carry
        r0 = pl.multiple_of(c8 * 8, 8)
        br = bre_ref[pl.ds(r0, 8), :]
        bi = bim_ref[pl.ds(r0, 8), :]
        rows_r, rows_i = [], []
        for r in range(8):
            rows_r.append(sr)
            rows_i.append(si)
            sr, si = (ar * sr - ai * si + br[r:r + 1, :], ar * si + ai * sr + bi[r:r + 1, :])
        sre_ref[pl.ds(r0, 8), :] = jnp.concatenate(rows_r, axis=0)
        sim_ref[pl.ds(r0, 8), :] = jnp.concatenate(rows_i, axis=0)
        return sr, si

    z = jnp.zeros(are_ref.shape, F32)
    sr, si = lax.fori_loop(0, nc // 8, body, (z, z))
    fre_ref[...] = sr
    fim_ref[...] = si


def _ssm_chunk_out_kernel(u_ref, lag_ref, e_ref, sre_ref, sim_ref, y_ref, *, rows):
    us = _chunk_steps(u_ref, rows)
    s = jnp.concatenate([sre_ref[...], sim_ref[...]], axis=1).astype(BF16)
    for pair in range(SSM_CHUNK // 2):
        tau = 2 * pair
        lhs = jnp.concatenate(us[tau + 1::-1], axis=1)
        y2 = _dot(lhs, lag_ref[0, :LANES * (tau + 2), :]) + _dot(s, e_ref[0, pair])
        y_ref[0, pl.ds(tau, rows, stride=SSM_CHUNK), :] = y2[:, :LANES]
        y_ref[0, pl.ds(tau + 1, rows, stride=SSM_CHUNK), :] = y2[:, LANES:]


SSM_ROWS = 512


def _ssm_prompt(u4, lag, f, e, a16_re, a16_im):
    L = u4.shape[1]
    nc = L // SSM_CHUNK
    rows = min(SSM_ROWS, nc)
    GN = N_SSM_GROUPS * SSM_STATE
    SW = GROUPS_PER_SLAB * SSM_STATE
    st = jax.ShapeDtypeStruct((nc, GN), F32)
    slab_rows = pl.BlockSpec((1, rows * SSM_CHUNK, LANES), lambda s, r: (s, r, 0))
    state_cols = pl.BlockSpec((rows, SW), lambda s, r: (r, s))
    b_re, b_im = pl.pallas_call(
        functools.partial(_ssm_chunk_in_kernel, rows=rows),
        grid=(SLABS, nc // rows),
        in_specs=[slab_rows, pl.BlockSpec((1,) + f.shape[1:], lambda s, r: (s, 0, 0))],
        out_specs=[state_cols, state_cols],
        out_shape=[st, st],
        compiler_params=_cparams(2, V7X_VMEM_LIMIT),
        name="ssm_chunk_in",
    )(u4, f)
    fin = jax.ShapeDtypeStruct((1, GN), F32)
    s_re, s_im, f_re, f_im = pl.pallas_call(
        _ssm_scan_kernel,
        out_shape=[st, st, fin, fin],
        compiler_params=pltpu.CompilerParams(vmem_limit_bytes=V7X_VMEM_LIMIT),
        name="ssm_scan",
    )(b_re, b_im, a16_re, a16_im)
    y4 = pl.pallas_call(
        functools.partial(_ssm_chunk_out_kernel, rows=rows),
        grid=(SLABS, nc // rows),
        in_specs=[slab_rows,
                  pl.BlockSpec((1,) + lag.shape[1:], lambda s, r: (s, 0, 0)),
                  pl.BlockSpec((1,) + e.shape[1:], lambda s, r: (s, 0, 0, 0)),
                  state_cols, state_cols],
        out_specs=slab_rows,
        out_shape=jax.ShapeDtypeStruct(u4.shape, F32),
        compiler_params=_cparams(2, V7X_VMEM_LIMIT),
        name="ssm_chunk_out",
    )(u4, lag, e, s_re, s_im)
    return y4, f_re, f_im


FF_CHUNK = 256


def _mix_and_prenorm(x, attn_bf, y, wglu_ref, bglu_ref, wouta_ref, wouts_ref, gpost_ref, gpre_ref):
    z = _gelu_tanh(y)
    ssm = z * _sigmoid(_dot(z.astype(BF16), wglu_ref[...]) + bglu_ref[...])
    mix = _dot(attn_bf, wouta_ref[...]) + _dot(ssm.astype(BF16), wouts_ref[...])
    x1 = x + _rms(mix, gpost_ref[...])
    h2 = _rms(x1, gpre_ref[...]).astype(BF16)
    return x1, h2


def _ffn_prompt_kernel(x_ref, attn_ref, y_ref, wglu_ref, bglu_ref, wouta_ref, wouts_ref, gpost_ref, gpre_ref,
                       wgate_ref, wup_ref, cw_ref, cb_ref, wdown_ref, gfpost_ref, out_ref, conv_ref, tail_s, act_s,
                       *, tl, dff):
    i = pl.program_id(0)

    @pl.when(i == 0)
    def _():
        tail_s[...] = jnp.zeros_like(tail_s)

    attn = jnp.concatenate([attn_ref[s] for s in range(attn_ref.shape[0])], axis=1)
    y = jnp.concatenate([y_ref[s] for s in range(SLABS)], axis=1)
    x1, h2 = _mix_and_prenorm(x_ref[...], attn, y, wglu_ref, bglu_ref, wouta_ref, wouts_ref, gpost_ref, gpre_ref)
    row = lax.broadcasted_iota(jnp.int32, (tl, FF_CHUNK), 0)
    for c in range(dff // FF_CHUNK):
        cs = slice(c * FF_CHUNK, (c + 1) * FF_CHUNK)
        g = _dot(h2, wgate_ref[:, cs])
        up = _dot(h2, wup_ref[:, cs])
        tail = tail_s[c]
        p1 = tail[7:8, :]
        p2 = tail[6:7, :]
        g1 = jnp.where(row == 0, p1, pltpu.roll(g, 1, 0))
        g2 = jnp.where(row == 0, p2, jnp.where(row == 1, p1, pltpu.roll(g, 2, 0)))
        gc = cw_ref[0:1, cs] * g2 + cw_ref[1:2, cs] * g1 + cw_ref[2:3, cs] * g + cb_ref[:, cs]
        act_s[:, cs] = (_gelu_tanh(gc) * up).astype(BF16)
        tail_s[c] = g[tl - 8:, :]
        conv_ref[:, cs] = g[tl - 8:, :]
    f = _dot(act_s[...], wdown_ref[...])
    out_ref[...] = x1 + _rms(f, gfpost_ref[...])


def _ffn_sample_kernel(x_ref, attn_ref, y_ref, b0_ref, b1_ref, wglu_ref, bglu_ref, wouta_ref, wouts_ref,
                       gpost_ref, gpre_ref, wgate_ref, wup_ref, cw_ref, cb_ref, wdown_ref, gfpost_ref,
                       out_ref, g_ref, *, dff):
    x1, h2 = _mix_and_prenorm(x_ref[...], attn_ref[...], y_ref[...], wglu_ref, bglu_ref, wouta_ref,
                              wouts_ref, gpost_ref, gpre_ref)
    f = jnp.zeros(x1.shape, F32)
    for c in range(dff // FF_CHUNK):
        cs = slice(c * FF_CHUNK, (c + 1) * FF_CHUNK)
        g = _dot(h2, wgate_ref[:, cs])
        up = _dot(h2, wup_ref[:, cs])
        gc = (cw_ref[0:1, cs] * b0_ref[:, cs] + cw_ref[1:2, cs] * b1_ref[:, cs] + cw_ref[2:3, cs] * g
              + cb_ref[:, cs])
        act = (_gelu_tanh(gc) * up).astype(BF16)
        f = f + _dot(act, wdown_ref[cs, :])
        g_ref[:, cs] = g
    out_ref[...] = x1 + _rms(f, gfpost_ref[...])


def _weight_specs(dff):
    c2 = lambda *_: (0, 0)
    full = lambda r, c: pl.BlockSpec((r, c), c2, pipeline_mode=pl.Buffered(1))
    return [full(SSM_WIDTH, SSM_WIDTH), full(1, SSM_WIDTH), full(ATTN_WIDTH, D_MODEL), full(SSM_WIDTH, D_MODEL),
            full(1, D_MODEL), full(1, D_MODEL), full(D_MODEL, dff), full(D_MODEL, dff), full(CONV_W, dff),
            full(1, dff), full(dff, D_MODEL), full(1, D_MODEL)]


def _ffn_prompt(x, attn2, y4, weights):
    L = x.shape[0]
    tl = ROW_TILE
    dff = weights[6].shape[1]
    rows = lambda w: pl.BlockSpec((tl, w), lambda i: (i, 0))
    slabs = lambda a: pl.BlockSpec((a.shape[0], tl, LANES), lambda i: (0, i, 0))
    return pl.pallas_call(
        functools.partial(_ffn_prompt_kernel, tl=tl, dff=dff),
        grid=(L // tl,),
        in_specs=[rows(D_MODEL), slabs(attn2), slabs(y4)] + _weight_specs(dff),
        out_specs=[rows(D_MODEL), pl.BlockSpec((8, dff), lambda i: (0, 0))],
        out_shape=[jax.ShapeDtypeStruct((L, D_MODEL), F32), jax.ShapeDtypeStruct((8, dff), F32)],
        scratch_shapes=[pltpu.VMEM((dff // FF_CHUNK, 8, FF_CHUNK), F32), pltpu.VMEM((tl, dff), BF16)],
        compiler_params=_cparams(1, V7X_VMEM_LIMIT),
        name="ffn_prompt",
    )(x, attn2, y4, *weights)


def _ffn_sample(x, attn_bf, y, buf0, buf1, weights):
    nb = x.shape[0]
    dff = weights[6].shape[1]
    rows = lambda w: pl.BlockSpec((nb, w), lambda i: (0, 0))
    return pl.pallas_call(
        functools.partial(_ffn_sample_kernel, dff=dff),
        grid=(1,),
        in_specs=[rows(D_MODEL), rows(ATTN_WIDTH), rows(SSM_WIDTH), rows(dff), rows(dff)] + _weight_specs(dff),
        out_specs=[rows(D_MODEL), rows(dff)],
        out_shape=[jax.ShapeDtypeStruct((nb, D_MODEL), F32), jax.ShapeDtypeStruct((nb, dff), F32)],
        compiler_params=_cparams(1, V7X_VMEM_LIMIT),
        name="ffn_sample",
    )(x, attn_bf, y, buf0, buf1, *weights)


def _inproj_sample_kernel(x_ref, g_ref, w_ref, o_ref):
    o_ref[...] = _dot(_rms(x_ref[...], g_ref[...]).astype(BF16), w_ref[...])


def _inproj_sample(x, g, w_bf):
    nb = x.shape[0]
    return pl.pallas_call(
        _inproj_sample_kernel,
        out_shape=jax.ShapeDtypeStruct((nb, w_bf.shape[1]), F32),
        compiler_params=pltpu.CompilerParams(vmem_limit_bytes=V7X_VMEM_LIMIT),
        name="inproj_sample",
    )(x, g, w_bf)


def _top_sample_kernel(g_ref, top_ref):
    gate = g_ref[...]
    nb = gate.shape[1]
    blk = lax.broadcasted_iota(jnp.int32, gate.shape, 1).astype(F32)
    picks = [first for first, _ in _top_blocks(gate, blk, nb, axis=1)]
    top_ref[...] = jnp.concatenate(picks, axis=1).astype(jnp.int32)


def _top_sample(gates):
    S, nb, H = gates.shape
    return pl.pallas_call(
        _top_sample_kernel,
        out_shape=jax.ShapeDtypeStruct((S, MOBA_TOPK, H), jnp.int32),
        name="top_sample",
    )(gates)


def _attn_sample_kernel(pt_ref, top_ref, q_ref, kn_ref, vn_ref, slope_ref, ck_ref, cv_ref, o_ref,
                        kbuf, vbuf, sems, *, n_sel, n_pages, page, past_len):
    H = N_HEADS
    s_i = pl.program_id(0)
    n_seq = pl.num_programs(0)
    scale = HEAD_DIM ** -0.5
    ppb = MOBA_BLOCK // page
    off = lax.broadcasted_iota(jnp.int32, (1, page), 1)

    def block_of(seq, h, r):
        return top_ref[(seq * MOBA_TOPK + r // ppb) * H + h]

    def copies(seq, slot):
        out = []
        for h in range(H):
            for r in range(n_sel):
                pg = pt_ref[seq * n_pages + block_of(seq, h, r) * ppb + r % ppb]
                out.append(pltpu.make_async_copy(ck_ref.at[pg, h], kbuf.at[slot, h * n_sel + r], sems.at[0, slot]))
                out.append(pltpu.make_async_copy(cv_ref.at[pg, h], vbuf.at[slot, h * n_sel + r], sems.at[1, slot]))
        return out

    slot = s_i % 2

    @pl.when(s_i == 0)
    def _():
        for c in copies(s_i, slot):
            c.start()

    @pl.when(s_i + 1 < n_seq)
    def _():
        for n, c in enumerate(copies(s_i + 1, 1 - slot)):
            c.start(priority=n % 2)

    for c in copies(s_i, slot):
        c.wait()

    q = q_ref[0]
    qk = []
    for h in range(H):
        q8 = jnp.broadcast_to(q[h:h + 1, :], (8, HEAD_DIM)).astype(BF16)
        kt = jnp.concatenate([kbuf[slot, h * n_sel + r] for r in range(n_sel)], axis=1).astype(BF16)
        qk.append(_dot(q8, kt)[0:1, :])
    scores = []
    for r in range(n_sel):
        dist_rows = [(past_len - (block_of(s_i, h, r) * MOBA_BLOCK + (r % ppb) * page + off)).astype(F32)
                     for h in range(H)]
        qk_r = jnp.concatenate([qk[h][:, r * page:(r + 1) * page] for h in range(H)], axis=0)
        scores.append(qk_r * scale - slope_ref[...] * jnp.concatenate(dist_rows, axis=0))
    s_own = jnp.sum(_round_bf16(q) * _round_bf16(kn_ref[0]), axis=1, keepdims=True) * scale
    m = s_own
    for s in scores:
        m = jnp.maximum(m, jnp.max(s, axis=1, keepdims=True))
    p_own = jnp.exp(s_own - m)
    ps = [jnp.exp(s - m) for s in scores]
    l = p_own
    for p in ps:
        l = l + jnp.sum(p, axis=1, keepdims=True)
    inv = 1.0 / l
    pn = [_round_bf16(p * inv) for p in ps]
    pn_own = _round_bf16(p_own * inv)
    vn = _round_bf16(vn_ref[0])
    for h in range(H):
        p8 = jnp.broadcast_to(jnp.concatenate([p[h:h + 1, :] for p in pn], axis=1), (8, n_sel * page)).astype(BF16)
        vt = jnp.concatenate([vbuf[slot, h * n_sel + r] for r in range(n_sel)], axis=1).astype(BF16)
        o_ref[0, h:h + 1, :] = _dot_nt(p8, vt)[0:1, :] + pn_own[h:h + 1, :] * vn[h:h + 1, :]


def _attn_sample(q, k_new, v_new, ck, cv, page_table, top, slopes_page):
    n_pool, H, Dh, page = ck.shape
    S, n_pages = page_table.shape
    ppb = MOBA_BLOCK // page
    n_sel = MOBA_TOPK * ppb
    past_len = n_pages * page

    col = pl.BlockSpec((1, H, Dh), lambda s, pt, tp: (s, 0, 0))
    cols = lambda a: a.reshape(S, H, Dh)
    hbm = pl.BlockSpec(memory_space=pl.ANY)
    tiles = pltpu.VMEM((2, H * n_sel, Dh, page), F32)
    out = pl.pallas_call(
        functools.partial(_attn_sample_kernel, n_sel=n_sel, n_pages=n_pages, page=page, past_len=past_len),
        grid_spec=pltpu.PrefetchScalarGridSpec(
            num_scalar_prefetch=2,
            grid=(S,),
            in_specs=[col, col, col, pl.BlockSpec((H, page), lambda s, pt, tp: (0, 0)), hbm, hbm],
            out_specs=col,
            scratch_shapes=[tiles, tiles, pltpu.SemaphoreType.DMA((2, 2))],
        ),
        out_shape=jax.ShapeDtypeStruct((S, H, Dh), F32),
        compiler_params=_cparams(1),
        name="attn_sample",
    )(page_table.reshape(-1), top.reshape(-1), cols(q), cols(k_new), cols(v_new), slopes_page, ck, cv)
    return out.reshape(S, H * Dh)


def _ssm_sample_kernel(u_ref, sre_ref, sim_ref, are_ref, aim_ref, f_ref, e_ref, lag_ref, y_ref, nre_ref, nim_ref):
    SW = GROUPS_PER_SLAB * SSM_STATE
    for s in range(SLABS):
        cs = slice(s * SW, (s + 1) * SW)
        ls = slice(s * LANES, (s + 1) * LANES)
        ub = u_ref[:, ls].astype(BF16)
        s0r, s0i = sre_ref[:, cs], sim_ref[:, cs]
        ar, ai = are_ref[:, cs], aim_ref[:, cs]
        b = _dot(ub, f_ref[s])
        nre_ref[:, cs] = ar * s0r - ai * s0i + b[:, :SW]
        nim_ref[:, cs] = ar * s0i + ai * s0r + b[:, SW:]
        s0 = jnp.concatenate([s0r, s0i], axis=1).astype(BF16)
        y_ref[:, ls] = _dot(s0, e_ref[s, 0, :, :LANES]) + _dot(ub, lag_ref[s, :, :LANES])


def _ssm_sample(u, s_re, s_im, tb, lag, f, e):
    S = u.shape[0]
    GN = N_SSM_GROUPS * SSM_STATE
    st = jax.ShapeDtypeStruct((S, GN), F32)
    whole = lambda a: pl.BlockSpec(a.shape, lambda i: (0,) * a.ndim)
    args = (u, s_re.reshape(S, GN), s_im.reshape(S, GN), tb["abar_re"].reshape(1, GN), tb["abar_im"].reshape(1, GN))
    return pl.pallas_call(
        _ssm_sample_kernel,
        grid=(1,),
        in_specs=[whole(a) for a in args]
        + [pl.BlockSpec((SLABS, LANES, f.shape[2]), lambda i: (0, SSM_CHUNK - 1, 0)),
           pl.BlockSpec((SLABS, 1) + e.shape[2:], lambda i: (0, 0, 0, 0)),
           pl.BlockSpec((SLABS, LANES, lag.shape[2]), lambda i: (0, 1, 0))],
        out_specs=[pl.BlockSpec((S, SSM_WIDTH), lambda i: (0, 0)), pl.BlockSpec((S, GN), lambda i: (0, 0)),
                   pl.BlockSpec((S, GN), lambda i: (0, 0))],
        out_shape=[jax.ShapeDtypeStruct((S, SSM_WIDTH), F32), st, st],
        compiler_params=_cparams(1, V7X_VMEM_LIMIT),
        name="ssm_sample",
    )(*args, f, e, lag)


def _layer(x, xs, cache_k, cache_v, page_table, s_re, s_im, conv_buf, lw):
    S = xs.shape[0]
    proj = _inproj_sample(xs, lw["g_mix_pre"], lw["w_in"])
    q = proj[:, :ATTN_WIDTH]
    k = proj[:, ATTN_WIDTH:2 * ATTN_WIDTH]
    v = proj[:, 2 * ATTN_WIDTH:3 * ATTN_WIDTH]
    u = proj[:, 3 * ATTN_WIDTH:]
    ck = jnp.transpose(cache_k, (0, 2, 3, 1))
    cv = jnp.transpose(cache_v, (0, 2, 3, 1))

    kt, vt, u4, qa, ka, va, sel, qn2, kn2 = _inproj_prompt(x, lw["g_mix_pre"], lw["w_in"], lw["k_aug"])
    keep = _alibi_keep_blocks(lw["slopes"], qn2[:, 0], kn2[:, 0], ka.shape[1])
    attn2, gates = _attn_prompt(keep, qa, ka, va, sel, lw["slopes_q"], q, ck, page_table)
    y4, f_re, f_im = _ssm_prompt(u4, *lw["ssm_chunk"])
    out, conv = _ffn_prompt(x, attn2, y4, lw["ffn"])
    prompt = (out, kt, vt, f_re.reshape(N_SSM_GROUPS, SSM_STATE), f_im.reshape(N_SSM_GROUPS, SSM_STATE),
              conv[8 - (CONV_W - 1):])

    attn = _attn_sample(q, k, v, ck, cv, page_table, _top_sample(gates), lw["slopes_page"])
    y, n_re, n_im = _ssm_sample(u, s_re, s_im, lw["ssm_tb"], *lw["ssm_chunk"][:3])
    outs, g = _ffn_sample(xs, attn.astype(BF16), y, conv_buf[:, 0], conv_buf[:, 1], lw["ffn"])
    conv_new = jnp.stack([conv_buf[:, 1], g], axis=1)
    sample = (outs, k, v, n_re.reshape(S, N_SSM_GROUPS, SSM_STATE), n_im.reshape(S, N_SSM_GROUPS, SSM_STATE),
              conv_new)
    return prompt, sample


def _alibi_key_table(slopes):
    off = jnp.arange(MOBA_BLOCK, dtype=F32)[None, :] * (slopes * LOG2E)[:, None]
    to_bf16 = lambda a: lax.reduce_precision(a, exponent_bits=8, mantissa_bits=7)
    t0 = to_bf16(off)
    t1 = to_bf16(off - t0)
    t2 = to_bf16(off - t0 - t1)
    terms = jnp.stack([t0, t1, t2], axis=-1).astype(BF16)
    half = jnp.pad(terms, ((0, 0), (0, 0), (0, HEAD_DIM - 3)))
    zero = jnp.zeros_like(half)
    odd = (jnp.arange(N_HEADS) % 2 == 1)[:, None, None]
    return jnp.where(odd, jnp.concatenate([half, zero], axis=-1), jnp.concatenate([zero, half], axis=-1))


def kernel(x_prompt, x_sample, cache_k, cache_v, page_table, state_ssm_re, state_ssm_im, state_conv,
           norm_mix_pre, norm_mix_post, w_in, ssm_a_re, ssm_a_im, ssm_log_step, ssm_b_re, ssm_b_im,
           ssm_c_re, ssm_c_im, ssm_d, w_glu, b_glu, w_out, norm_ffn_pre, norm_ffn_post,
           w_gate, w_up, conv_w, conv_b, w_down):
    depth = w_in.shape[0]
    bp, lp_len = x_prompt.shape[:2]
    bs, ls_len = x_sample.shape[:2]
    page = cache_k.shape[2]
    assert bp == 1 and ls_len == 1 and lp_len % ATTN_TQ == 0 and page == LANES
    slopes = jnp.exp2(-8.0 * jnp.arange(1, N_HEADS + 1, dtype=F32) / N_HEADS)
    hp = x_prompt[0]
    hs = x_sample[:, 0]
    outs = [[] for _ in range(10)]
    for l in range(depth):
        tb = _ssm_tables(ssm_a_re[l], ssm_a_im[l], ssm_log_step[l], ssm_b_re[l], ssm_b_im[l])
        row = lambda a: a[l].reshape(1, -1)
        lw = dict(
            g_mix_pre=row(norm_mix_pre), w_in=w_in[l].astype(BF16),
            k_aug=_alibi_key_table(slopes),
            slopes=slopes,
            slopes_q=jnp.broadcast_to((slopes * LOG2E)[:, None, None], (N_HEADS, 1, MOBA_BLOCK)),
            slopes_page=jnp.broadcast_to(slopes[:, None], (N_HEADS, page)),
            ssm_tb=tb,
            ssm_chunk=_ssm_chunk_tables(tb, ssm_c_re[l], ssm_c_im[l], ssm_d[l]),
            ffn=[w_glu[l].astype(BF16), row(b_glu), w_out[l, :ATTN_WIDTH].astype(BF16),
                 w_out[l, ATTN_WIDTH:].astype(BF16), row(norm_mix_post), row(norm_ffn_pre),
                 w_gate[l].astype(BF16), w_up[l].astype(BF16), conv_w[l], row(conv_b),
                 w_down[l].astype(BF16), row(norm_ffn_post)],
        )
        (hp, ktp, vtp, sr, si, cp), (hs, ks, vs, srs, sis, cs) = _layer(
            hp, hs, cache_k[l], cache_v[l], page_table, state_ssm_re[l], state_ssm_im[l], state_conv[l], lw)
        outs[0].append(jnp.transpose(ktp, (2, 0, 1))[None])
        outs[1].append(jnp.transpose(vtp, (2, 0, 1))[None])
        outs[4].append(sr[None])
        outs[5].append(si[None])
        outs[8].append(cp[None])
        sr, si = srs, sis
        outs[2].append(ks.reshape(bs, ls_len, N_HEADS, HEAD_DIM))
        outs[3].append(vs.reshape(bs, ls_len, N_HEADS, HEAD_DIM))
        outs[6].append(sr)
        outs[7].append(si)
        outs[9].append(cs)
    return (hp[None], hs[:, None], *[jnp.stack(o) for o in outs])
```
